```python
import math
import jax, jax.numpy as jnp
from jax import lax
import numpy as np

D_MODEL = 1024
BATCH = 8
SEQ = 4096
DEPTH = 2

MIX_WIDTH = D_MODEL
ATT_WIDTH = MIX_WIDTH // 2
FOURIER_WIDTH = MIX_WIDTH - ATT_WIDTH
N_HEADS = 8
HEAD_DIM = ATT_WIDTH // (2 * N_HEADS)
V_HEAD_DIM = 2 * HEAD_DIM
N_FGROUPS = 4
FGROUP_DIM = FOURIER_WIDTH // N_FGROUPS
IN_COLS = 3 * ATT_WIDTH + FOURIER_WIDTH
D_FF = ((8 * D_MODEL + 3 * 256 - 1) // (3 * 256)) * 256
Q_BLOCK = 128
LN_EPS = 1e-5
SUBLN_EPS = 1e-5
ALPHA = (2.0 * DEPTH) ** 0.25
BETA = (8.0 * DEPTH) ** -0.25
LAMBDA_STD = 0.1

kernel_name = 'hymba_diffattn_fnet_deepnorm_encoder'


def layer_norm(x, g, b):
    x32 = x.astype(jnp.float32)
    mu = jnp.mean(x32, axis=-1, keepdims=True)
    var = jnp.mean(jnp.square(x32 - mu), axis=-1, keepdims=True)
    return ((x32 - mu) * lax.rsqrt(var + LN_EPS) * g + b).astype(x.dtype)


def lambda_init_fn(layer_idx):
    return 0.8 - 0.6 * math.exp(-0.3 * layer_idx)


def diff_attention(q, k, v, lam, lam_init, subln_g):
    B, S = q.shape[0], q.shape[1]
    nb = S // Q_BLOCK
    slopes = jnp.exp2(-8.0 * jnp.arange(1, N_HEADS + 1, dtype=jnp.float32) / N_HEADS)
    kpos = jnp.arange(S, dtype=jnp.float32)
    scale = HEAD_DIM ** -0.5
    qb = q.reshape(B, nb, Q_BLOCK, N_HEADS, 2, HEAD_DIM).transpose(1, 0, 2, 3, 4, 5)
    starts = jnp.arange(nb, dtype=jnp.float32) * Q_BLOCK

    def block(args):
        qi, start = args
        s = jnp.einsum('bqhcd,bkhcd->bhcqk', qi, k).astype(jnp.float32) * scale
        qpos = start + jnp.arange(Q_BLOCK, dtype=jnp.float32)
        dist = jnp.abs(qpos[:, None] - kpos[None, :])
        s = s - slopes[:, None, None, None] * dist[None, None]
        p = jax.nn.softmax(s, axis=-1)
        a = p[:, :, 0] - lam * p[:, :, 1]
        o = jnp.einsum('bhqk,bkhd->bqhd', a.astype(v.dtype), v).astype(jnp.float32)
        o = o * lax.rsqrt(jnp.mean(jnp.square(o), axis=-1, keepdims=True) + SUBLN_EPS)
        o = o * subln_g * (1.0 - lam_init)
        return o.astype(v.dtype)

    out = lax.map(block, (qb, starts))
    return out.transpose(1, 0, 2, 3, 4).reshape(B, S, N_HEADS * V_HEAD_DIM)


def fourier_mix(u, w_f, b_f):
    B, S = u.shape[0], u.shape[1]
    ug = u.reshape(B, S, N_FGROUPS, FGROUP_DIM).astype(jnp.float32)
    f = jnp.fft.fft(ug, axis=3, norm='ortho')
    f = jnp.fft.fft(f, axis=1, norm='ortho').real.astype(u.dtype)
    y = jnp.einsum('bsgc,gcd->bsgd', f, w_f)
    return y.reshape(B, S, FOURIER_WIDTH) + b_f


def swiglu(h, w_gu, w_down):
    gu = jnp.einsum('bsd,df->bsf', h, w_gu)
    g, up = gu[..., :D_FF], gu[..., D_FF:]
    return jnp.einsum('bsf,fd->bsd', jax.nn.silu(g) * up, w_down)


def setup_inputs(seed: int = 0) -> dict:
    key = jax.random.key(seed)
    ks = jax.random.split(key, 16)
    f32 = jnp.float32
    nrm = lambda k, shape: jax.random.normal(k, shape, f32)
    return {
        'x': nrm(ks[0], (BATCH, SEQ, D_MODEL)),
        'ln_in_g': 1.0 + 0.01 * nrm(ks[1], (D_MODEL,)),
        'ln_in_b': 0.01 * nrm(ks[2], (D_MODEL,)),
        'w_in': nrm(ks[3], (DEPTH, D_MODEL, IN_COLS)) * D_MODEL ** -0.5,
        'lam_params': LAMBDA_STD * nrm(ks[4], (DEPTH, 4, HEAD_DIM)),
        'subln_g': 1.0 + 0.01 * nrm(ks[5], (DEPTH, V_HEAD_DIM)),
        'w_f': nrm(ks[6], (DEPTH, N_FGROUPS, FGROUP_DIM, FGROUP_DIM)) * FGROUP_DIM ** -0.5,
        'b_f': 0.01 * nrm(ks[7], (DEPTH, FOURIER_WIDTH)),
        'w_o': nrm(ks[8], (DEPTH, MIX_WIDTH, D_MODEL)) * (MIX_WIDTH ** -0.5 * BETA),
        'ln1_g': 1.0 + 0.01 * nrm(ks[9], (DEPTH, D_MODEL)),
        'ln1_b': 0.01 * nrm(ks[10], (DEPTH, D_MODEL)),
        'w_gu': nrm(ks[11], (DEPTH, D_MODEL, 2 * D_FF)) * D_MODEL ** -0.5,
        'w_down': nrm(ks[12], (DEPTH, D_FF, D_MODEL)) * (D_FF ** -0.5 * BETA),
        'ln2_g': 1.0 + 0.01 * nrm(ks[13], (DEPTH, D_MODEL)),
        'ln2_b': 0.01 * nrm(ks[14], (DEPTH, D_MODEL)),
    }


def reference(x, ln_in_g, ln_in_b, w_in, lam_params, subln_g, w_f, b_f, w_o,
              ln1_g, ln1_b, w_gu, w_down, ln2_g, ln2_b):
    B, S = x.shape[0], x.shape[1]
    h = layer_norm(x, ln_in_g, ln_in_b)
    for l in range(DEPTH):
        lam_init = lambda_init_fn(l)
        proj = jnp.einsum('bsd,de->bse', h, w_in[l])
        q = proj[..., :ATT_WIDTH].reshape(B, S, N_HEADS, 2, HEAD_DIM)
        k = proj[..., ATT_WIDTH:2 * ATT_WIDTH].reshape(B, S, N_HEADS, 2, HEAD_DIM)
        v = proj[..., 2 * ATT_WIDTH:3 * ATT_WIDTH].reshape(B, S, N_HEADS, V_HEAD_DIM)
        u = proj[..., 3 * ATT_WIDTH:]
        lp = lam_params[l].astype(jnp.float32)
        lam = jnp.exp(jnp.sum(lp[0] * lp[1])) - jnp.exp(jnp.sum(lp[2] * lp[3])) + lam_init
        a = diff_attention(q, k, v, lam, lam_init, subln_g[l])
        f = fourier_mix(u, w_f[l], b_f[l])
        mix = jnp.einsum('bse,ed->bsd', jnp.concatenate([a, f], axis=-1), w_o[l])
        h = layer_norm(ALPHA * h + mix, ln1_g[l], ln1_b[l])
        h = layer_norm(ALPHA * h + swiglu(h, w_gu[l], w_down[l]), ln2_g[l], ln2_b[l])
    return h
```

```python
import functools
import math

import jax
import jax.numpy as jnp
from jax import lax
from jax.experimental import pallas as pl
from jax.experimental.pallas import tpu as pltpu

D_MODEL = 1024
DEPTH = 2
ATT_WIDTH = 512
FOURIER_WIDTH = 512
N_HEADS = 8
HEAD_DIM = 32
V_HEAD_DIM = 64
N_FGROUPS = 4
FGROUP_DIM = 128
IN_COLS = 2048
D_FF = 2816
LN_EPS = 1e-5
SUBLN_EPS = 1e-5
ALPHA = (2.0 * DEPTH) ** 0.25

LANES = 128
VMEM_LIMIT = 56 * 1024 * 1024

BF16 = jnp.bfloat16
F32 = jnp.float32


def _lambda_init(layer_idx):
    return 0.8 - 0.6 * math.exp(-0.3 * layer_idx)


def _layer_norm(x, g, b):
    mu = jnp.mean(x, axis=-1, keepdims=True)
    xc = x - mu
    var = jnp.mean(xc * xc, axis=-1, keepdims=True)
    return xc * lax.rsqrt(var + LN_EPS) * g + b


def _params(*sem):
    return pltpu.CompilerParams(dimension_semantics=sem, vmem_limit_bytes=VMEM_LIMIT)


def _ln_kernel(x_ref, g_ref, b_ref, o_ref):
    o_ref[...] = _layer_norm(x_ref[...], g_ref[...], b_ref[...])


def _ln_call(x2, g, b, tm=1024):
    m, d = x2.shape
    return pl.pallas_call(
        _ln_kernel,
        grid=(m // tm,),
        in_specs=[pl.BlockSpec((tm, d), lambda i: (i, 0)),
                  pl.BlockSpec((1, d), lambda i: (0, 0)),
                  pl.BlockSpec((1, d), lambda i: (0, 0))],
        out_specs=pl.BlockSpec((tm, d), lambda i: (i, 0)),
        out_shape=jax.ShapeDtypeStruct((m, d), F32),
        compiler_params=_params("parallel"),
        name="ln_in",
    )(x2, g.reshape(1, d), b.reshape(1, d))


PROJ_TN = 512


def _proj_kernel(h_ref, w_ref, o_ref):
    hb = h_ref[...].astype(BF16)
    for j in range(IN_COLS // PROJ_TN):
        cols = slice(j * PROJ_TN, (j + 1) * PROJ_TN)
        o_ref[:, cols] = jnp.dot(hb, w_ref[:, cols], preferred_element_type=F32).astype(BF16)


def _proj_call(h2, w_bf, tm=1024):
    m, d = h2.shape
    n = w_bf.shape[1]
    return pl.pallas_call(
        _proj_kernel,
        grid=(m // tm,),
        in_specs=[pl.BlockSpec((tm, d), lambda i: (i, 0)),
                  pl.BlockSpec((d, n), lambda i: (0, 0))],
        out_specs=pl.BlockSpec((tm, n), lambda i: (i, 0)),
        out_shape=jax.ShapeDtypeStruct((m, n), BF16),
        compiler_params=_params("parallel"),
        name="proj",
    )(h2, w_bf)


ATT_TQ = 256
HEADS_PER_STEP = LANES // V_HEAD_DIM


def _attn_kernel(lam_ref, g_ref, q_ref, k_ref, v_ref, o_ref, *, lam_init, seq):
    pair = pl.program_id(1)
    qi = pl.program_id(2)
    lp = lam_ref[...]
    lam = (jnp.exp(jnp.sum(lp[0:1] * lp[1:2], axis=-1, keepdims=True))
           - jnp.exp(jnp.sum(lp[2:3] * lp[3:4], axis=-1, keepdims=True)) + lam_init)
    qpos = qi * ATT_TQ + lax.broadcasted_iota(jnp.int32, (ATT_TQ, seq), 0)
    kpos = lax.broadcasted_iota(jnp.int32, (ATT_TQ, seq), 1)
    dist = jnp.abs(qpos - kpos).astype(F32)
    scale = HEAD_DIM ** -0.5
    q = q_ref[...]
    k = k_ref[...]
    v = v_ref[...]
    g = g_ref[...]
    for hh in range(HEADS_PER_STEP):
        head = pair * HEADS_PER_STEP + hh
        slope = jnp.exp2(-jnp.full((1, 1), head + 1, jnp.int32).astype(F32))
        bias = slope * dist
        probs = []
        for c in range(2):
            lo = hh * V_HEAD_DIM + c * HEAD_DIM
            s = lax.dot_general(q[:, lo:lo + HEAD_DIM], k[:, lo:lo + HEAD_DIM],
                                (((1,), (1,)), ((), ())), preferred_element_type=F32)
            s = s * scale - bias
            e = jnp.exp(s - jnp.max(s, axis=-1, keepdims=True))
            probs.append(e * (1.0 / jnp.sum(e, axis=-1, keepdims=True)))
        a = probs[0] - lam * probs[1]
        vh = v[:, hh * V_HEAD_DIM:(hh + 1) * V_HEAD_DIM]
        o = jnp.dot(a.astype(BF16), vh, preferred_element_type=F32)
        o = o * lax.rsqrt(jnp.mean(o * o, axis=-1, keepdims=True) + SUBLN_EPS)
        o = o * g * (1.0 - lam_init)
        o_ref[:, hh * V_HEAD_DIM:(hh + 1) * V_HEAD_DIM] = o.astype(o_ref.dtype)


def _attn_call(proj, lam_params, subln_g, batch, seq, lam_init):
    n_pairs = N_HEADS // HEADS_PER_STEP
    nq = seq // ATT_TQ
    k_off = ATT_WIDTH // LANES
    v_off = 2 * ATT_WIDTH // LANES
    kern = functools.partial(_attn_kernel, lam_init=lam_init, seq=seq)
    return pl.pallas_call(
        kern,
        grid=(batch, n_pairs, nq),
        in_specs=[pl.BlockSpec((4, HEAD_DIM), lambda b, p, i: (0, 0)),
                  pl.BlockSpec((1, V_HEAD_DIM), lambda b, p, i: (0, 0)),
                  pl.BlockSpec((ATT_TQ, LANES), lambda b, p, i: (b * nq + i, p)),
                  pl.BlockSpec((seq, LANES), lambda b, p, i: (b, k_off + p)),
                  pl.BlockSpec((seq, LANES), lambda b, p, i: (b, v_off + p))],
        out_specs=pl.BlockSpec((ATT_TQ, LANES), lambda b, p, i: (b * nq + i, p)),
        out_shape=jax.ShapeDtypeStruct((batch * seq, ATT_WIDTH), BF16),
        compiler_params=_params("parallel", "parallel", "arbitrary"),
        name="diff_attn",
    )(lam_params, subln_g.reshape(1, V_HEAD_DIM), proj, proj, proj)


def _dft_tables(seq):
    c = FGROUP_DIM
    ci = lax.broadcasted_iota(jnp.int32, (c, c), 0) * lax.broadcasted_iota(jnp.int32, (c, c), 1) % c
    ang_c = ci.astype(F32) * (2.0 * math.pi / c)
    chan = jnp.concatenate([jnp.cos(ang_c), jnp.sin(ang_c)], axis=1) * c ** -0.5
    si = lax.broadcasted_iota(jnp.int32, (seq, seq), 0) * lax.broadcasted_iota(jnp.int32, (seq, seq), 1) % seq
    ang_s = si.astype(F32) * (2.0 * math.pi / seq)
    pos_cos = (jnp.cos(ang_s) * seq ** -0.5).astype(BF16)
    pos_sin = (jnp.sin(ang_s) * seq ** -0.5).astype(BF16)
    return chan.astype(BF16), pos_cos, pos_sin


def _chan_dft_kernel(u_ref, t_ref, a_ref, b_ref):
    t = t_ref[...]
    for gi in range(N_FGROUPS):
        cols = slice(gi * FGROUP_DIM, (gi + 1) * FGROUP_DIM)
        r = jnp.dot(u_ref[:, cols], t, preferred_element_type=F32)
        a_ref[:, cols] = r[:, :FGROUP_DIM].astype(a_ref.dtype)
        b_ref[:, cols] = r[:, FGROUP_DIM:].astype(b_ref.dtype)


def _chan_dft_call(proj, chan_tab, tm=1024):
    m = proj.shape[0]
    u_off = 3 * ATT_WIDTH // FOURIER_WIDTH
    out = jax.ShapeDtypeStruct((m, FOURIER_WIDTH), BF16)
    return pl.pallas_call(
        _chan_dft_kernel,
        grid=(m // tm,),
        in_specs=[pl.BlockSpec((tm, FOURIER_WIDTH), lambda i: (i, u_off)),
                  pl.BlockSpec((FGROUP_DIM, 2 * FGROUP_DIM), lambda i: (0, 0))],
        out_specs=[pl.BlockSpec((tm, FOURIER_WIDTH), lambda i: (i, 0)),
                   pl.BlockSpec((tm, FOURIER_WIDTH), lambda i: (i, 0))],
        out_shape=[out, out],
        compiler_params=_params("parallel"),
        name="chan_dft",
    )(proj, chan_tab)


def _pos_dft_kernel(c_ref, s_ref, a_ref, b_ref, o_ref):
    re = jnp.dot(c_ref[...], a_ref[...], preferred_element_type=F32)
    re = re - jnp.dot(s_ref[...], b_ref[...], preferred_element_type=F32)
    o_ref[...] = re.astype(o_ref.dtype)


def _pos_dft_call(pos_cos, pos_sin, a, b, batch, seq, tm=512):
    nt = seq // tm
    return pl.pallas_call(
        _pos_dft_kernel,
        grid=(nt, batch),
        in_specs=[pl.BlockSpec((tm, seq), lambda i, bb: (i, 0)),
                  pl.BlockSpec((tm, seq), lambda i, bb: (i, 0)),
                  pl.BlockSpec((seq, FOURIER_WIDTH), lambda i, bb: (bb, 0)),
                  pl.BlockSpec((seq, FOURIER_WIDTH), lambda i, bb: (bb, 0))],
        out_specs=pl.BlockSpec((tm, FOURIER_WIDTH), lambda i, bb: (bb * nt + i, 0)),
        out_shape=jax.ShapeDtypeStruct((batch * seq, FOURIER_WIDTH), BF16),
        compiler_params=_params("parallel", "arbitrary"),
        name="pos_dft",
    )(pos_cos, pos_sin, a, b)


def _mix_kernel(h_ref, a_ref, f_ref, wf_ref, bf_ref, wo_ref, g_ref, b_ref, o_ref):
    parts = []
    for gi in range(N_FGROUPS):
        cols = slice(gi * FGROUP_DIM, (gi + 1) * FGROUP_DIM)
        parts.append(jnp.dot(f_ref[:, cols], wf_ref[gi], preferred_element_type=F32))
    y = jnp.concatenate(parts, axis=-1) + bf_ref[...]
    mix = jnp.dot(a_ref[...], wo_ref[:ATT_WIDTH, :], preferred_element_type=F32)
    mix = mix + jnp.dot(y.astype(BF16), wo_ref[ATT_WIDTH:, :], preferred_element_type=F32)
    o_ref[...] = _layer_norm(ALPHA * h_ref[...] + mix, g_ref[...], b_ref[...])


def _mix_call(h2, a, f, wf_bf, b_f, wo_bf, g, b, tm=1024):
    m, d = h2.shape
    return pl.pallas_call(
        _mix_kernel,
        grid=(m // tm,),
        in_specs=[pl.BlockSpec((tm, d), lambda i: (i, 0)),
                  pl.BlockSpec((tm, ATT_WIDTH), lambda i: (i, 0)),
                  pl.BlockSpec((tm, FOURIER_WIDTH), lambda i: (i, 0)),
                  pl.BlockSpec((N_FGROUPS, FGROUP_DIM, FGROUP_DIM), lambda i: (0, 0, 0)),
                  pl.BlockSpec((1, FOURIER_WIDTH), lambda i: (0, 0)),
                  pl.BlockSpec((d, d), lambda i: (0, 0)),
                  pl.BlockSpec((1, d), lambda i: (0, 0)),
                  pl.BlockSpec((1, d), lambda i: (0, 0))],
        out_specs=pl.BlockSpec((tm, d), lambda i: (i, 0)),
        out_shape=jax.ShapeDtypeStruct((m, d), F32),
        compiler_params=_params("parallel"),
        name="mix_ln1",
    )(h2, a, f, wf_bf, b_f.reshape(1, -1), wo_bf, g.reshape(1, d), b.reshape(1, d))


FFN_CHUNK = D_FF // 2


def _ffn_kernel(h_ref, wgu_ref, wd_ref, g_ref, b_ref, o_ref):
    h = h_ref[...]
    hb = h.astype(BF16)
    acc = None
    for j in range(D_FF // FFN_CHUNK):
        lo = j * FFN_CHUNK
        gate = jnp.dot(hb, wgu_ref[:, lo:lo + FFN_CHUNK], preferred_element_type=F32)
        up = jnp.dot(hb, wgu_ref[:, D_FF + lo:D_FF + lo + FFN_CHUNK], preferred_element_type=F32)
        act = (gate * jax.nn.sigmoid(gate) * up).astype(BF16)
        part = jnp.dot(act, wd_ref[lo:lo + FFN_CHUNK, :], preferred_element_type=F32)
        acc = part if acc is None else acc + part
    o_ref[...] = _layer_norm(ALPHA * h + acc, g_ref[...], b_ref[...])


def _ffn_call(h2, wgu_bf, wd_bf, g, b, tm=512):
    m, d = h2.shape
    const = dict(pipeline_mode=pl.Buffered(1))
    return pl.pallas_call(
        _ffn_kernel,
        grid=(m // tm,),
        in_specs=[pl.BlockSpec((tm, d), lambda i: (i, 0)),
                  pl.BlockSpec((d, 2 * D_FF), lambda i: (0, 0), **const),
                  pl.BlockSpec((D_FF, d), lambda i: (0, 0), **const),
                  pl.BlockSpec((1, d), lambda i: (0, 0)),
                  pl.BlockSpec((1, d), lambda i: (0, 0))],
        out_specs=pl.BlockSpec((tm, d), lambda i: (i, 0)),
        out_shape=jax.ShapeDtypeStruct((m, d), F32),
        compiler_params=_params("parallel"),
        name="ffn_ln2",
    )(h2, wgu_bf, wd_bf, g.reshape(1, d), b.reshape(1, d))


def kernel(x, ln_in_g, ln_in_b, w_in, lam_params, subln_g, w_f, b_f, w_o, ln1_g, ln1_b, w_gu, w_down, ln2_g, ln2_b):
    batch, seq, d = x.shape
    assert d == D_MODEL and seq % ATT_TQ == 0
    chan_tab, pos_cos, pos_sin = _dft_tables(seq)
    h = _ln_call(x.reshape(batch * seq, d), ln_in_g, ln_in_b)
    for l in range(DEPTH):
        proj = _proj_call(h, w_in[l].astype(BF16))
        a = _attn_call(proj, lam_params[l], subln_g[l], batch, seq, _lambda_init(l))
        fa, fb = _chan_dft_call(proj, chan_tab)
        f = _pos_dft_call(pos_cos, pos_sin, fa, fb, batch, seq)
        h = _mix_call(h, a, f, w_f[l].astype(BF16), b_f[l], w_o[l].astype(BF16), ln1_g[l], ln1_b[l])
        h = _ffn_call(h, w_gu[l].astype(BF16), w_down[l].astype(BF16), ln2_g[l], ln2_b[l])
    return h.reshape(batch, seq, d)
```

```python
import functools
import math

import jax
import jax.numpy as jnp
from jax import lax
from jax.experimental import pallas as pl
from jax.experimental.pallas import tpu as pltpu

D_MODEL = 1024
DEPTH = 2
ATT_WIDTH = 512
FOURIER_WIDTH = 512
N_HEADS = 8
HEAD_DIM = 32
V_HEAD_DIM = 64
N_FGROUPS = 4
FGROUP_DIM = 128
IN_COLS = 2048
D_FF = 2816
LN_EPS = 1e-5
SUBLN_EPS = 1e-5
ALPHA = (2.0 * DEPTH) ** 0.25
LOG2E = math.log2(math.e)

LANES = 128
BF16_SUBLANES = 16
VMEM_LIMIT = 56 * 1024 * 1024

BF16 = jnp.bfloat16
F32 = jnp.float32


def _lambda_init(layer_idx):
    return 0.8 - 0.6 * math.exp(-0.3 * layer_idx)


def _layer_norm(x, g, b):
    mu = jnp.mean(x, axis=-1, keepdims=True)
    xc = x - mu
    var = jnp.mean(xc * xc, axis=-1, keepdims=True)
    return xc * lax.rsqrt(var + LN_EPS) * g + b


def _params(*sem):
    return pltpu.CompilerParams(dimension_semantics=sem, vmem_limit_bytes=VMEM_LIMIT)


def _split3(x):
    hi = x.astype(BF16).astype(F32)
    r = x - hi
    mid = r.astype(BF16).astype(F32)
    lo = (r - mid).astype(BF16).astype(F32)
    return hi, mid, lo


def _ln_kernel(x_ref, g_ref, b_ref, o_ref):
    o_ref[...] = _layer_norm(x_ref[...], g_ref[...], b_ref[...])


def _ln_call(x2, g, b, tm=1024):
    m, d = x2.shape
    return pl.pallas_call(
        _ln_kernel,
        grid=(m // tm,),
        in_specs=[pl.BlockSpec((tm, d), lambda i: (i, 0)),
                  pl.BlockSpec((1, d), lambda i: (0, 0)),
                  pl.BlockSpec((1, d), lambda i: (0, 0))],
        out_specs=pl.BlockSpec((tm, d), lambda i: (i, 0)),
        out_shape=jax.ShapeDtypeStruct((m, d), F32),
        compiler_params=_params("parallel"),
        name="ln_in",
    )(x2, g.reshape(1, d), b.reshape(1, d))


ATT_TQ = 256
ATT_KBLK = 1024
HEADS_PER_STEP = LANES // V_HEAD_DIM
CHAINS_PER_STEP = 2 * HEADS_PER_STEP
VT_ROWS = V_HEAD_DIM + BF16_SUBLANES
KP_WIDTH = 2 * LANES
MASK_LANE = 6
MASK_BIG = 2.0 ** 100
NORM_ROUND_UP = 1.05
MAX_UNSHIFTED_SCORE = 50.0


def _slice_norms(x_bf):
    x32 = x_bf.astype(F32)
    seg = lax.broadcasted_iota(jnp.int32, (ATT_WIDTH, ATT_WIDTH), 0) // HEAD_DIM
    col = lax.broadcasted_iota(jnp.int32, (ATT_WIDTH, ATT_WIDTH), 1)
    ind = jnp.where(((col % LANES) < CHAINS_PER_STEP)
                    & (seg == (col // LANES) * CHAINS_PER_STEP + col % LANES), 1.0, 0.0).astype(BF16)
    return jnp.dot((x32 * x32).astype(BF16), ind, preferred_element_type=F32) * NORM_ROUND_UP


def _row_max(x, groups=8):
    r, c = x.shape
    return jnp.max(jnp.max(x.reshape(groups, r // groups, c), axis=0), axis=0, keepdims=True)


def _proj_kernel(h_ref, w_ref, q_ref, kp_ref, vt_ref, nrm_ref, u_ref, *, seq, tm):
    i = pl.program_id(0)
    hb = h_ref[...].astype(BF16)
    qf = jnp.dot(hb, w_ref[:, 0:ATT_WIDTH], preferred_element_type=F32)
    qb = (qf * (HEAD_DIM ** -0.5 * LOG2E)).astype(BF16)
    q_ref[...] = qb

    kf = jnp.dot(hb, w_ref[:, ATT_WIDTH:2 * ATT_WIDTH], preferred_element_type=F32).astype(BF16)
    pos = (i * tm) % seq + lax.broadcasted_iota(jnp.int32, (tm, 1), 0)
    hi, mid, lo = _split3((pos - seq // 2).astype(F32) * LOG2E)
    lane = lax.broadcasted_iota(jnp.int32, (tm, LANES), 1)
    aux = jnp.where(lane < 3, 1.0,
                    jnp.where(lane == 3, hi, jnp.where(lane == 4, mid, jnp.where(lane == 5, lo, 0.0))))
    aux = aux.astype(BF16)
    for p in range(N_HEADS // HEADS_PER_STEP):
        kp_ref[:, p * KP_WIDTH:p * KP_WIDTH + LANES] = kf[:, p * LANES:(p + 1) * LANES]
        kp_ref[:, p * KP_WIDTH + LANES:(p + 1) * KP_WIDTH] = aux

    nrm_ref[0:8, :] = jnp.broadcast_to(_row_max(_slice_norms(qb)), (8, ATT_WIDTH))
    nrm_ref[8:16, :] = jnp.broadcast_to(_row_max(_slice_norms(kf)), (8, ATT_WIDTH))

    vf = jnp.dot(hb, w_ref[:, 2 * ATT_WIDTH:3 * ATT_WIDTH], preferred_element_type=F32)
    vtr = vf.T.astype(BF16)
    ones = jnp.ones((BF16_SUBLANES, tm), BF16)
    for hd in range(N_HEADS):
        vt_ref[hd * VT_ROWS:hd * VT_ROWS + V_HEAD_DIM, :] = vtr[hd * V_HEAD_DIM:(hd + 1) * V_HEAD_DIM, :]
        vt_ref[hd * VT_ROWS + V_HEAD_DIM:(hd + 1) * VT_ROWS, :] = ones

    u_ref[...] = jnp.dot(hb, w_ref[:, 3 * ATT_WIDTH:], preferred_element_type=F32).astype(BF16)


def _proj_call(h2, w_bf, batch, seq, tm=1024):
    m, d = h2.shape
    n = w_bf.shape[1]
    tiles_per_seq = seq // tm
    kern = functools.partial(_proj_kernel, seq=seq, tm=tm)
    return pl.pallas_call(
        kern,
        grid=(m // tm,),
        in_specs=[pl.BlockSpec((tm, d), lambda i: (i, 0)),
                  pl.BlockSpec((d, n), lambda i: (0, 0))],
        out_specs=[pl.BlockSpec((tm, ATT_WIDTH), lambda i: (i, 0)),
                   pl.BlockSpec((tm, 4 * KP_WIDTH), lambda i: (i, 0)),
                   pl.BlockSpec((None, N_HEADS * VT_ROWS, tm),
                                lambda i: (i // tiles_per_seq, 0, i % tiles_per_seq)),
                   pl.BlockSpec((None, 16, ATT_WIDTH), lambda i: (i, 0, 0)),
                   pl.BlockSpec((tm, FOURIER_WIDTH), lambda i: (i, 0))],
        out_shape=[jax.ShapeDtypeStruct((m, ATT_WIDTH), BF16),
                   jax.ShapeDtypeStruct((m, 4 * KP_WIDTH), BF16),
                   jax.ShapeDtypeStruct((batch, N_HEADS * VT_ROWS, seq), BF16),
                   jax.ShapeDtypeStruct((m // tm, 16, ATT_WIDTH), F32),
                   jax.ShapeDtypeStruct((m, FOURIER_WIDTH), BF16)],
        compiler_params=_params("parallel"),
        name="proj",
    )(h2, w_bf)


def _attn_kernel(lam_ref, g_ref, nrm_ref, q_ref, kp_ref, kd_ref, vt_ref, vd_ref, o_ref, ks_ref, flag_ref, *,
                 lam_init, seq):
    pair = pl.program_id(1)
    qi = pl.program_id(2)
    diag_rows = pl.ds(pl.multiple_of(qi * ATT_TQ, ATT_TQ), ATT_TQ)
    lane_k = lax.broadcasted_iota(jnp.int32, (ATT_TQ, LANES), 1)

    @pl.when(qi == 0)
    def _():
        ks_ref[:, :LANES] = kp_ref[:, :LANES]
        ks_ref[:, LANES:] = -kp_ref[:, LANES:]
        nrm = jnp.max(nrm_ref[...], axis=0)
        lane = lax.broadcasted_iota(jnp.int32, (1, LANES), 1)
        prod = jnp.where(lane < CHAINS_PER_STEP, nrm[0:1, :] * nrm[8:9, :], 0.0)
        flag_ref[0] = (jnp.max(prod) <= MAX_UNSHIFTED_SCORE ** 2).astype(jnp.int32)

    @pl.when(qi > 0)
    def _():
        prev_rows = pl.ds(pl.multiple_of((qi - 1) * ATT_TQ, ATT_TQ), ATT_TQ)
        ks_ref[prev_rows, LANES:] = kp_ref[prev_rows, LANES:]

    ks_ref[diag_rows, LANES:] = jnp.where(lane_k == MASK_LANE, -MASK_BIG, 0.0).astype(BF16)

    lp = lam_ref[...]
    lam = (jnp.exp(jnp.sum(lp[0:1] * lp[1:2], axis=-1, keepdims=True))
           - jnp.exp(jnp.sum(lp[2:3] * lp[3:4], axis=-1, keepdims=True)) + lam_init)

    q32 = q_ref[...].astype(F32)
    lane_q = lax.broadcasted_iota(jnp.int32, (ATT_TQ, LANES), 1)
    pos = qi * ATT_TQ + lax.broadcasted_iota(jnp.int32, (ATT_TQ, 1), 0)
    hi, mid, lo = _split3((pos - seq // 2).astype(F32) * LOG2E)
    base = jnp.where(lane_q == 0, -hi,
                     jnp.where(lane_q == 1, -mid,
                               jnp.where(lane_q == 2, -lo, jnp.where(lane_q < 6, 1.0, 0.0))))
    zeros = jnp.zeros((ATT_TQ, LANES), BF16)
    rr = lax.broadcasted_iota(jnp.int32, (ATT_TQ, ATT_TQ), 0)
    cc = lax.broadcasted_iota(jnp.int32, (ATT_TQ, ATT_TQ), 1)
    diag_dist = jnp.abs(rr - cc).astype(F32) * LOG2E
    nt_dims = (((1,), (1,)), ((), ()))
    g_col = g_ref[...]

    def head_operands(hh):
        head = pair * HEADS_PER_STEP + hh
        slope = jnp.exp2(-jnp.full((1, 1), head + 1, jnp.int32).astype(F32))
        aux = jnp.where(lane_q == MASK_LANE, 1.0, base * slope).astype(BF16)
        q_main, q_diag = [], []
        for c in range(2):
            chain = hh * 2 + c
            qm = jnp.where(lane_q // HEAD_DIM == chain, q32, 0.0).astype(BF16)
            q_main.append(jnp.concatenate([qm, aux], axis=1))
            q_diag.append(jnp.concatenate([qm, zeros], axis=1))
        q_main = jnp.concatenate(q_main, axis=0)
        q_diag = jnp.concatenate(q_diag, axis=0)
        sd = lax.dot_general(kd_ref[...], q_diag, nt_dims, preferred_element_type=F32)
        sd = sd - slope * jnp.concatenate([diag_dist, diag_dist], axis=1)
        vt = vt_ref[hh * VT_ROWS:(hh + 1) * VT_ROWS, :]
        vd = vd_ref[hh * VT_ROWS:(hh + 1) * VT_ROWS, :]
        return q_main, sd, vt, vd

    def finish(accs):
        outs = []
        for acc in accs:
            norm = []
            for c in range(2):
                part = acc[:, c * ATT_TQ:(c + 1) * ATT_TQ]
                denom = part[V_HEAD_DIM:V_HEAD_DIM + 1, :]
                norm.append(part[:V_HEAD_DIM, :] * (1.0 / denom))
            a_t = norm[0] - lam * norm[1]
            a_t = a_t * lax.rsqrt(jnp.mean(a_t * a_t, axis=0, keepdims=True) + SUBLN_EPS)
            outs.append(a_t * g_col * (1.0 - lam_init))
        o_ref[...] = jnp.concatenate(outs, axis=0).T.astype(o_ref.dtype)

    @pl.when(flag_ref[0] == 1)
    def _():
        accs = []
        for hh in range(HEADS_PER_STEP):
            q_main, sd, vt, vd = head_operands(hh)
            acc = jnp.dot(vd, jnp.exp2(sd).astype(BF16), preferred_element_type=F32)
            for blk in range(seq // ATT_KBLK):
                rows = slice(blk * ATT_KBLK, (blk + 1) * ATT_KBLK)
                st = lax.dot_general(ks_ref[rows, :], q_main, nt_dims, preferred_element_type=F32)
                acc = acc + jnp.dot(vt[:, rows], jnp.exp2(st).astype(BF16), preferred_element_type=F32)
            accs.append(acc)
        finish(accs)

    @pl.when(flag_ref[0] != 1)
    def _():
        accs = []
        for hh in range(HEADS_PER_STEP):
            q_main, sd, vt, vd = head_operands(hh)
            st = lax.dot_general(ks_ref[...], q_main, nt_dims, preferred_element_type=F32)
            m = jnp.maximum(jnp.max(sd, axis=0, keepdims=True), jnp.max(st, axis=0, keepdims=True))
            accs.append(jnp.dot(vd, jnp.exp2(sd - m).astype(BF16), preferred_element_type=F32)
                        + jnp.dot(vt, jnp.exp2(st - m).astype(BF16), preferred_element_type=F32))
        finish(accs)


def _attn_call(q, kp, vt, nrm, lam_params, subln_g, batch, seq, lam_init):
    n_pairs = N_HEADS // HEADS_PER_STEP
    nq = seq // ATT_TQ
    tiles_per_seq = nrm.shape[0] // batch
    kern = functools.partial(_attn_kernel, lam_init=lam_init, seq=seq)
    pair_rows = HEADS_PER_STEP * VT_ROWS
    return pl.pallas_call(
        kern,
        grid=(batch, n_pairs, nq),
        in_specs=[pl.BlockSpec((4, HEAD_DIM), lambda b, p, i: (0, 0)),
                  pl.BlockSpec((V_HEAD_DIM, 1), lambda b, p, i: (0, 0)),
                  pl.BlockSpec((tiles_per_seq, 16, LANES), lambda b, p, i: (b, 0, p)),
                  pl.BlockSpec((ATT_TQ, LANES), lambda b, p, i: (b * nq + i, p)),
                  pl.BlockSpec((seq, KP_WIDTH), lambda b, p, i: (b, p)),
                  pl.BlockSpec((ATT_TQ, KP_WIDTH), lambda b, p, i: (b * nq + i, p)),
                  pl.BlockSpec((None, pair_rows, seq), lambda b, p, i: (b, p, 0)),
                  pl.BlockSpec((None, pair_rows, ATT_TQ), lambda b, p, i: (b, p, i))],
        out_specs=pl.BlockSpec((ATT_TQ, LANES), lambda b, p, i: (b * nq + i, p)),
        out_shape=jax.ShapeDtypeStruct((batch * seq, ATT_WIDTH), BF16),
        scratch_shapes=[pltpu.VMEM((seq, KP_WIDTH), BF16), pltpu.SMEM((1,), jnp.int32)],
        compiler_params=_params("parallel", "parallel", "arbitrary"),
        name="diff_attn",
    )(lam_params, subln_g.reshape(V_HEAD_DIM, 1), nrm, q, kp, kp, vt, vt)


def _dft_tables(seq):
    c = FGROUP_DIM
    ci = lax.broadcasted_iota(jnp.int32, (c, c), 0) * lax.broadcasted_iota(jnp.int32, (c, c), 1) % c
    ang_c = ci.astype(F32) * (2.0 * math.pi / c)
    chan = jnp.concatenate([jnp.cos(ang_c), jnp.sin(ang_c)], axis=1) * c ** -0.5
    si = lax.broadcasted_iota(jnp.int32, (seq, seq), 0) * lax.broadcasted_iota(jnp.int32, (seq, seq), 1) % seq
    ang_s = si.astype(F32) * (2.0 * math.pi / seq)
    pos_cos = (jnp.cos(ang_s) * seq ** -0.5).astype(BF16)
    pos_sin = (jnp.sin(ang_s) * seq ** -0.5).astype(BF16)
    return chan.astype(BF16), pos_cos, pos_sin


def _chan_dft_kernel(u_ref, t_ref, a_ref, b_ref):
    t = t_ref[...]
    for gi in range(N_FGROUPS):
        cols = slice(gi * FGROUP_DIM, (gi + 1) * FGROUP_DIM)
        r = jnp.dot(u_ref[:, cols], t, preferred_element_type=F32)
        a_ref[:, cols] = r[:, :FGROUP_DIM].astype(a_ref.dtype)
        b_ref[:, cols] = r[:, FGROUP_DIM:].astype(b_ref.dtype)


def _chan_dft_call(u, chan_tab, tm=1024):
    m = u.shape[0]
    out = jax.ShapeDtypeStruct((m, FOURIER_WIDTH), BF16)
    return pl.pallas_call(
        _chan_dft_kernel,
        grid=(m // tm,),
        in_specs=[pl.BlockSpec((tm, FOURIER_WIDTH), lambda i: (i, 0)),
                  pl.BlockSpec((FGROUP_DIM, 2 * FGROUP_DIM), lambda i: (0, 0))],
        out_specs=[pl.BlockSpec((tm, FOURIER_WIDTH), lambda i: (i, 0)),
                   pl.BlockSpec((tm, FOURIER_WIDTH), lambda i: (i, 0))],
        out_shape=[out, out],
        compiler_params=_params("parallel"),
        name="chan_dft",
    )(u, chan_tab)


def _pos_dft_kernel(c_ref, s_ref, a_ref, b_ref, o_ref):
    re = jnp.dot(c_ref[...], a_ref[...], preferred_element_type=F32)
    re = re - jnp.dot(s_ref[...], b_ref[...], preferred_element_type=F32)
    o_ref[...] = re.astype(o_ref.dtype)


def _pos_dft_call(pos_cos, pos_sin, a, b, batch, seq, tm=512):
    nt = seq // tm
    return pl.pallas_call(
        _pos_dft_kernel,
        grid=(nt, batch),
        in_specs=[pl.BlockSpec((tm, seq), lambda i, bb: (i, 0)),
                  pl.BlockSpec((tm, seq), lambda i, bb: (i, 0)),
                  pl.BlockSpec((seq, FOURIER_WIDTH), lambda i, bb: (bb, 0)),
                  pl.BlockSpec((seq, FOURIER_WIDTH), lambda i, bb: (bb, 0))],
        out_specs=pl.BlockSpec((tm, FOURIER_WIDTH), lambda i, bb: (bb * nt + i, 0)),
        out_shape=jax.ShapeDtypeStruct((batch * seq, FOURIER_WIDTH), BF16),
        compiler_params=_params("parallel", "arbitrary"),
        name="pos_dft",
    )(pos_cos, pos_sin, a, b)


def _mix_kernel(h_ref, a_ref, f_ref, wf_ref, bf_ref, wo_ref, g_ref, b_ref, o_ref):
    parts = []
    for gi in range(N_FGROUPS):
        cols = slice(gi * FGROUP_DIM, (gi + 1) * FGROUP_DIM)
        parts.append(jnp.dot(f_ref[:, cols], wf_ref[gi], preferred_element_type=F32))
    y = jnp.concatenate(parts, axis=-1) + bf_ref[...]
    mix = jnp.dot(a_ref[...], wo_ref[:ATT_WIDTH, :], preferred_element_type=F32)
    mix = mix + jnp.dot(y.astype(BF16), wo_ref[ATT_WIDTH:, :], preferred_element_type=F32)
    o_ref[...] = _layer_norm(ALPHA * h_ref[...] + mix, g_ref[...], b_ref[...])


def _mix_call(h2, a, f, wf_bf, b_f, wo_bf, g, b, tm=1024):
    m, d = h2.shape
    return pl.pallas_call(
        _mix_kernel,
        grid=(m // tm,),
        in_specs=[pl.BlockSpec((tm, d), lambda i: (i, 0)),
                  pl.BlockSpec((tm, ATT_WIDTH), lambda i: (i, 0)),
                  pl.BlockSpec((tm, FOURIER_WIDTH), lambda i: (i, 0)),
                  pl.BlockSpec((N_FGROUPS, FGROUP_DIM, FGROUP_DIM), lambda i: (0, 0, 0)),
                  pl.BlockSpec((1, FOURIER_WIDTH), lambda i: (0, 0)),
                  pl.BlockSpec((d, d), lambda i: (0, 0)),
                  pl.BlockSpec((1, d), lambda i: (0, 0)),
                  pl.BlockSpec((1, d), lambda i: (0, 0))],
        out_specs=pl.BlockSpec((tm, d), lambda i: (i, 0)),
        out_shape=jax.ShapeDtypeStruct((m, d), F32),
        compiler_params=_params("parallel"),
        name="mix_ln1",
    )(h2, a, f, wf_bf, b_f.reshape(1, -1), wo_bf, g.reshape(1, d), b.reshape(1, d))


FFN_CHUNK = D_FF // 2


def _ffn_kernel(h_ref, wgu_ref, wd_ref, g_ref, b_ref, o_ref):
    h = h_ref[...]
    hb = h.astype(BF16)
    acc = None
    for j in range(D_FF // FFN_CHUNK):
        lo = j * FFN_CHUNK
        gate = jnp.dot(hb, wgu_ref[:, lo:lo + FFN_CHUNK], preferred_element_type=F32)
        up = jnp.dot(hb, wgu_ref[:, D_FF + lo:D_FF + lo + FFN_CHUNK], preferred_element_type=F32)
        act = (gate * jax.nn.sigmoid(gate) * up).astype(BF16)
        part = jnp.dot(act, wd_ref[lo:lo + FFN_CHUNK, :], preferred_element_type=F32)
        acc = part if acc is None else acc + part
    o_ref[...] = _layer_norm(ALPHA * h + acc, g_ref[...], b_ref[...])


def _ffn_call(h2, wgu_bf, wd_bf, g, b, tm=512):
    m, d = h2.shape
    const = dict(pipeline_mode=pl.Buffered(1))
    return pl.pallas_call(
        _ffn_kernel,
        grid=(m // tm,),
        in_specs=[pl.BlockSpec((tm, d), lambda i: (i, 0)),
                  pl.BlockSpec((d, 2 * D_FF), lambda i: (0, 0), **const),
                  pl.BlockSpec((D_FF, d), lambda i: (0, 0), **const),
                  pl.BlockSpec((1, d), lambda i: (0, 0)),
                  pl.BlockSpec((1, d), lambda i: (0, 0))],
        out_specs=pl.BlockSpec((tm, d), lambda i: (i, 0)),
        out_shape=jax.ShapeDtypeStruct((m, d), F32),
        compiler_params=_params("parallel"),
        name="ffn_ln2",
    )(h2, wgu_bf, wd_bf, g.reshape(1, d), b.reshape(1, d))


def kernel(x, ln_in_g, ln_in_b, w_in, lam_params, subln_g, w_f, b_f, w_o, ln1_g, ln1_b, w_gu, w_down, ln2_g, ln2_b):
    batch, seq, d = x.shape
    assert d == D_MODEL and seq % ATT_KBLK == 0 and ATT_KBLK % ATT_TQ == 0
    chan_tab, pos_cos, pos_sin = _dft_tables(seq)
    h = _ln_call(x.reshape(batch * seq, d), ln_in_g, ln_in_b)
    for l in range(DEPTH):
        q, kp, vt, nrm, u = _proj_call(h, w_in[l].astype(BF16), batch, seq)
        a = _attn_call(q, kp, vt, nrm, lam_params[l], subln_g[l], batch, seq, _lambda_init(l))
        fa, fb = _chan_dft_call(u, chan_tab)
        f = _pos_dft_call(pos_cos, pos_sin, fa, fb, batch, seq)
        h = _mix_call(h, a, f, w_f[l].astype(BF16), b_f[l], w_o[l].astype(BF16), ln1_g[l], ln1_b[l])
        h = _ffn_call(h, w_gu[l].astype(BF16), w_down[l].astype(BF16), ln2_g[l], ln2_b[l])
    return h.reshape(batch, seq, d)
```

```python
import functools
import math

import jax
import jax.numpy as jnp
from jax import lax
from jax.experimental import pallas as pl
from jax.experimental.pallas import tpu as pltpu

D_MODEL = 1024
DEPTH = 2
ATT_WIDTH = 512
FOURIER_WIDTH = 512
N_HEADS = 8
HEAD_DIM = 32
V_HEAD_DIM = 64
N_FGROUPS = 4
FGROUP_DIM = 128
IN_COLS = 2048
D_FF = 2816
LN_EPS = 1e-5
SUBLN_EPS = 1e-5
ALPHA = (2.0 * DEPTH) ** 0.25
LOG2E = math.log2(math.e)

LANES = 128
BF16_SUBLANES = 16
VMEM_LIMIT = 56 * 1024 * 1024

BF16 = jnp.bfloat16
F32 = jnp.float32


def _lambda_init(layer_idx):
    return 0.8 - 0.6 * math.exp(-0.3 * layer_idx)


def _layer_norm(x, g, b):
    mu = jnp.mean(x, axis=-1, keepdims=True)
    xc = x - mu
    var = jnp.mean(xc * xc, axis=-1, keepdims=True)
    return xc * lax.rsqrt(var + LN_EPS) * g + b


def _params(*sem):
    return pltpu.CompilerParams(dimension_semantics=sem, vmem_limit_bytes=VMEM_LIMIT)


def _split3(x):
    hi = x.astype(BF16).astype(F32)
    r = x - hi
    mid = r.astype(BF16).astype(F32)
    lo = (r - mid).astype(BF16).astype(F32)
    return hi, mid, lo


def _ln_kernel(x_ref, g_ref, b_ref, o_ref):
    o_ref[...] = _layer_norm(x_ref[...], g_ref[...], b_ref[...])


def _ln_call(x2, g, b, tm=1024):
    m, d = x2.shape
    return pl.pallas_call(
        _ln_kernel,
        grid=(m // tm,),
        in_specs=[pl.BlockSpec((tm, d), lambda i: (i, 0)),
                  pl.BlockSpec((1, d), lambda i: (0, 0)),
                  pl.BlockSpec((1, d), lambda i: (0, 0))],
        out_specs=pl.BlockSpec((tm, d), lambda i: (i, 0)),
        out_shape=jax.ShapeDtypeStruct((m, d), F32),
        compiler_params=_params("parallel"),
        name="ln_in",
    )(x2, g.reshape(1, d), b.reshape(1, d))


ATT_TQ = 256
ATT_KBLK = 1024
STEP_TILES = 2
HEADS_PER_STEP = LANES // V_HEAD_DIM
CHAINS_PER_STEP = 2 * HEADS_PER_STEP
VT_ROWS = V_HEAD_DIM + BF16_SUBLANES
KP_WIDTH = 2 * LANES
MASK_LANE = 6
MASK_BIG = 2.0 ** 100
NORM_ROUND_UP = 1.05
MAX_UNSHIFTED_SCORE = 50.0


def _slice_norms(x_bf):
    x32 = x_bf.astype(F32)
    seg = lax.broadcasted_iota(jnp.int32, (ATT_WIDTH, ATT_WIDTH), 0) // HEAD_DIM
    col = lax.broadcasted_iota(jnp.int32, (ATT_WIDTH, ATT_WIDTH), 1)
    ind = jnp.where(((col % LANES) < CHAINS_PER_STEP)
                    & (seg == (col // LANES) * CHAINS_PER_STEP + col % LANES), 1.0, 0.0).astype(BF16)
    return jnp.dot((x32 * x32).astype(BF16), ind, preferred_element_type=F32) * NORM_ROUND_UP


def _row_max(x, groups=8):
    r, c = x.shape
    return jnp.max(jnp.max(x.reshape(groups, r // groups, c), axis=0), axis=0, keepdims=True)


def _proj_kernel(h_ref, w_ref, q_ref, kp_ref, vt_ref, nrm_ref, u_ref, *, seq, tm):
    i = pl.program_id(0)
    hb = h_ref[...].astype(BF16)
    qf = jnp.dot(hb, w_ref[:, 0:ATT_WIDTH], preferred_element_type=F32)
    qb = (qf * (HEAD_DIM ** -0.5 * LOG2E)).astype(BF16)
    q_ref[...] = qb

    kf = jnp.dot(hb, w_ref[:, ATT_WIDTH:2 * ATT_WIDTH], preferred_element_type=F32).astype(BF16)
    pos = (i * tm) % seq + lax.broadcasted_iota(jnp.int32, (tm, 1), 0)
    hi, mid, lo = _split3((pos - seq // 2).astype(F32) * LOG2E)
    lane = lax.broadcasted_iota(jnp.int32, (tm, LANES), 1)
    aux = jnp.where(lane < 3, 1.0,
                    jnp.where(lane == 3, hi, jnp.where(lane == 4, mid, jnp.where(lane == 5, lo, 0.0))))
    aux = aux.astype(BF16)
    for p in range(N_HEADS // HEADS_PER_STEP):
        kp_ref[:, p * KP_WIDTH:p * KP_WIDTH + LANES] = kf[:, p * LANES:(p + 1) * LANES]
        kp_ref[:, p * KP_WIDTH + LANES:(p + 1) * KP_WIDTH] = aux

    nrm_ref[0:8, :] = jnp.broadcast_to(_row_max(_slice_norms(qb)), (8, ATT_WIDTH))
    nrm_ref[8:16, :] = jnp.broadcast_to(_row_max(_slice_norms(kf)), (8, ATT_WIDTH))

    vf = jnp.dot(hb, w_ref[:, 2 * ATT_WIDTH:3 * ATT_WIDTH], preferred_element_type=F32)
    vtr = vf.T.astype(BF16)
    ones = jnp.ones((BF16_SUBLANES, tm), BF16)
    for hd in range(N_HEADS):
        vt_ref[hd * VT_ROWS:hd * VT_ROWS + V_HEAD_DIM, :] = vtr[hd * V_HEAD_DIM:(hd + 1) * V_HEAD_DIM, :]
        vt_ref[hd * VT_ROWS + V_HEAD_DIM:(hd + 1) * VT_ROWS, :] = ones

    u_ref[...] = jnp.dot(hb, w_ref[:, 3 * ATT_WIDTH:], preferred_element_type=F32).astype(BF16)


def _proj_call(h2, w_bf, batch, seq, tm=1024):
    m, d = h2.shape
    n = w_bf.shape[1]
    tiles_per_seq = seq // tm
    kern = functools.partial(_proj_kernel, seq=seq, tm=tm)
    return pl.pallas_call(
        kern,
        grid=(m // tm,),
        in_specs=[pl.BlockSpec((tm, d), lambda i: (i, 0)),
                  pl.BlockSpec((d, n), lambda i: (0, 0))],
        out_specs=[pl.BlockSpec((tm, ATT_WIDTH), lambda i: (i, 0)),
                   pl.BlockSpec((tm, 4 * KP_WIDTH), lambda i: (i, 0)),
                   pl.BlockSpec((None, N_HEADS * VT_ROWS, tm),
                                lambda i: (i // tiles_per_seq, 0, i % tiles_per_seq)),
                   pl.BlockSpec((None, 16, ATT_WIDTH), lambda i: (i, 0, 0)),
                   pl.BlockSpec((tm, FOURIER_WIDTH), lambda i: (i, 0))],
        out_shape=[jax.ShapeDtypeStruct((m, ATT_WIDTH), BF16),
                   jax.ShapeDtypeStruct((m, 4 * KP_WIDTH), BF16),
                   jax.ShapeDtypeStruct((batch, N_HEADS * VT_ROWS, seq), BF16),
                   jax.ShapeDtypeStruct((m // tm, 16, ATT_WIDTH), F32),
                   jax.ShapeDtypeStruct((m, FOURIER_WIDTH), BF16)],
        compiler_params=_params("parallel"),
        name="proj",
    )(h2, w_bf)


def _attn_kernel(lam_ref, g_ref, nrm_ref, q_ref, kp_ref, kd_ref, vt_ref, vd_ref, o_ref, ks_ref, p_ref, flag_ref, *,
                 lam_init, seq):
    pair = pl.program_id(1)
    step = pl.program_id(2)
    lane_k = lax.broadcasted_iota(jnp.int32, (ATT_TQ, LANES), 1)
    diag_cols = jnp.where(lane_k == MASK_LANE, -MASK_BIG, 0.0).astype(BF16)

    def chunk_rows(chunk):
        return pl.ds(pl.multiple_of(chunk * ATT_TQ, ATT_TQ), ATT_TQ)

    def enter_tile(tile):
        chunk = step * STEP_TILES + tile

        @pl.when(chunk > 0)
        def _():
            ks_ref[chunk_rows(chunk - 1), LANES:] = kp_ref[chunk_rows(chunk - 1), LANES:]

        ks_ref[chunk_rows(chunk), LANES:] = diag_cols

    @pl.when(step == 0)
    def _():
        ks_ref[:, :LANES] = kp_ref[:, :LANES]
        ks_ref[:, LANES:] = -kp_ref[:, LANES:]
        nrm = jnp.max(nrm_ref[...], axis=0)
        lane = lax.broadcasted_iota(jnp.int32, (1, LANES), 1)
        prod = jnp.where(lane < CHAINS_PER_STEP, nrm[0:1, :] * nrm[8:9, :], 0.0)
        flag_ref[0] = (jnp.max(prod) <= MAX_UNSHIFTED_SCORE ** 2).astype(jnp.int32)

    lp = lam_ref[...]
    lam = (jnp.exp(jnp.sum(lp[0:1] * lp[1:2], axis=-1, keepdims=True))
           - jnp.exp(jnp.sum(lp[2:3] * lp[3:4], axis=-1, keepdims=True)) + lam_init)

    lane_q = lax.broadcasted_iota(jnp.int32, (ATT_TQ, LANES), 1)
    zeros = jnp.zeros((ATT_TQ, LANES), BF16)
    rr = lax.broadcasted_iota(jnp.int32, (ATT_TQ, ATT_TQ), 0)
    cc = lax.broadcasted_iota(jnp.int32, (ATT_TQ, ATT_TQ), 1)
    diag_dist = jnp.abs(rr - cc).astype(F32) * LOG2E
    diag_dist = jnp.concatenate([diag_dist, diag_dist], axis=1)
    nt_dims = (((1,), (1,)), ((), ()))
    g_col = g_ref[...]

    def stage_operands(tile, hh):
        tile_rows = slice(tile * ATT_TQ, (tile + 1) * ATT_TQ)
        q32 = q_ref[tile_rows, :].astype(F32)
        pos = (step * STEP_TILES + tile) * ATT_TQ + lax.broadcasted_iota(jnp.int32, (ATT_TQ, 1), 0)
        hi, mid, lo = _split3((pos - seq // 2).astype(F32) * LOG2E)
        base = jnp.where(lane_q == 0, -hi,
                         jnp.where(lane_q == 1, -mid,
                                   jnp.where(lane_q == 2, -lo, jnp.where(lane_q < 6, 1.0, 0.0))))
        head = pair * HEADS_PER_STEP + hh
        slope = jnp.exp2(-jnp.full((1, 1), head + 1, jnp.int32).astype(F32))
        aux = jnp.where(lane_q == MASK_LANE, 1.0, base * slope).astype(BF16)
        q_main, q_diag = [], []
        for c in range(2):
            chain = hh * 2 + c
            qm = jnp.where(lane_q // HEAD_DIM == chain, q32, 0.0).astype(BF16)
            q_main.append(jnp.concatenate([qm, aux], axis=1))
            q_diag.append(jnp.concatenate([qm, zeros], axis=1))
        q_main = jnp.concatenate(q_main, axis=0)
        q_diag = jnp.concatenate(q_diag, axis=0)
        sd = lax.dot_general(kd_ref[tile_rows, :], q_diag, nt_dims, preferred_element_type=F32)
        sd = sd - slope * diag_dist
        vt = vt_ref[hh * VT_ROWS:(hh + 1) * VT_ROWS, :]
        vd = vd_ref[hh * VT_ROWS:(hh + 1) * VT_ROWS, tile_rows]
        return q_main, sd, vt, vd

    def finish(tile, accs):
        outs = []
        for acc in accs:
            norm = []
            for c in range(2):
                part = acc[:, c * ATT_TQ:(c + 1) * ATT_TQ]
                denom = part[V_HEAD_DIM:V_HEAD_DIM + 1, :]
                norm.append(part[:V_HEAD_DIM, :] * (1.0 / denom))
            a_t = norm[0] - lam * norm[1]
            a_t = a_t * lax.rsqrt(jnp.mean(a_t * a_t, axis=0, keepdims=True) + SUBLN_EPS)
            outs.append(a_t * g_col * (1.0 - lam_init))
        o_ref[tile * ATT_TQ:(tile + 1) * ATT_TQ, :] = jnp.concatenate(outs, axis=0).T.astype(o_ref.dtype)

    stages = [(tile, hh) for tile in range(STEP_TILES) for hh in range(HEADS_PER_STEP)]
    key_blocks = [slice(b * ATT_KBLK, (b + 1) * ATT_KBLK) for b in range(seq // ATT_KBLK)]

    @pl.when(flag_ref[0] == 1)
    def _():
        ops = [None] * len(stages)
        accs = [None] * len(stages)
        for n in range(len(stages) + 1):
            if n < len(stages):
                tile, hh = stages[n]
                if hh == 0:
                    enter_tile(tile)
                ops[n] = stage_operands(tile, hh)
            if n >= 1:
                _, sd, _, vd = ops[n - 1]
                accs[n - 1] = jnp.dot(vd, jnp.exp2(sd).astype(BF16), preferred_element_type=F32)
            for rows in key_blocks:
                if n < len(stages):
                    st = lax.dot_general(ks_ref[rows, :], ops[n][0], nt_dims, preferred_element_type=F32)
                    p_ref[n % 2, rows, :] = jnp.exp2(st).astype(BF16)
                if n >= 1:
                    vt = ops[n - 1][2]
                    accs[n - 1] = accs[n - 1] + jnp.dot(vt[:, rows], p_ref[(n - 1) % 2, rows, :],
                                                        preferred_element_type=F32)
            if n >= 1 and stages[n - 1][1] == HEADS_PER_STEP - 1:
                tile = stages[n - 1][0]
                finish(tile, accs[n - HEADS_PER_STEP:n])

    @pl.when(flag_ref[0] != 1)
    def _():
        for tile in range(STEP_TILES):
            enter_tile(tile)
            accs = []
            for hh in range(HEADS_PER_STEP):
                q_main, sd, vt, vd = stage_operands(tile, hh)
                st = lax.dot_general(ks_ref[...], q_main, nt_dims, preferred_element_type=F32)
                m = jnp.maximum(jnp.max(sd, axis=0, keepdims=True), jnp.max(st, axis=0, keepdims=True))
                accs.append(jnp.dot(vd, jnp.exp2(sd - m).astype(BF16), preferred_element_type=F32)
                            + jnp.dot(vt, jnp.exp2(st - m).astype(BF16), preferred_element_type=F32))
            finish(tile, accs)


def _attn_call(q, kp, vt, nrm, lam_params, subln_g, batch, seq, lam_init):
    n_pairs = N_HEADS // HEADS_PER_STEP
    step_rows = STEP_TILES * ATT_TQ
    nsteps = seq // step_rows
    tiles_per_seq = nrm.shape[0] // batch
    kern = functools.partial(_attn_kernel, lam_init=lam_init, seq=seq)
    pair_rows = HEADS_PER_STEP * VT_ROWS
    return pl.pallas_call(
        kern,
        grid=(batch, n_pairs, nsteps),
        in_specs=[pl.BlockSpec((4, HEAD_DIM), lambda b, p, i: (0, 0)),
                  pl.BlockSpec((V_HEAD_DIM, 1), lambda b, p, i: (0, 0)),
                  pl.BlockSpec((tiles_per_seq, 16, LANES), lambda b, p, i: (b, 0, p)),
                  pl.BlockSpec((step_rows, LANES), lambda b, p, i: (b * nsteps + i, p)),
                  pl.BlockSpec((seq, KP_WIDTH), lambda b, p, i: (b, p)),
                  pl.BlockSpec((step_rows, KP_WIDTH), lambda b, p, i: (b * nsteps + i, p)),
                  pl.BlockSpec((None, pair_rows, seq), lambda b, p, i: (b, p, 0)),
                  pl.BlockSpec((None, pair_rows, step_rows), lambda b, p, i: (b, p, i))],
        out_specs=pl.BlockSpec((step_rows, LANES), lambda b, p, i: (b * nsteps + i, p)),
        out_shape=jax.ShapeDtypeStruct((batch * seq, ATT_WIDTH), BF16),
        scratch_shapes=[pltpu.VMEM((seq, KP_WIDTH), BF16),
                        pltpu.VMEM((2, seq, 2 * ATT_TQ), BF16),
                        pltpu.SMEM((1,), jnp.int32)],
        compiler_params=_params("parallel", "parallel", "arbitrary"),
        name="diff_attn",
    )(lam_params, subln_g.reshape(V_HEAD_DIM, 1), nrm, q, kp, kp, vt, vt)


def _dft_tables(seq):
    c = FGROUP_DIM
    ci = lax.broadcasted_iota(jnp.int32, (c, c), 0) * lax.broadcasted_iota(jnp.int32, (c, c), 1) % c
    ang_c = ci.astype(F32) * (2.0 * math.pi / c)
    chan = jnp.concatenate([jnp.cos(ang_c), jnp.sin(ang_c)], axis=1) * c ** -0.5
    si = lax.broadcasted_iota(jnp.int32, (seq, seq), 0) * lax.broadcasted_iota(jnp.int32, (seq, seq), 1) % seq
    ang_s = si.astype(F32) * (2.0 * math.pi / seq)
    pos_cos = (jnp.cos(ang_s) * seq ** -0.5).astype(BF16)
    pos_sin = (jnp.sin(ang_s) * seq ** -0.5).astype(BF16)
    return chan.astype(BF16), pos_cos, pos_sin


def _chan_dft_kernel(u_ref, t_ref, a_ref, b_ref):
    t = t_ref[...]
    for gi in range(N_FGROUPS):
        cols = slice(gi * FGROUP_DIM, (gi + 1) * FGROUP_DIM)
        r = jnp.dot(u_ref[:, cols], t, preferred_element_type=F32)
        a_ref[:, cols] = r[:, :FGROUP_DIM].astype(a_ref.dtype)
        b_ref[:, cols] = r[:, FGROUP_DIM:].astype(b_ref.dtype)


def _chan_dft_call(u, chan_tab, tm=1024):
    m = u.shape[0]
    out = jax.ShapeDtypeStruct((m, FOURIER_WIDTH), BF16)
    return pl.pallas_call(
        _chan_dft_kernel,
        grid=(m // tm,),
        in_specs=[pl.BlockSpec((tm, FOURIER_WIDTH), lambda i: (i, 0)),
                  pl.BlockSpec((FGROUP_DIM, 2 * FGROUP_DIM), lambda i: (0, 0))],
        out_specs=[pl.BlockSpec((tm, FOURIER_WIDTH), lambda i: (i, 0)),
                   pl.BlockSpec((tm, FOURIER_WIDTH), lambda i: (i, 0))],
        out_shape=[out, out],
        compiler_params=_params("parallel"),
        name="chan_dft",
    )(u, chan_tab)


def _pos_dft_kernel(c_ref, s_ref, a_ref, b_ref, o_ref):
    re = jnp.dot(c_ref[...], a_ref[...], preferred_element_type=F32)
    re = re - jnp.dot(s_ref[...], b_ref[...], preferred_element_type=F32)
    o_ref[...] = re.astype(o_ref.dtype)


def _pos_dft_call(pos_cos, pos_sin, a, b, batch, seq, tm=512):
    nt = seq // tm
    return pl.pallas_call(
        _pos_dft_kernel,
        grid=(nt, batch),
        in_specs=[pl.BlockSpec((tm, seq), lambda i, bb: (i, 0)),
                  pl.BlockSpec((tm, seq), lambda i, bb: (i, 0)),
                  pl.BlockSpec((seq, FOURIER_WIDTH), lambda i, bb: (bb, 0)),
                  pl.BlockSpec((seq, FOURIER_WIDTH), lambda i, bb: (bb, 0))],
        out_specs=pl.BlockSpec((tm, FOURIER_WIDTH), lambda i, bb: (bb * nt + i, 0)),
        out_shape=jax.ShapeDtypeStruct((batch * seq, FOURIER_WIDTH), BF16),
        compiler_params=_params("parallel", "arbitrary"),
        name="pos_dft",
    )(pos_cos, pos_sin, a, b)


def _mix_kernel(h_ref, a_ref, f_ref, wf_ref, bf_ref, wo_ref, g_ref, b_ref, o_ref):
    parts = []
    for gi in range(N_FGROUPS):
        cols = slice(gi * FGROUP_DIM, (gi + 1) * FGROUP_DIM)
        parts.append(jnp.dot(f_ref[:, cols], wf_ref[gi], preferred_element_type=F32))
    y = jnp.concatenate(parts, axis=-1) + bf_ref[...]
    mix = jnp.dot(a_ref[...], wo_ref[:ATT_WIDTH, :], preferred_element_type=F32)
    mix = mix + jnp.dot(y.astype(BF16), wo_ref[ATT_WIDTH:, :], preferred_element_type=F32)
    o_ref[...] = _layer_norm(ALPHA * h_ref[...] + mix, g_ref[...], b_ref[...])


def _mix_call(h2, a, f, wf_bf, b_f, wo_bf, g, b, tm=1024):
    m, d = h2.shape
    return pl.pallas_call(
        _mix_kernel,
        grid=(m // tm,),
        in_specs=[pl.BlockSpec((tm, d), lambda i: (i, 0)),
                  pl.BlockSpec((tm, ATT_WIDTH), lambda i: (i, 0)),
                  pl.BlockSpec((tm, FOURIER_WIDTH), lambda i: (i, 0)),
                  pl.BlockSpec((N_FGROUPS, FGROUP_DIM, FGROUP_DIM), lambda i: (0, 0, 0)),
                  pl.BlockSpec((1, FOURIER_WIDTH), lambda i: (0, 0)),
                  pl.BlockSpec((d, d), lambda i: (0, 0)),
                  pl.BlockSpec((1, d), lambda i: (0, 0)),
                  pl.BlockSpec((1, d), lambda i: (0, 0))],
        out_specs=pl.BlockSpec((tm, d), lambda i: (i, 0)),
        out_shape=jax.ShapeDtypeStruct((m, d), F32),
        compiler_params=_params("parallel"),
        name="mix_ln1",
    )(h2, a, f, wf_bf, b_f.reshape(1, -1), wo_bf, g.reshape(1, d), b.reshape(1, d))


FFN_CHUNK = D_FF // 2


def _ffn_kernel(h_ref, wgu_ref, wd_ref, g_ref, b_ref, o_ref):
    h = h_ref[...]
    hb = h.astype(BF16)
    acc = None
    for j in range(D_FF // FFN_CHUNK):
        lo = j * FFN_CHUNK
        gate = jnp.dot(hb, wgu_ref[:, lo:lo + FFN_CHUNK], preferred_element_type=F32)
        up = jnp.dot(hb, wgu_ref[:, D_FF + lo:D_FF + lo + FFN_CHUNK], preferred_element_type=F32)
        act = (gate * jax.nn.sigmoid(gate) * up).astype(BF16)
        part = jnp.dot(act, wd_ref[lo:lo + FFN_CHUNK, :], preferred_element_type=F32)
        acc = part if acc is None else acc + part
    o_ref[...] = _layer_norm(ALPHA * h + acc, g_ref[...], b_ref[...])


def _ffn_call(h2, wgu_bf, wd_bf, g, b, tm=512):
    m, d = h2.shape
    const = dict(pipeline_mode=pl.Buffered(1))
    return pl.pallas_call(
        _ffn_kernel,
        grid=(m // tm,),
        in_specs=[pl.BlockSpec((tm, d), lambda i: (i, 0)),
                  pl.BlockSpec((d, 2 * D_FF), lambda i: (0, 0), **const),
                  pl.BlockSpec((D_FF, d), lambda i: (0, 0), **const),
                  pl.BlockSpec((1, d), lambda i: (0, 0)),
                  pl.BlockSpec((1, d), lambda i: (0, 0))],
        out_specs=pl.BlockSpec((tm, d), lambda i: (i, 0)),
        out_shape=jax.ShapeDtypeStruct((m, d), F32),
        compiler_params=_params("parallel"),
        name="ffn_ln2",
    )(h2, wgu_bf, wd_bf, g.reshape(1, d), b.reshape(1, d))


def kernel(x, ln_in_g, ln_in_b, w_in, lam_params, subln_g, w_f, b_f, w_o, ln1_g, ln1_b, w_gu, w_down, ln2_g, ln2_b):
    batch, seq, d = x.shape
    assert d == D_MODEL and seq % ATT_KBLK == 0 and seq % (STEP_TILES * ATT_TQ) == 0
    chan_tab, pos_cos, pos_sin = _dft_tables(seq)
    h = _ln_call(x.reshape(batch * seq, d), ln_in_g, ln_in_b)
    for l in range(DEPTH):
        q, kp, vt, nrm, u = _proj_call(h, w_in[l].astype(BF16), batch, seq)
        a = _attn_call(q, kp, vt, nrm, lam_params[l], subln_g[l], batch, seq, _lambda_init(l))
        fa, fb = _chan_dft_call(u, chan_tab)
        f = _pos_dft_call(pos_cos, pos_sin, fa, fb, batch, seq)
        h = _mix_call(h, a, f, w_f[l].astype(BF16), b_f[l], w_o[l].astype(BF16), ln1_g[l], ln1_b[l])
        h = _ffn_call(h, w_gu[l].astype(BF16), w_down[l].astype(BF16), ln2_g[l], ln2_b[l])
    return h.reshape(batch, seq, d)
```

```python
import functools
import math

import jax
import jax.numpy as jnp
from jax import lax
from jax.experimental import pallas as pl
from jax.experimental.pallas import tpu as pltpu

D_MODEL = 1024
DEPTH = 2
ATT_WIDTH = 512
FOURIER_WIDTH = 512
N_HEADS = 8
HEAD_DIM = 32
V_HEAD_DIM = 64
N_FGROUPS = 4
FGROUP_DIM = 128
IN_COLS = 2048
D_FF = 2816
LN_EPS = 1e-5
SUBLN_EPS = 1e-5
ALPHA = (2.0 * DEPTH) ** 0.25
LOG2E = math.log2(math.e)

LANES = 128
BF16_SUBLANES = 16
VMEM_LIMIT = 56 * 1024 * 1024

BF16 = jnp.bfloat16
F32 = jnp.float32


def _lambda_init(layer_idx):
    return 0.8 - 0.6 * math.exp(-0.3 * layer_idx)


def _layer_norm(x, g, b):
    mu = jnp.mean(x, axis=-1, keepdims=True)
    xc = x - mu
    var = jnp.mean(xc * xc, axis=-1, keepdims=True)
    return xc * lax.rsqrt(var + LN_EPS) * g + b


def _params(*sem):
    return pltpu.CompilerParams(dimension_semantics=sem, vmem_limit_bytes=VMEM_LIMIT)


def _split3(x):
    hi = x.astype(BF16).astype(F32)
    r = x - hi
    mid = r.astype(BF16).astype(F32)
    lo = (r - mid).astype(BF16).astype(F32)
    return hi, mid, lo


def _ln_kernel(x_ref, g_ref, b_ref, o_ref):
    o_ref[...] = _layer_norm(x_ref[...], g_ref[...], b_ref[...])


def _ln_call(x2, g, b, tm=1024):
    m, d = x2.shape
    return pl.pallas_call(
        _ln_kernel,
        grid=(m // tm,),
        in_specs=[pl.BlockSpec((tm, d), lambda i: (i, 0)),
                  pl.BlockSpec((1, d), lambda i: (0, 0)),
                  pl.BlockSpec((1, d), lambda i: (0, 0))],
        out_specs=pl.BlockSpec((tm, d), lambda i: (i, 0)),
        out_shape=jax.ShapeDtypeStruct((m, d), F32),
        compiler_params=_params("parallel"),
        name="ln_in",
    )(x2, g.reshape(1, d), b.reshape(1, d))


ATT_TQ = 256
ATT_KBLK = 4096
STEP_TILES = 2
HEADS_PER_STEP = LANES // V_HEAD_DIM
CHAINS_PER_STEP = 2 * HEADS_PER_STEP
VT_ROWS = V_HEAD_DIM + BF16_SUBLANES
KP_WIDTH = 2 * LANES
MASK_LANE = 6
MASK_BIG = 2.0 ** 100
NORM_ROUND_UP = 1.05
MAX_UNSHIFTED_SCORE = 50.0


def _slice_norms(x_bf):
    x32 = x_bf.astype(F32)
    seg = lax.broadcasted_iota(jnp.int32, (ATT_WIDTH, ATT_WIDTH), 0) // HEAD_DIM
    col = lax.broadcasted_iota(jnp.int32, (ATT_WIDTH, ATT_WIDTH), 1)
    ind = jnp.where(((col % LANES) < CHAINS_PER_STEP)
                    & (seg == (col // LANES) * CHAINS_PER_STEP + col % LANES), 1.0, 0.0).astype(BF16)
    return jnp.dot((x32 * x32).astype(BF16), ind, preferred_element_type=F32) * NORM_ROUND_UP


def _row_max(x, groups=8):
    r, c = x.shape
    return jnp.max(jnp.max(x.reshape(groups, r // groups, c), axis=0), axis=0, keepdims=True)


def _proj_kernel(h_ref, w_ref, q_ref, kp_ref, vt_ref, nrm_ref, u_ref, *, seq, tm):
    i = pl.program_id(0)
    hb = h_ref[...].astype(BF16)
    qf = jnp.dot(hb, w_ref[:, 0:ATT_WIDTH], preferred_element_type=F32)
    qb = (qf * (HEAD_DIM ** -0.5 * LOG2E)).astype(BF16)
    q_ref[...] = qb

    kf = jnp.dot(hb, w_ref[:, ATT_WIDTH:2 * ATT_WIDTH], preferred_element_type=F32).astype(BF16)
    pos = (i * tm) % seq + lax.broadcasted_iota(jnp.int32, (tm, 1), 0)
    hi, mid, lo = _split3((pos - seq // 2).astype(F32) * LOG2E)
    lane = lax.broadcasted_iota(jnp.int32, (tm, LANES), 1)
    aux = jnp.where(lane < 3, 1.0,
                    jnp.where(lane == 3, hi, jnp.where(lane == 4, mid, jnp.where(lane == 5, lo, 0.0))))
    aux = aux.astype(BF16)
    for p in range(N_HEADS // HEADS_PER_STEP):
        kp_ref[:, p * KP_WIDTH:p * KP_WIDTH + LANES] = kf[:, p * LANES:(p + 1) * LANES]
        kp_ref[:, p * KP_WIDTH + LANES:(p + 1) * KP_WIDTH] = aux

    nrm_ref[0:8, :] = jnp.broadcast_to(_row_max(_slice_norms(qb)), (8, ATT_WIDTH))
    nrm_ref[8:16, :] = jnp.broadcast_to(_row_max(_slice_norms(kf)), (8, ATT_WIDTH))

    vf = jnp.dot(hb, w_ref[:, 2 * ATT_WIDTH:3 * ATT_WIDTH], preferred_element_type=F32)
    vtr = vf.T.astype(BF16)
    ones = jnp.ones((BF16_SUBLANES, tm), BF16)
    for hd in range(N_HEADS):
        vt_ref[hd * VT_ROWS:hd * VT_ROWS + V_HEAD_DIM, :] = vtr[hd * V_HEAD_DIM:(hd + 1) * V_HEAD_DIM, :]
        vt_ref[hd * VT_ROWS + V_HEAD_DIM:(hd + 1) * VT_ROWS, :] = ones

    u_ref[...] = jnp.dot(hb, w_ref[:, 3 * ATT_WIDTH:], preferred_element_type=F32).astype(BF16)


def _proj_call(h2, w_bf, batch, seq, tm=1024):
    m, d = h2.shape
    n = w_bf.shape[1]
    tiles_per_seq = seq // tm
    kern = functools.partial(_proj_kernel, seq=seq, tm=tm)
    return pl.pallas_call(
        kern,
        grid=(m // tm,),
        in_specs=[pl.BlockSpec((tm, d), lambda i: (i, 0)),
                  pl.BlockSpec((d, n), lambda i: (0, 0))],
        out_specs=[pl.BlockSpec((tm, ATT_WIDTH), lambda i: (i, 0)),
                   pl.BlockSpec((tm, 4 * KP_WIDTH), lambda i: (i, 0)),
                   pl.BlockSpec((None, N_HEADS * VT_ROWS, tm),
                                lambda i: (i // tiles_per_seq, 0, i % tiles_per_seq)),
                   pl.BlockSpec((None, 16, ATT_WIDTH), lambda i: (i, 0, 0)),
                   pl.BlockSpec((tm, FOURIER_WIDTH), lambda i: (i, 0))],
        out_shape=[jax.ShapeDtypeStruct((m, ATT_WIDTH), BF16),
                   jax.ShapeDtypeStruct((m, 4 * KP_WIDTH), BF16),
                   jax.ShapeDtypeStruct((batch, N_HEADS * VT_ROWS, seq), BF16),
                   jax.ShapeDtypeStruct((m // tm, 16, ATT_WIDTH), F32),
                   jax.ShapeDtypeStruct((m, FOURIER_WIDTH), BF16)],
        compiler_params=_params("parallel"),
        name="proj",
    )(h2, w_bf)


def _attn_kernel(lam_ref, g_ref, nrm_ref, q_ref, kp_ref, kd_ref, vt_ref, vd_ref, o_ref, ks_ref, p_ref, flag_ref, *,
                 lam_init, seq):
    pair = pl.program_id(1)
    step = pl.program_id(2)
    lane_k = lax.broadcasted_iota(jnp.int32, (ATT_TQ, LANES), 1)
    diag_cols = jnp.where(lane_k == MASK_LANE, -MASK_BIG, 0.0).astype(BF16)

    def chunk_rows(chunk):
        return pl.ds(pl.multiple_of(chunk * ATT_TQ, ATT_TQ), ATT_TQ)

    def enter_tile(tile):
        chunk = step * STEP_TILES + tile

        @pl.when(chunk > 0)
        def _():
            ks_ref[chunk_rows(chunk - 1), LANES:] = kp_ref[chunk_rows(chunk - 1), LANES:]

        ks_ref[chunk_rows(chunk), LANES:] = diag_cols

    @pl.when(step == 0)
    def _():
        ks_ref[:, :LANES] = kp_ref[:, :LANES]
        ks_ref[:, LANES:] = -kp_ref[:, LANES:]
        nrm = jnp.max(nrm_ref[...], axis=0)
        lane = lax.broadcasted_iota(jnp.int32, (1, LANES), 1)
        prod = jnp.where(lane < CHAINS_PER_STEP, nrm[0:1, :] * nrm[8:9, :], 0.0)
        flag_ref[0] = (jnp.max(prod) <= MAX_UNSHIFTED_SCORE ** 2).astype(jnp.int32)

    lp = lam_ref[...]
    lam = (jnp.exp(jnp.sum(lp[0:1] * lp[1:2], axis=-1, keepdims=True))
           - jnp.exp(jnp.sum(lp[2:3] * lp[3:4], axis=-1, keepdims=True)) + lam_init)

    lane_q = lax.broadcasted_iota(jnp.int32, (ATT_TQ, LANES), 1)
    zeros = jnp.zeros((ATT_TQ, LANES), BF16)
    rr = lax.broadcasted_iota(jnp.int32, (ATT_TQ, ATT_TQ), 0)
    cc = lax.broadcasted_iota(jnp.int32, (ATT_TQ, ATT_TQ), 1)
    diag_dist = jnp.abs(rr - cc).astype(F32) * LOG2E
    diag_dist = jnp.concatenate([diag_dist, diag_dist], axis=1)
    nt_dims = (((1,), (1,)), ((), ()))
    g_col = g_ref[...]

    def stage_operands(tile, hh):
        tile_rows = slice(tile * ATT_TQ, (tile + 1) * ATT_TQ)
        q32 = q_ref[tile_rows, :].astype(F32)
        pos = (step * STEP_TILES + tile) * ATT_TQ + lax.broadcasted_iota(jnp.int32, (ATT_TQ, 1), 0)
        hi, mid, lo = _split3((pos - seq // 2).astype(F32) * LOG2E)
        base = jnp.where(lane_q == 0, -hi,
                         jnp.where(lane_q == 1, -mid,
                                   jnp.where(lane_q == 2, -lo, jnp.where(lane_q < 6, 1.0, 0.0))))
        head = pair * HEADS_PER_STEP + hh
        slope = jnp.exp2(-jnp.full((1, 1), head + 1, jnp.int32).astype(F32))
        aux = jnp.where(lane_q == MASK_LANE, 1.0, base * slope).astype(BF16)
        q_main, q_diag = [], []
        for c in range(2):
            chain = hh * 2 + c
            qm = jnp.where(lane_q // HEAD_DIM == chain, q32, 0.0).astype(BF16)
            q_main.append(jnp.concatenate([qm, aux], axis=1))
            q_diag.append(jnp.concatenate([qm, zeros], axis=1))
        q_main = jnp.concatenate(q_main, axis=0)
        q_diag = jnp.concatenate(q_diag, axis=0)
        sd = lax.dot_general(kd_ref[tile_rows, :], q_diag, nt_dims, preferred_element_type=F32)
        sd = sd - slope * diag_dist
        vt = vt_ref[hh * VT_ROWS:(hh + 1) * VT_ROWS, :]
        vd = vd_ref[hh * VT_ROWS:(hh + 1) * VT_ROWS, tile_rows]
        return q_main, sd, vt, vd

    def finish(tile, accs):
        outs = []
        for acc in accs:
            norm = []
            for c in range(2):
                part = acc[:, c * ATT_TQ:(c + 1) * ATT_TQ]
                denom = part[V_HEAD_DIM:V_HEAD_DIM + 1, :]
                norm.append(part[:V_HEAD_DIM, :] * (1.0 / denom))
            a_t = norm[0] - lam * norm[1]
            a_t = a_t * lax.rsqrt(jnp.mean(a_t * a_t, axis=0, keepdims=True) + SUBLN_EPS)
            outs.append(a_t * g_col * (1.0 - lam_init))
        o_ref[tile * ATT_TQ:(tile + 1) * ATT_TQ, :] = jnp.concatenate(outs, axis=0).T.astype(o_ref.dtype)

    stages = [(tile, hh) for tile in range(STEP_TILES) for hh in range(HEADS_PER_STEP)]
    key_blocks = [slice(b * ATT_KBLK, (b + 1) * ATT_KBLK) for b in range(seq // ATT_KBLK)]

    @pl.when(flag_ref[0] == 1)
    def _():
        ops = [None] * len(stages)
        accs = [None] * len(stages)
        for n in range(len(stages) + 1):
            if n < len(stages):
                tile, hh = stages[n]
                if hh == 0:
                    enter_tile(tile)
                ops[n] = stage_operands(tile, hh)
            if n >= 1:
                _, sd, _, vd = ops[n - 1]
                accs[n - 1] = jnp.dot(vd, jnp.exp2(sd).astype(BF16), preferred_element_type=F32)
            for rows in key_blocks:
                if n < len(stages):
                    st = lax.dot_general(ks_ref[rows, :], ops[n][0], nt_dims, preferred_element_type=F32)
                    p_ref[n % 2, rows, :] = jnp.exp2(st).astype(BF16)
                if n >= 1:
                    vt = ops[n - 1][2]
                    accs[n - 1] = accs[n - 1] + jnp.dot(vt[:, rows], p_ref[(n - 1) % 2, rows, :],
                                                        preferred_element_type=F32)
            if n >= 1 and stages[n - 1][1] == HEADS_PER_STEP - 1:
                tile = stages[n - 1][0]
                finish(tile, accs[n - HEADS_PER_STEP:n])

    @pl.when(flag_ref[0] != 1)
    def _():
        for tile in range(STEP_TILES):
            enter_tile(tile)
            accs = []
            for hh in range(HEADS_PER_STEP):
                q_main, sd, vt, vd = stage_operands(tile, hh)
                st = lax.dot_general(ks_ref[...], q_main, nt_dims, preferred_element_type=F32)
                m = jnp.maximum(jnp.max(sd, axis=0, keepdims=True), jnp.max(st, axis=0, keepdims=True))
                accs.append(jnp.dot(vd, jnp.exp2(sd - m).astype(BF16), preferred_element_type=F32)
                            + jnp.dot(vt, jnp.exp2(st - m).astype(BF16), preferred_element_type=F32))
            finish(tile, accs)


def _attn_call(q, kp, vt, nrm, lam_params, subln_g, batch, seq, lam_init):
    n_pairs = N_HEADS // HEADS_PER_STEP
    step_rows = STEP_TILES * ATT_TQ
    nsteps = seq // step_rows
    tiles_per_seq = nrm.shape[0] // batch
    kern = functools.partial(_attn_kernel, lam_init=lam_init, seq=seq)
    pair_rows = HEADS_PER_STEP * VT_ROWS
    return pl.pallas_call(
        kern,
        grid=(batch, n_pairs, nsteps),
        in_specs=[pl.BlockSpec((4, HEAD_DIM), lambda b, p, i: (0, 0)),
                  pl.BlockSpec((V_HEAD_DIM, 1), lambda b, p, i: (0, 0)),
                  pl.BlockSpec((tiles_per_seq, 16, LANES), lambda b, p, i: (b, 0, p)),
                  pl.BlockSpec((step_rows, LANES), lambda b, p, i: (b * nsteps + i, p)),
                  pl.BlockSpec((seq, KP_WIDTH), lambda b, p, i: (b, p)),
                  pl.BlockSpec((step_rows, KP_WIDTH), lambda b, p, i: (b * nsteps + i, p)),
                  pl.BlockSpec((None, pair_rows, seq), lambda b, p, i: (b, p, 0)),
                  pl.BlockSpec((None, pair_rows, step_rows), lambda b, p, i: (b, p, i))],
        out_specs=pl.BlockSpec((step_rows, LANES), lambda b, p, i: (b * nsteps + i, p)),
        out_shape=jax.ShapeDtypeStruct((batch * seq, ATT_WIDTH), BF16),
        scratch_shapes=[pltpu.VMEM((seq, KP_WIDTH), BF16),
                        pltpu.VMEM((2, seq, 2 * ATT_TQ), BF16),
                        pltpu.SMEM((1,), jnp.int32)],
        compiler_params=_params("parallel", "parallel", "arbitrary"),
        name="diff_attn",
    )(lam_params, subln_g.reshape(V_HEAD_DIM, 1), nrm, q, kp, kp, vt, vt)


FOLD_BLK = 256


def _dft_tables(seq):
    c = FGROUP_DIM
    ci = lax.broadcasted_iota(jnp.int32, (c, c), 0) * lax.broadcasted_iota(jnp.int32, (c, c), 1) % c
    ang_c = ci.astype(F32) * (2.0 * math.pi / c)
    cc, sc = jnp.cos(ang_c) * c ** -0.5, jnp.sin(ang_c) * c ** -0.5
    chan = jnp.concatenate([jnp.concatenate([cc, sc], axis=1), jnp.concatenate([cc, -sc], axis=1)], axis=0)
    half = seq // 2
    k = lax.broadcasted_iota(jnp.int32, (half, half), 0)
    n = lax.broadcasted_iota(jnp.int32, (half, half), 1) + 1
    ang_s = (k * n % seq).astype(F32) * (2.0 * math.pi / seq)
    weight = jnp.where(n == half, 0.5, 1.0) * seq ** -0.5
    pos_cos = (jnp.cos(ang_s) * weight).astype(BF16)
    pos_sin = (jnp.sin(ang_s) * weight).astype(BF16)
    n1 = lax.broadcasted_iota(jnp.int32, (BF16_SUBLANES, half), 1) + 1
    row = lax.broadcasted_iota(jnp.int32, (BF16_SUBLANES, half), 0)
    alt = jnp.where(row == 0, jnp.where(n1 % 2 == 0, 1.0, -1.0), 0.0)
    alt = (alt * jnp.where(n1 == half, 0.5, 1.0) * seq ** -0.5).astype(BF16)
    return chan.astype(BF16), pos_cos, pos_sin, alt


def _select(rows, cols, offset):
    r = lax.broadcasted_iota(jnp.int32, (rows, cols), 0)
    c = lax.broadcasted_iota(jnp.int32, (rows, cols), 1)
    return jnp.where(c == offset - r, 1.0, 0.0).astype(BF16)


def _fold_kernel(u_ref, t_ref, ae_ref, bo_ref, a0_ref, *, seq):
    half = seq // 2
    t = t_ref[...]
    r = lax.broadcasted_iota(jnp.int32, (FOLD_BLK, 2 * FOLD_BLK), 0)
    c = lax.broadcasted_iota(jnp.int32, (FOLD_BLK, 2 * FOLD_BLK), 1)
    shift_one = jnp.where(c == r + 1, 1.0, 0.0).astype(BF16)
    reverse = _select(FOLD_BLK, FOLD_BLK, FOLD_BLK - 1)
    for i in range(half // FOLD_BLK):
        lo = i * FOLD_BLK
        fwd = jnp.dot(shift_one, u_ref[lo:lo + 2 * FOLD_BLK, :], preferred_element_type=F32).astype(BF16)
        hi = seq - lo - FOLD_BLK
        bwd = jnp.dot(reverse, u_ref[hi:hi + FOLD_BLK, :], preferred_element_type=F32).astype(BF16)
        for gi in range(N_FGROUPS):
            cols = slice(gi * FGROUP_DIM, (gi + 1) * FGROUP_DIM)
            both = jnp.concatenate([fwd[:, cols], bwd[:, cols]], axis=1)
            res = jnp.dot(both, t, preferred_element_type=F32)
            ae_ref[lo:lo + FOLD_BLK, cols] = res[:, :FGROUP_DIM].astype(ae_ref.dtype)
            bo_ref[lo:lo + FOLD_BLK, cols] = res[:, FGROUP_DIM:].astype(bo_ref.dtype)
    for gi in range(N_FGROUPS):
        cols = slice(gi * FGROUP_DIM, (gi + 1) * FGROUP_DIM)
        first = jnp.dot(u_ref[0:BF16_SUBLANES, cols], t[:FGROUP_DIM, :FGROUP_DIM], preferred_element_type=F32)
        a0_ref[:, cols] = jnp.broadcast_to(first[0:1, :], (a0_ref.shape[0], FGROUP_DIM))


def _fold_call(u, chan_tab, batch, seq):
    half = seq // 2
    out = jax.ShapeDtypeStruct((batch * half, FOURIER_WIDTH), BF16)
    return pl.pallas_call(
        functools.partial(_fold_kernel, seq=seq),
        grid=(batch,),
        in_specs=[pl.BlockSpec((seq, FOURIER_WIDTH), lambda b: (b, 0)),
                  pl.BlockSpec((2 * FGROUP_DIM, 2 * FGROUP_DIM), lambda b: (0, 0))],
        out_specs=[pl.BlockSpec((half, FOURIER_WIDTH), lambda b: (b, 0)),
                   pl.BlockSpec((half, FOURIER_WIDTH), lambda b: (b, 0)),
                   pl.BlockSpec((None, 8, FOURIER_WIDTH), lambda b: (b, 0, 0))],
        out_shape=[out, out, jax.ShapeDtypeStruct((batch, 8, FOURIER_WIDTH), F32)],
        compiler_params=_params("parallel"),
        name="dft_fold",
    )(u, chan_tab)


def _pos_dft_kernel(c_ref, s_ref, alt_ref, ae_ref, bo_ref, a0_ref, o_ref, *, seq):
    half = seq // 2
    ae = ae_ref[...]
    dc = a0_ref[0:1, :] * seq ** -0.5
    even = jnp.dot(c_ref[...], ae, preferred_element_type=F32) + dc
    odd = jnp.dot(s_ref[...], bo_ref[...], preferred_element_type=F32)
    o_ref[0:half, :] = (even - odd).astype(o_ref.dtype)
    mirror = (even + odd).astype(BF16)
    middle = jnp.dot(alt_ref[...], ae, preferred_element_type=F32)[0:1, :] + dc
    middle = jnp.broadcast_to(middle, (FOLD_BLK, FOURIER_WIDTH)).astype(BF16)
    take = _select(FOLD_BLK, 2 * FOLD_BLK, FOLD_BLK)
    nblk = half // FOLD_BLK
    for bt in range(nblk):
        upper = middle if bt == 0 else mirror[(nblk - bt) * FOLD_BLK:(nblk - bt + 1) * FOLD_BLK, :]
        lower = mirror[(nblk - 1 - bt) * FOLD_BLK:(nblk - bt) * FOLD_BLK, :]
        blk = jnp.dot(take, jnp.concatenate([lower, upper], axis=0), preferred_element_type=F32)
        o_ref[half + bt * FOLD_BLK:half + (bt + 1) * FOLD_BLK, :] = blk.astype(o_ref.dtype)


def _pos_dft_call(pos_cos, pos_sin, alt, ae, bo, a0, batch, seq):
    half = seq // 2
    const = dict(pipeline_mode=pl.Buffered(1))
    return pl.pallas_call(
        functools.partial(_pos_dft_kernel, seq=seq),
        grid=(batch,),
        in_specs=[pl.BlockSpec((half, half), lambda b: (0, 0), **const),
                  pl.BlockSpec((half, half), lambda b: (0, 0), **const),
                  pl.BlockSpec((BF16_SUBLANES, half), lambda b: (0, 0)),
                  pl.BlockSpec((half, FOURIER_WIDTH), lambda b: (b, 0)),
                  pl.BlockSpec((half, FOURIER_WIDTH), lambda b: (b, 0)),
                  pl.BlockSpec((None, 8, FOURIER_WIDTH), lambda b: (b, 0, 0))],
        out_specs=pl.BlockSpec((seq, FOURIER_WIDTH), lambda b: (b, 0)),
        out_shape=jax.ShapeDtypeStruct((batch * seq, FOURIER_WIDTH), BF16),
        compiler_params=_params("parallel"),
        name="pos_dft",
    )(pos_cos, pos_sin, alt, ae, bo, a0)


def _mix_kernel(h_ref, a_ref, f_ref, wf_ref, bf_ref, wo_ref, g_ref, b_ref, o_ref):
    parts = []
    for gi in range(N_FGROUPS):
        cols = slice(gi * FGROUP_DIM, (gi + 1) * FGROUP_DIM)
        parts.append(jnp.dot(f_ref[:, cols], wf_ref[gi], preferred_element_type=F32))
    y = jnp.concatenate(parts, axis=-1) + bf_ref[...]
    mix = jnp.dot(a_ref[...], wo_ref[:ATT_WIDTH, :], preferred_element_type=F32)
    mix = mix + jnp.dot(y.astype(BF16), wo_ref[ATT_WIDTH:, :], preferred_element_type=F32)
    o_ref[...] = _layer_norm(ALPHA * h_ref[...] + mix, g_ref[...], b_ref[...])


def _mix_call(h2, a, f, wf_bf, b_f, wo_bf, g, b, tm=1024):
    m, d = h2.shape
    return pl.pallas_call(
        _mix_kernel,
        grid=(m // tm,),
        in_specs=[pl.BlockSpec((tm, d), lambda i: (i, 0)),
                  pl.BlockSpec((tm, ATT_WIDTH), lambda i: (i, 0)),
                  pl.BlockSpec((tm, FOURIER_WIDTH), lambda i: (i, 0)),
                  pl.BlockSpec((N_FGROUPS, FGROUP_DIM, FGROUP_DIM), lambda i: (0, 0, 0)),
                  pl.BlockSpec((1, FOURIER_WIDTH), lambda i: (0, 0)),
                  pl.BlockSpec((d, d), lambda i: (0, 0)),
                  pl.BlockSpec((1, d), lambda i: (0, 0)),
                  pl.BlockSpec((1, d), lambda i: (0, 0))],
        out_specs=pl.BlockSpec((tm, d), lambda i: (i, 0)),
        out_shape=jax.ShapeDtypeStruct((m, d), F32),
        compiler_params=_params("parallel"),
        name="mix_ln1",
    )(h2, a, f, wf_bf, b_f.reshape(1, -1), wo_bf, g.reshape(1, d), b.reshape(1, d))


FFN_CHUNK = D_FF // 2


def _ffn_kernel(h_ref, wgu_ref, wd_ref, g_ref, b_ref, o_ref):
    h = h_ref[...]
    hb = h.astype(BF16)
    acc = None
    for j in range(D_FF // FFN_CHUNK):
        lo = j * FFN_CHUNK
        gate = jnp.dot(hb, wgu_ref[:, lo:lo + FFN_CHUNK], preferred_element_type=F32)
        up = jnp.dot(hb, wgu_ref[:, D_FF + lo:D_FF + lo + FFN_CHUNK], preferred_element_type=F32)
        act = (gate * jax.nn.sigmoid(gate) * up).astype(BF16)
        part = jnp.dot(act, wd_ref[lo:lo + FFN_CHUNK, :], preferred_element_type=F32)
        acc = part if acc is None else acc + part
    o_ref[...] = _layer_norm(ALPHA * h + acc, g_ref[...], b_ref[...])


def _ffn_call(h2, wgu_bf, wd_bf, g, b, tm=512):
    m, d = h2.shape
    const = dict(pipeline_mode=pl.Buffered(1))
    return pl.pallas_call(
        _ffn_kernel,
        grid=(m // tm,),
        in_specs=[pl.BlockSpec((tm, d), lambda i: (i, 0)),
                  pl.BlockSpec((d, 2 * D_FF), lambda i: (0, 0), **const),
                  pl.BlockSpec((D_FF, d), lambda i: (0, 0), **const),
                  pl.BlockSpec((1, d), lambda i: (0, 0)),
                  pl.BlockSpec((1, d), lambda i: (0, 0))],
        out_specs=pl.BlockSpec((tm, d), lambda i: (i, 0)),
        out_shape=jax.ShapeDtypeStruct((m, d), F32),
        compiler_params=_params("parallel"),
        name="ffn_ln2",
    )(h2, wgu_bf, wd_bf, g.reshape(1, d), b.reshape(1, d))


def kernel(x, ln_in_g, ln_in_b, w_in, lam_params, subln_g, w_f, b_f, w_o, ln1_g, ln1_b, w_gu, w_down, ln2_g, ln2_b):
    batch, seq, d = x.shape
    assert d == D_MODEL and seq % ATT_KBLK == 0 and seq % (STEP_TILES * ATT_TQ) == 0
    chan_tab, pos_cos, pos_sin, pos_alt = _dft_tables(seq)
    h = _ln_call(x.reshape(batch * seq, d), ln_in_g, ln_in_b)
    for l in range(DEPTH):
        q, kp, vt, nrm, u = _proj_call(h, w_in[l].astype(BF16), batch, seq)
        a = _attn_call(q, kp, vt, nrm, lam_params[l], subln_g[l], batch, seq, _lambda_init(l))
        fe, fo, f0 = _fold_call(u, chan_tab, batch, seq)
        f = _pos_dft_call(pos_cos, pos_sin, pos_alt, fe, fo, f0, batch, seq)
        h = _mix_call(h, a, f, w_f[l].astype(BF16), b_f[l], w_o[l].astype(BF16), ln1_g[l], ln1_b[l])
        h = _ffn_call(h, w_gu[l].astype(BF16), w_down[l].astype(BF16), ln2_g[l], ln2_b[l])
    return h.reshape(batch, seq, d)
```

```python
import functools
import math

import jax
import jax.numpy as jnp
from jax import lax
from jax.experimental import pallas as pl
from jax.experimental.pallas import tpu as pltpu

D_MODEL = 1024
DEPTH = 2
ATT_WIDTH = 512
FOURIER_WIDTH = 512
N_HEADS = 8
HEAD_DIM = 32
V_HEAD_DIM = 64
N_FGROUPS = 4
FGROUP_DIM = 128
IN_COLS = 2048
D_FF = 2816
LN_EPS = 1e-5
SUBLN_EPS = 1e-5
ALPHA = (2.0 * DEPTH) ** 0.25
LOG2E = math.log2(math.e)

LANES = 128
BF16_SUBLANES = 16
VMEM_LIMIT = 56 * 1024 * 1024

BF16 = jnp.bfloat16
F32 = jnp.float32


def _lambda_init(layer_idx):
    return 0.8 - 0.6 * math.exp(-0.3 * layer_idx)


def _layer_norm(x, g, b):
    mu = jnp.mean(x, axis=-1, keepdims=True)
    xc = x - mu
    var = jnp.mean(xc * xc, axis=-1, keepdims=True)
    return xc * lax.rsqrt(var + LN_EPS) * g + b


def _params(*sem):
    return pltpu.CompilerParams(dimension_semantics=sem, vmem_limit_bytes=VMEM_LIMIT)


def _split3(x):
    hi = x.astype(BF16).astype(F32)
    r = x - hi
    mid = r.astype(BF16).astype(F32)
    lo = (r - mid).astype(BF16).astype(F32)
    return hi, mid, lo


ATT_TQ = 256
ATT_KBLK = 4096
STEP_TILES = 4
HEADS_PER_STEP = LANES // V_HEAD_DIM
CHAINS_PER_STEP = 2 * HEADS_PER_STEP
VT_ROWS = V_HEAD_DIM + BF16_SUBLANES
KP_WIDTH = 2 * LANES
MASK_LANE = 6
MASK_BIG = 2.0 ** 100
NORM_ROUND_UP = 1.05
MAX_UNSHIFTED_SCORE = 50.0


def _slice_norms(x_bf):
    x32 = x_bf.astype(F32)
    seg = lax.broadcasted_iota(jnp.int32, (ATT_WIDTH, LANES), 0) // HEAD_DIM
    col = lax.broadcasted_iota(jnp.int32, (ATT_WIDTH, LANES), 1)
    ind = jnp.where(seg == col, 1.0, 0.0).astype(BF16)
    return jnp.dot((x32 * x32).astype(BF16), ind, preferred_element_type=F32) * NORM_ROUND_UP


def _row_max(x, groups=8):
    r, c = x.shape
    return jnp.max(jnp.max(x.reshape(groups, r // groups, c), axis=0), axis=0, keepdims=True)


def _proj_kernel(*refs, seq, tm, pre_ln):
    if pre_ln:
        x_ref, g_ref, b_ref, w_ref, h_ref, q_ref, kp_ref, vt_ref, nrm_ref, u_ref = refs
        h = _layer_norm(x_ref[...], g_ref[...], b_ref[...])
        h_ref[...] = h
    else:
        x_ref, w_ref, q_ref, kp_ref, vt_ref, nrm_ref, u_ref = refs
        h = x_ref[...]
    i = pl.program_id(0)
    hb = h.astype(BF16)
    qf = jnp.dot(hb, w_ref[:, 0:ATT_WIDTH], preferred_element_type=F32)
    qb = (qf * (HEAD_DIM ** -0.5 * LOG2E)).astype(BF16)
    q_ref[...] = qb

    kf = jnp.dot(hb, w_ref[:, ATT_WIDTH:2 * ATT_WIDTH], preferred_element_type=F32).astype(BF16)
    pos = (i * tm) % seq + lax.broadcasted_iota(jnp.int32, (tm, 1), 0)
    hi, mid, lo = _split3((pos - seq // 2).astype(F32) * LOG2E)
    lane = lax.broadcasted_iota(jnp.int32, (tm, LANES), 1)
    aux = jnp.where(lane < 3, 1.0,
                    jnp.where(lane == 3, hi, jnp.where(lane == 4, mid, jnp.where(lane == 5, lo, 0.0))))
    aux = aux.astype(BF16)
    for p in range(N_HEADS // HEADS_PER_STEP):
        kp_ref[:, p * KP_WIDTH:p * KP_WIDTH + LANES] = kf[:, p * LANES:(p + 1) * LANES]
        kp_ref[:, p * KP_WIDTH + LANES:(p + 1) * KP_WIDTH] = aux

    nrm_ref[0:8, :] = jnp.broadcast_to(_row_max(_slice_norms(qb)), (8, LANES))
    nrm_ref[8:16, :] = jnp.broadcast_to(_row_max(_slice_norms(kf)), (8, LANES))

    vf = jnp.dot(hb, w_ref[:, 2 * ATT_WIDTH:3 * ATT_WIDTH], preferred_element_type=F32)
    vtr = vf.T.astype(BF16)
    ones = jnp.ones((BF16_SUBLANES, tm), BF16)
    for hd in range(N_HEADS):
        vt_ref[hd * VT_ROWS:hd * VT_ROWS + V_HEAD_DIM, :] = vtr[hd * V_HEAD_DIM:(hd + 1) * V_HEAD_DIM, :]
        vt_ref[hd * VT_ROWS + V_HEAD_DIM:(hd + 1) * VT_ROWS, :] = ones

    u_ref[...] = jnp.dot(hb, w_ref[:, 3 * ATT_WIDTH:], preferred_element_type=F32).astype(BF16)


def _proj_call(x2, w_bf, batch, seq, ln=None, tm=1024):
    m, d = x2.shape
    n = w_bf.shape[1]
    tiles_per_seq = seq // tm
    kern = functools.partial(_proj_kernel, seq=seq, tm=tm, pre_ln=ln is not None)
    row = lambda i: (i, 0)
    fixed = lambda i: (0, 0)
    in_specs = [pl.BlockSpec((tm, d), row), pl.BlockSpec((d, n), fixed)]
    args = [x2, w_bf]
    out_specs = [pl.BlockSpec((tm, ATT_WIDTH), row),
                 pl.BlockSpec((tm, 4 * KP_WIDTH), row),
                 pl.BlockSpec((None, N_HEADS * VT_ROWS, tm), lambda i: (i // tiles_per_seq, 0, i % tiles_per_seq)),
                 pl.BlockSpec((None, 16, LANES), lambda i: (i, 0, 0)),
                 pl.BlockSpec((tm, FOURIER_WIDTH), row)]
    out_shape = [jax.ShapeDtypeStruct((m, ATT_WIDTH), BF16),
                 jax.ShapeDtypeStruct((m, 4 * KP_WIDTH), BF16),
                 jax.ShapeDtypeStruct((batch, N_HEADS * VT_ROWS, seq), BF16),
                 jax.ShapeDtypeStruct((m // tm, 16, LANES), F32),
                 jax.ShapeDtypeStruct((m, FOURIER_WIDTH), BF16)]
    if ln is not None:
        in_specs[1:1] = [pl.BlockSpec((1, d), fixed), pl.BlockSpec((1, d), fixed)]
        args[1:1] = [ln[0].reshape(1, d), ln[1].reshape(1, d)]
        out_specs.insert(0, pl.BlockSpec((tm, d), row))
        out_shape.insert(0, jax.ShapeDtypeStruct((m, d), F32))
    return pl.pallas_call(
        kern,
        grid=(m // tm,),
        in_specs=in_specs,
        out_specs=out_specs,
        out_shape=out_shape,
        compiler_params=_params("parallel"),
        name="proj",
    )(*args)


def _attn_kernel(lam_ref, g_ref, nrm_ref, q_ref, kp_ref, kd_ref, vt_ref, vd_ref, o_ref, ks_ref, p_ref, flag_ref, *,
                 lam_init, seq):
    pair = pl.program_id(1)
    step = pl.program_id(2)
    lane_k = lax.broadcasted_iota(jnp.int32, (ATT_TQ, LANES), 1)
    diag_cols = jnp.where(lane_k == MASK_LANE, -MASK_BIG, 0.0).astype(BF16)

    def chunk_rows(chunk):
        return pl.ds(pl.multiple_of(chunk * ATT_TQ, ATT_TQ), ATT_TQ)

    def enter_tile(tile):
        chunk = step * STEP_TILES + tile

        @pl.when(chunk > 0)
        def _():
            ks_ref[chunk_rows(chunk - 1), LANES:] = kp_ref[chunk_rows(chunk - 1), LANES:]

        ks_ref[chunk_rows(chunk), LANES:] = diag_cols

    @pl.when(step == 0)
    def _():
        ks_ref[:, :LANES] = kp_ref[:, :LANES]
        ks_ref[:, LANES:] = -kp_ref[:, LANES:]
        nrm = jnp.max(nrm_ref[...], axis=0)
        lane = lax.broadcasted_iota(jnp.int32, (1, LANES), 1)
        prod = jnp.where(lane // CHAINS_PER_STEP == pair, nrm[0:1, :] * nrm[8:9, :], 0.0)
        flag_ref[0] = (jnp.max(prod) <= MAX_UNSHIFTED_SCORE ** 2).astype(jnp.int32)

    lp = lam_ref[...]
    lam = (jnp.exp(jnp.sum(lp[0:1] * lp[1:2], axis=-1, keepdims=True))
           - jnp.exp(jnp.sum(lp[2:3] * lp[3:4], axis=-1, keepdims=True)) + lam_init)

    lane_q = lax.broadcasted_iota(jnp.int32, (ATT_TQ, LANES), 1)
    zeros = jnp.zeros((ATT_TQ, LANES), BF16)
    rr = lax.broadcasted_iota(jnp.int32, (ATT_TQ, ATT_TQ), 0)
    cc = lax.broadcasted_iota(jnp.int32, (ATT_TQ, ATT_TQ), 1)
    diag_dist = jnp.abs(rr - cc).astype(F32) * LOG2E
    diag_dist = jnp.concatenate([diag_dist, diag_dist], axis=1)
    nt_dims = (((1,), (1,)), ((), ()))
    g_col = g_ref[...]

    def stage_operands(tile, hh):
        tile_rows = slice(tile * ATT_TQ, (tile + 1) * ATT_TQ)
        q32 = q_ref[tile_rows, :].astype(F32)
        pos = (step * STEP_TILES + tile) * ATT_TQ + lax.broadcasted_iota(jnp.int32, (ATT_TQ, 1), 0)
        hi, mid, lo = _split3((pos - seq // 2).astype(F32) * LOG2E)
        base = jnp.where(lane_q == 0, -hi,
                         jnp.where(lane_q == 1, -mid,
                                   jnp.where(lane_q == 2, -lo, jnp.where(lane_q < 6, 1.0, 0.0))))
        head = pair * HEADS_PER_STEP + hh
        slope = jnp.exp2(-jnp.full((1, 1), head + 1, jnp.int32).astype(F32))
        aux = jnp.where(lane_q == MASK_LANE, 1.0, base * slope).astype(BF16)
        q_main, q_diag = [], []
        for c in range(2):
            chain = hh * 2 + c
            qm = jnp.where(lane_q // HEAD_DIM == chain, q32, 0.0).astype(BF16)
            q_main.append(jnp.concatenate([qm, aux], axis=1))
            q_diag.append(jnp.concatenate([qm, zeros], axis=1))
        q_main = jnp.concatenate(q_main, axis=0)
        q_diag = jnp.concatenate(q_diag, axis=0)
        sd = lax.dot_general(kd_ref[tile_rows, :], q_diag, nt_dims, preferred_element_type=F32)
        sd = sd - slope * diag_dist
        vt = vt_ref[hh * VT_ROWS:(hh + 1) * VT_ROWS, :]
        vd = vd_ref[hh * VT_ROWS:(hh + 1) * VT_ROWS, tile_rows]
        return q_main, sd, vt, vd

    def finish(tile, accs):
        outs = []
        for acc in accs:
            norm = []
            for c in range(2):
                part = acc[:, c * ATT_TQ:(c + 1) * ATT_TQ]
                denom = part[V_HEAD_DIM:V_HEAD_DIM + 1, :]
                norm.append(part[:V_HEAD_DIM, :] * (1.0 / denom))
            a_t = norm[0] - lam * norm[1]
            a_t = a_t * lax.rsqrt(jnp.mean(a_t * a_t, axis=0, keepdims=True) + SUBLN_EPS)
            outs.append(a_t * g_col * (1.0 - lam_init))
        o_ref[tile * ATT_TQ:(tile + 1) * ATT_TQ, :] = jnp.concatenate(outs, axis=0).T.astype(o_ref.dtype)

    stages = [(tile, hh) for tile in range(STEP_TILES) for hh in range(HEADS_PER_STEP)]
    key_blocks = [slice(b * ATT_KBLK, (b + 1) * ATT_KBLK) for b in range(seq // ATT_KBLK)]

    @pl.when(flag_ref[0] == 1)
    def _():
        ops = [None] * len(stages)
        accs = [None] * len(stages)
        for n in range(len(stages) + 1):
            if n < len(stages):
                tile, hh = stages[n]
                if hh == 0:
                    enter_tile(tile)
                ops[n] = stage_operands(tile, hh)
            if n >= 1:
                _, sd, _, vd = ops[n - 1]
                accs[n - 1] = jnp.dot(vd, jnp.exp2(sd).astype(BF16), preferred_element_type=F32)
            for rows in key_blocks:
                if n < len(stages):
                    st = lax.dot_general(ks_ref[rows, :], ops[n][0], nt_dims, preferred_element_type=F32)
                    p_ref[n % 2, rows, :] = jnp.exp2(st).astype(BF16)
                if n >= 1:
                    vt = ops[n - 1][2]
                    accs[n - 1] = accs[n - 1] + jnp.dot(vt[:, rows], p_ref[(n - 1) % 2, rows, :],
                                                        preferred_element_type=F32)
            if n >= 1 and stages[n - 1][1] == HEADS_PER_STEP - 1:
                tile = stages[n - 1][0]
                finish(tile, accs[n - HEADS_PER_STEP:n])

    @pl.when(flag_ref[0] != 1)
    def _():
        for tile in range(STEP_TILES):
            enter_tile(tile)
            accs = []
            for hh in range(HEADS_PER_STEP):
                q_main, sd, vt, vd = stage_operands(tile, hh)
                st = lax.dot_general(ks_ref[...], q_main, nt_dims, preferred_element_type=F32)
                m = jnp.maximum(jnp.max(sd, axis=0, keepdims=True), jnp.max(st, axis=0, keepdims=True))
                accs.append(jnp.dot(vd, jnp.exp2(sd - m).astype(BF16), preferred_element_type=F32)
                            + jnp.dot(vt, jnp.exp2(st - m).astype(BF16), preferred_element_type=F32))
            finish(tile, accs)


def _attn_call(q, kp, vt, nrm, lam_params, subln_g, batch, seq, lam_init):
    n_pairs = N_HEADS // HEADS_PER_STEP
    step_rows = STEP_TILES * ATT_TQ
    nsteps = seq // step_rows
    tiles_per_seq = nrm.shape[0] // batch
    kern = functools.partial(_attn_kernel, lam_init=lam_init, seq=seq)
    pair_rows = HEADS_PER_STEP * VT_ROWS
    return pl.pallas_call(
        kern,
        grid=(batch, n_pairs, nsteps),
        in_specs=[pl.BlockSpec((4, HEAD_DIM), lambda b, p, i: (0, 0)),
                  pl.BlockSpec((V_HEAD_DIM, 1), lambda b, p, i: (0, 0)),
                  pl.BlockSpec((tiles_per_seq, 16, LANES), lambda b, p, i: (b, 0, 0)),
                  pl.BlockSpec((step_rows, LANES), lambda b, p, i: (b * nsteps + i, p)),
                  pl.BlockSpec((seq, KP_WIDTH), lambda b, p, i: (b, p)),
                  pl.BlockSpec((step_rows, KP_WIDTH), lambda b, p, i: (b * nsteps + i, p)),
                  pl.BlockSpec((None, pair_rows, seq), lambda b, p, i: (b, p, 0)),
                  pl.BlockSpec((None, pair_rows, step_rows), lambda b, p, i: (b, p, i))],
        out_specs=pl.BlockSpec((step_rows, LANES), lambda b, p, i: (b * nsteps + i, p)),
        out_shape=jax.ShapeDtypeStruct((batch * seq, ATT_WIDTH), BF16),
        scratch_shapes=[pltpu.VMEM((seq, KP_WIDTH), BF16),
                        pltpu.VMEM((2, seq, 2 * ATT_TQ), BF16),
                        pltpu.SMEM((1,), jnp.int32)],
        compiler_params=_params("parallel", "parallel", "arbitrary"),
        name="diff_attn",
    )(lam_params, subln_g.reshape(V_HEAD_DIM, 1), nrm, q, kp, kp, vt, vt)


FOLD_BLK = 256


def _dft_tables(seq):
    c = FGROUP_DIM
    ci = lax.broadcasted_iota(jnp.int32, (c, c), 0) * lax.broadcasted_iota(jnp.int32, (c, c), 1) % c
    ang_c = ci.astype(F32) * (2.0 * math.pi / c)
    cc, sc = jnp.cos(ang_c) * c ** -0.5, jnp.sin(ang_c) * c ** -0.5
    chan = jnp.concatenate([jnp.concatenate([cc, sc], axis=1), jnp.concatenate([cc, -sc], axis=1)], axis=0)
    half = seq // 2
    k = lax.broadcasted_iota(jnp.int32, (half, half), 0)
    n = lax.broadcasted_iota(jnp.int32, (half, half), 1) + 1
    ang_s = (k * n % seq).astype(F32) * (2.0 * math.pi / seq)
    weight = jnp.where(n == half, 0.5, 1.0) * seq ** -0.5
    pos_cos = (jnp.cos(ang_s) * weight).astype(BF16)
    pos_sin = (jnp.sin(ang_s) * weight).astype(BF16)
    n1 = lax.broadcasted_iota(jnp.int32, (BF16_SUBLANES, half), 1) + 1
    row = lax.broadcasted_iota(jnp.int32, (BF16_SUBLANES, half), 0)
    alt = jnp.where(row == 0, jnp.where(n1 % 2 == 0, 1.0, -1.0), 0.0)
    alt = (alt * jnp.where(n1 == half, 0.5, 1.0) * seq ** -0.5).astype(BF16)
    return chan.astype(BF16), pos_cos, pos_sin, alt


def _select(rows, cols, offset):
    r = lax.broadcasted_iota(jnp.int32, (rows, cols), 0)
    c = lax.broadcasted_iota(jnp.int32, (rows, cols), 1)
    return jnp.where(c == offset - r, 1.0, 0.0).astype(BF16)


def _fold_kernel(u_ref, t_ref, ae_ref, bo_ref, a0_ref, *, seq):
    half = seq // 2
    t = t_ref[...]
    r = lax.broadcasted_iota(jnp.int32, (FOLD_BLK, 2 * FOLD_BLK), 0)
    c = lax.broadcasted_iota(jnp.int32, (FOLD_BLK, 2 * FOLD_BLK), 1)
    shift_one = jnp.where(c == r + 1, 1.0, 0.0).astype(BF16)
    reverse = _select(FOLD_BLK, FOLD_BLK, FOLD_BLK - 1)
    for i in range(half // FOLD_BLK):
        lo = i * FOLD_BLK
        fwd = jnp.dot(shift_one, u_ref[lo:lo + 2 * FOLD_BLK, :], preferred_element_type=F32).astype(BF16)
        hi = seq - lo - FOLD_BLK
        bwd = jnp.dot(reverse, u_ref[hi:hi + FOLD_BLK, :], preferred_element_type=F32).astype(BF16)
        for gi in range(N_FGROUPS):
            cols = slice(gi * FGROUP_DIM, (gi + 1) * FGROUP_DIM)
            both = jnp.concatenate([fwd[:, cols], bwd[:, cols]], axis=1)
            res = jnp.dot(both, t, preferred_element_type=F32)
            ae_ref[lo:lo + FOLD_BLK, cols] = res[:, :FGROUP_DIM].astype(ae_ref.dtype)
            bo_ref[lo:lo + FOLD_BLK, cols] = res[:, FGROUP_DIM:].astype(bo_ref.dtype)
    for gi in range(N_FGROUPS):
        cols = slice(gi * FGROUP_DIM, (gi + 1) * FGROUP_DIM)
        first = jnp.dot(u_ref[0:BF16_SUBLANES, cols], t[:FGROUP_DIM, :FGROUP_DIM], preferred_element_type=F32)
        a0_ref[:, cols] = jnp.broadcast_to(first[0:1, :], (a0_ref.shape[0], FGROUP_DIM))


def _fold_call(u, chan_tab, batch, seq):
    half = seq // 2
    out = jax.ShapeDtypeStruct((batch * half, FOURIER_WIDTH), BF16)
    return pl.pallas_call(
        functools.partial(_fold_kernel, seq=seq),
        grid=(batch,),
        in_specs=[pl.BlockSpec((seq, FOURIER_WIDTH), lambda b: (b, 0)),
                  pl.BlockSpec((2 * FGROUP_DIM, 2 * FGROUP_DIM), lambda b: (0, 0))],
        out_specs=[pl.BlockSpec((half, FOURIER_WIDTH), lambda b: (b, 0)),
                   pl.BlockSpec((half, FOURIER_WIDTH), lambda b: (b, 0)),
                   pl.BlockSpec((None, 8, FOURIER_WIDTH), lambda b: (b, 0, 0))],
        out_shape=[out, out, jax.ShapeDtypeStruct((batch, 8, FOURIER_WIDTH), F32)],
        compiler_params=_params("parallel"),
        name="dft_fold",
    )(u, chan_tab)


def _pos_dft_kernel(c_ref, s_ref, alt_ref, ae_ref, bo_ref, a0_ref, o_ref, *, seq):
    half = seq // 2
    ae = ae_ref[...]
    dc = a0_ref[0:1, :] * seq ** -0.5
    even = jnp.dot(c_ref[...], ae, preferred_element_type=F32) + dc
    odd = jnp.dot(s_ref[...], bo_ref[...], preferred_element_type=F32)
    o_ref[0:half, :] = (even - odd).astype(o_ref.dtype)
    mirror = (even + odd).astype(BF16)
    middle = jnp.dot(alt_ref[...], ae, preferred_element_type=F32)[0:1, :] + dc
    middle = jnp.broadcast_to(middle, (FOLD_BLK, FOURIER_WIDTH)).astype(BF16)
    take = _select(FOLD_BLK, 2 * FOLD_BLK, FOLD_BLK)
    nblk = half // FOLD_BLK
    for bt in range(nblk):
        upper = middle if bt == 0 else mirror[(nblk - bt) * FOLD_BLK:(nblk - bt + 1) * FOLD_BLK, :]
        lower = mirror[(nblk - 1 - bt) * FOLD_BLK:(nblk - bt) * FOLD_BLK, :]
        blk = jnp.dot(take, jnp.concatenate([lower, upper], axis=0), preferred_element_type=F32)
        o_ref[half + bt * FOLD_BLK:half + (bt + 1) * FOLD_BLK, :] = blk.astype(o_ref.dtype)


def _pos_dft_call(pos_cos, pos_sin, alt, ae, bo, a0, batch, seq):
    half = seq // 2
    const = dict(pipeline_mode=pl.Buffered(1))
    return pl.pallas_call(
        functools.partial(_pos_dft_kernel, seq=seq),
        grid=(batch,),
        in_specs=[pl.BlockSpec((half, half), lambda b: (0, 0), **const),
                  pl.BlockSpec((half, half), lambda b: (0, 0), **const),
                  pl.BlockSpec((BF16_SUBLANES, half), lambda b: (0, 0)),
                  pl.BlockSpec((half, FOURIER_WIDTH), lambda b: (b, 0)),
                  pl.BlockSpec((half, FOURIER_WIDTH), lambda b: (b, 0)),
                  pl.BlockSpec((None, 8, FOURIER_WIDTH), lambda b: (b, 0, 0))],
        out_specs=pl.BlockSpec((seq, FOURIER_WIDTH), lambda b: (b, 0)),
        out_shape=jax.ShapeDtypeStruct((batch * seq, FOURIER_WIDTH), BF16),
        compiler_params=_params("parallel"),
        name="pos_dft",
    )(pos_cos, pos_sin, alt, ae, bo, a0)


FFN_CHUNKS = (1536, 1280)


def _tail_kernel(h_ref, a_ref, f_ref, wf_ref, bf_ref, wo_ref, g1_ref, b1_ref, wgu_ref, wd_ref, g2_ref, b2_ref,
                 o_ref):
    parts = []
    for gi in range(N_FGROUPS):
        cols = slice(gi * FGROUP_DIM, (gi + 1) * FGROUP_DIM)
        parts.append(jnp.dot(f_ref[:, cols], wf_ref[gi], preferred_element_type=F32))
    y = jnp.concatenate(parts, axis=-1) + bf_ref[...]
    mix = jnp.dot(a_ref[...], wo_ref[:ATT_WIDTH, :], preferred_element_type=F32)
    mix = mix + jnp.dot(y.astype(BF16), wo_ref[ATT_WIDTH:, :], preferred_element_type=F32)
    h1 = _layer_norm(ALPHA * h_ref[...] + mix, g1_ref[...], b1_ref[...])

    hb = h1.astype(BF16)
    acc = None
    lo = 0
    for width in FFN_CHUNKS:
        gate = jnp.dot(hb, wgu_ref[:, lo:lo + width], preferred_element_type=F32)
        up = jnp.dot(hb, wgu_ref[:, D_FF + lo:D_FF + lo + width], preferred_element_type=F32)
        act = (gate * jax.nn.sigmoid(gate) * up).astype(BF16)
        part = jnp.dot(act, wd_ref[lo:lo + width, :], preferred_element_type=F32)
        acc = part if acc is None else acc + part
        lo += width
    o_ref[...] = _layer_norm(ALPHA * h1 + acc, g2_ref[...], b2_ref[...])


def _tail_call(h2, a, f, wf_bf, b_f, wo_bf, g1, b1, wgu_bf, wd_bf, g2, b2, tm=512):
    m, d = h2.shape
    assert sum(FFN_CHUNKS) == D_FF
    const = dict(pipeline_mode=pl.Buffered(1))
    row = lambda i: (i, 0)
    fixed = lambda i: (0, 0)
    return pl.pallas_call(
        _tail_kernel,
        grid=(m // tm,),
        in_specs=[pl.BlockSpec((tm, d), row),
                  pl.BlockSpec((tm, ATT_WIDTH), row),
                  pl.BlockSpec((tm, FOURIER_WIDTH), row),
                  pl.BlockSpec((N_FGROUPS, FGROUP_DIM, FGROUP_DIM), lambda i: (0, 0, 0), **const),
                  pl.BlockSpec((1, FOURIER_WIDTH), fixed),
                  pl.BlockSpec((d, d), fixed, **const),
                  pl.BlockSpec((1, d), fixed),
                  pl.BlockSpec((1, d), fixed),
                  pl.BlockSpec((d, 2 * D_FF), fixed, **const),
                  pl.BlockSpec((D_FF, d), fixed, **const),
                  pl.BlockSpec((1, d), fixed),
                  pl.BlockSpec((1, d), fixed)],
        out_specs=pl.BlockSpec((tm, d), row),
        out_shape=jax.ShapeDtypeStruct((m, d), F32),
        compiler_params=_params("parallel"),
        name="mix_ffn",
    )(h2, a, f, wf_bf, b_f.reshape(1, -1), wo_bf, g1.reshape(1, d), b1.reshape(1, d),
      wgu_bf, wd_bf, g2.reshape(1, d), b2.reshape(1, d))


def kernel(x, ln_in_g, ln_in_b, w_in, lam_params, subln_g, w_f, b_f, w_o, ln1_g, ln1_b, w_gu, w_down, ln2_g, ln2_b):
    batch, seq, d = x.shape
    assert d == D_MODEL and seq % ATT_KBLK == 0 and seq % (STEP_TILES * ATT_TQ) == 0
    chan_tab, pos_cos, pos_sin, pos_alt = _dft_tables(seq)
    h = x.reshape(batch * seq, d)
    for l in range(DEPTH):
        if l == 0:
            h, q, kp, vt, nrm, u = _proj_call(h, w_in[l].astype(BF16), batch, seq, ln=(ln_in_g, ln_in_b))
        else:
            q, kp, vt, nrm, u = _proj_call(h, w_in[l].astype(BF16), batch, seq)
        a = _attn_call(q, kp, vt, nrm, lam_params[l], subln_g[l], batch, seq, _lambda_init(l))
        fe, fo, f0 = _fold_call(u, chan_tab, batch, seq)
        f = _pos_dft_call(pos_cos, pos_sin, pos_alt, fe, fo, f0, batch, seq)
        h = _tail_call(h, a, f, w_f[l].astype(BF16), b_f[l], w_o[l].astype(BF16), ln1_g[l], ln1_b[l],
                       w_gu[l].astype(BF16), w_down[l].astype(BF16), ln2_g[l], ln2_b[l])
    return h.reshape(batch, seq, d)
```

```python
import functools
import math

import jax
import jax.numpy as jnp
from jax import lax
from jax.experimental import pallas as pl
from jax.experimental.pallas import tpu as pltpu

D_MODEL = 1024
DEPTH = 2
ATT_WIDTH = 512
FOURIER_WIDTH = 512
N_HEADS = 8
HEAD_DIM = 32
V_HEAD_DIM = 64
N_FGROUPS = 4
FGROUP_DIM = 128
IN_COLS = 2048
D_FF = 2816
LN_EPS = 1e-5
SUBLN_EPS = 1e-5
ALPHA = (2.0 * DEPTH) ** 0.25
LOG2E = math.log2(math.e)

LANES = 128
BF16_SUBLANES = 16
VMEM_LIMIT = 56 * 1024 * 1024

BF16 = jnp.bfloat16
F32 = jnp.float32


def _lambda_init(layer_idx):
    return 0.8 - 0.6 * math.exp(-0.3 * layer_idx)


def _layer_norm(x, g, b):
    mu = jnp.mean(x, axis=-1, keepdims=True)
    xc = x - mu
    var = jnp.mean(xc * xc, axis=-1, keepdims=True)
    return xc * lax.rsqrt(var + LN_EPS) * g + b


def _params(*sem):
    return pltpu.CompilerParams(dimension_semantics=sem, vmem_limit_bytes=VMEM_LIMIT)


def _split3(x):
    hi = x.astype(BF16).astype(F32)
    r = x - hi
    mid = r.astype(BF16).astype(F32)
    lo = (r - mid).astype(BF16).astype(F32)
    return hi, mid, lo


ATT_TQ = 256
HEADS_PER_STEP = LANES // V_HEAD_DIM
CHAINS_PER_STEP = 2 * HEADS_PER_STEP
VT_ROWS = V_HEAD_DIM + BF16_SUBLANES
KP_WIDTH = 2 * LANES
MASK_LANE = 6
MASK_BIG = 2.0 ** 100
NORM_ROUND_UP = 1.05
MAX_UNSHIFTED_SCORE = 50.0


def _slice_norms(x_bf):
    x32 = x_bf.astype(F32)
    seg = lax.broadcasted_iota(jnp.int32, (ATT_WIDTH, LANES), 0) // HEAD_DIM
    col = lax.broadcasted_iota(jnp.int32, (ATT_WIDTH, LANES), 1)
    ind = jnp.where(seg == col, 1.0, 0.0).astype(BF16)
    return jnp.dot((x32 * x32).astype(BF16), ind, preferred_element_type=F32) * NORM_ROUND_UP


def _row_max(x, groups=8):
    r, c = x.shape
    return jnp.max(jnp.max(x.reshape(groups, r // groups, c), axis=0), axis=0, keepdims=True)


def _proj_kernel(*refs, seq, tm, pre_ln):
    if pre_ln:
        x_ref, g_ref, b_ref, w_ref, h_ref, q_ref, kp_ref, vt_ref, nrm_ref, u_ref = refs
        h = _layer_norm(x_ref[...], g_ref[...], b_ref[...])
        h_ref[...] = h
    else:
        x_ref, w_ref, q_ref, kp_ref, vt_ref, nrm_ref, u_ref = refs
        h = x_ref[...]
    i = pl.program_id(0)
    hb = h.astype(BF16)
    qf = jnp.dot(hb, w_ref[:, 0:ATT_WIDTH], preferred_element_type=F32)
    qb = (qf * (HEAD_DIM ** -0.5 * LOG2E)).astype(BF16)
    q_ref[...] = qb

    kf = jnp.dot(hb, w_ref[:, ATT_WIDTH:2 * ATT_WIDTH], preferred_element_type=F32).astype(BF16)
    pos = (i * tm) % seq + lax.broadcasted_iota(jnp.int32, (tm, 1), 0)
    hi, mid, lo = _split3((pos - seq // 2).astype(F32) * LOG2E)
    lane = lax.broadcasted_iota(jnp.int32, (tm, LANES), 1)
    aux = jnp.where(lane < 3, 1.0,
                    jnp.where(lane == 3, hi, jnp.where(lane == 4, mid, jnp.where(lane == 5, lo, 0.0))))
    aux = aux.astype(BF16)
    for p in range(N_HEADS // HEADS_PER_STEP):
        kp_ref[:, p * KP_WIDTH:p * KP_WIDTH + LANES] = kf[:, p * LANES:(p + 1) * LANES]
        kp_ref[:, p * KP_WIDTH + LANES:(p + 1) * KP_WIDTH] = aux

    nrm_ref[0:8, :] = jnp.broadcast_to(_row_max(_slice_norms(qb)), (8, LANES))
    nrm_ref[8:16, :] = jnp.broadcast_to(_row_max(_slice_norms(kf)), (8, LANES))

    vf = jnp.dot(hb, w_ref[:, 2 * ATT_WIDTH:3 * ATT_WIDTH], preferred_element_type=F32)
    vtr = vf.T.astype(BF16)
    ones = jnp.ones((BF16_SUBLANES, tm), BF16)
    for hd in range(N_HEADS):
        vt_ref[hd * VT_ROWS:hd * VT_ROWS + V_HEAD_DIM, :] = vtr[hd * V_HEAD_DIM:(hd + 1) * V_HEAD_DIM, :]
        vt_ref[hd * VT_ROWS + V_HEAD_DIM:(hd + 1) * VT_ROWS, :] = ones

    u_ref[...] = jnp.dot(hb, w_ref[:, 3 * ATT_WIDTH:], preferred_element_type=F32).astype(BF16)


def _proj_call(x2, w_bf, batch, seq, ln=None, tm=1024):
    m, d = x2.shape
    n = w_bf.shape[1]
    tiles_per_seq = seq // tm
    kern = functools.partial(_proj_kernel, seq=seq, tm=tm, pre_ln=ln is not None)
    row = lambda i: (i, 0)
    fixed = lambda i: (0, 0)
    in_specs = [pl.BlockSpec((tm, d), row), pl.BlockSpec((d, n), fixed)]
    args = [x2, w_bf]
    out_specs = [pl.BlockSpec((tm, ATT_WIDTH), row),
                 pl.BlockSpec((tm, 4 * KP_WIDTH), row),
                 pl.BlockSpec((None, N_HEADS * VT_ROWS, tm), lambda i: (i // tiles_per_seq, 0, i % tiles_per_seq)),
                 pl.BlockSpec((None, 16, LANES), lambda i: (i, 0, 0)),
                 pl.BlockSpec((tm, FOURIER_WIDTH), row)]
    out_shape = [jax.ShapeDtypeStruct((m, ATT_WIDTH), BF16),
                 jax.ShapeDtypeStruct((m, 4 * KP_WIDTH), BF16),
                 jax.ShapeDtypeStruct((batch, N_HEADS * VT_ROWS, seq), BF16),
                 jax.ShapeDtypeStruct((m // tm, 16, LANES), F32),
                 jax.ShapeDtypeStruct((m, FOURIER_WIDTH), BF16)]
    if ln is not None:
        in_specs[1:1] = [pl.BlockSpec((1, d), fixed), pl.BlockSpec((1, d), fixed)]
        args[1:1] = [ln[0].reshape(1, d), ln[1].reshape(1, d)]
        out_specs.insert(0, pl.BlockSpec((tm, d), row))
        out_shape.insert(0, jax.ShapeDtypeStruct((m, d), F32))
    return pl.pallas_call(
        kern,
        grid=(m // tm,),
        in_specs=in_specs,
        out_specs=out_specs,
        out_shape=out_shape,
        compiler_params=_params("parallel"),
        name="proj",
    )(*args)


def _attn_kernel(lam_ref, g_ref, nrm_ref, q_ref, kp_ref, vt_ref, o_ref, ks_ref, p_ref, flag_ref, *, lam_init, seq):
    pair = pl.program_id(1)
    lane_k = lax.broadcasted_iota(jnp.int32, (ATT_TQ, LANES), 1)
    diag_cols = jnp.where(lane_k == MASK_LANE, -MASK_BIG, 0.0).astype(BF16)

    def tile_rows(tile):
        return pl.ds(pl.multiple_of(tile * ATT_TQ, ATT_TQ), ATT_TQ)

    def enter_tile(tile):
        @pl.when(tile > 0)
        def _():
            ks_ref[tile_rows(tile - 1), LANES:] = kp_ref[tile_rows(tile - 1), LANES:]

        ks_ref[tile_rows(tile), LANES:] = diag_cols

    ks_ref[:, :LANES] = kp_ref[:, :LANES]
    ks_ref[:, LANES:] = -kp_ref[:, LANES:]
    nrm = jnp.max(nrm_ref[...], axis=0)
    lane = lax.broadcasted_iota(jnp.int32, (1, LANES), 1)
    prod = jnp.where(lane // CHAINS_PER_STEP == pair, nrm[0:1, :] * nrm[8:9, :], 0.0)
    flag_ref[0] = (jnp.max(prod) <= MAX_UNSHIFTED_SCORE ** 2).astype(jnp.int32)

    lp = lam_ref[...]
    lam = (jnp.exp(jnp.sum(lp[0:1] * lp[1:2], axis=-1, keepdims=True))
           - jnp.exp(jnp.sum(lp[2:3] * lp[3:4], axis=-1, keepdims=True)) + lam_init)

    lane_q = lax.broadcasted_iota(jnp.int32, (ATT_TQ, LANES), 1)
    zeros = jnp.zeros((ATT_TQ, LANES), BF16)
    rr = lax.broadcasted_iota(jnp.int32, (ATT_TQ, ATT_TQ), 0)
    cc = lax.broadcasted_iota(jnp.int32, (ATT_TQ, ATT_TQ), 1)
    diag_dist = jnp.abs(rr - cc).astype(F32) * LOG2E
    diag_dist = jnp.concatenate([diag_dist, diag_dist], axis=1)
    nt_dims = (((1,), (1,)), ((), ()))
    g_col = g_ref[...]

    def stage_operands(tile, hh):
        q32 = q_ref[tile_rows(tile), :].astype(F32)
        pos = tile * ATT_TQ + lax.broadcasted_iota(jnp.int32, (ATT_TQ, 1), 0)
        hi, mid, lo = _split3((pos - seq // 2).astype(F32) * LOG2E)
        base = jnp.where(lane_q == 0, -hi,
                         jnp.where(lane_q == 1, -mid,
                                   jnp.where(lane_q == 2, -lo, jnp.where(lane_q < 6, 1.0, 0.0))))
        head = pair * HEADS_PER_STEP + hh
        slope = jnp.exp2(-jnp.full((1, 1), head + 1, jnp.int32).astype(F32))
        aux = jnp.where(lane_q == MASK_LANE, 1.0, base * slope).astype(BF16)
        q_main, q_diag = [], []
        for c in range(2):
            chain = hh * 2 + c
            qm = jnp.where(lane_q // HEAD_DIM == chain, q32, 0.0).astype(BF16)
            q_main.append(jnp.concatenate([qm, aux], axis=1))
            q_diag.append(jnp.concatenate([qm, zeros], axis=1))
        q_main = jnp.concatenate(q_main, axis=0)
        q_diag = jnp.concatenate(q_diag, axis=0)
        sd = lax.dot_general(kp_ref[tile_rows(tile), :], q_diag, nt_dims, preferred_element_type=F32)
        sd = sd - slope * diag_dist
        return q_main, sd

    def attend(tile, hh, shifted):
        q_main, sd = stage_operands(tile, hh)
        st = lax.dot_general(ks_ref[...], q_main, nt_dims, preferred_element_type=F32)
        if shifted:
            m = jnp.maximum(jnp.max(sd, axis=0, keepdims=True), jnp.max(st, axis=0, keepdims=True))
            st, sd = st - m, sd - m
        p_ref[hh] = jnp.exp2(st).astype(BF16)
        p_ref[hh, tile_rows(tile), :] = jnp.exp2(sd).astype(BF16)
        vt = vt_ref[hh * VT_ROWS:(hh + 1) * VT_ROWS, :]
        return jnp.dot(vt, p_ref[hh], preferred_element_type=F32)

    def tile_body(tile, shifted):
        enter_tile(tile)
        outs = []
        for hh in range(HEADS_PER_STEP):
            acc = attend(tile, hh, shifted)
            norm = []
            for c in range(2):
                part = acc[:, c * ATT_TQ:(c + 1) * ATT_TQ]
                denom = part[V_HEAD_DIM:V_HEAD_DIM + 1, :]
                norm.append(part[:V_HEAD_DIM, :] * (1.0 / denom))
            a_t = norm[0] - lam * norm[1]
            a_t = a_t * lax.rsqrt(jnp.mean(a_t * a_t, axis=0, keepdims=True) + SUBLN_EPS)
            outs.append(a_t * g_col * (1.0 - lam_init))
        o_ref[tile_rows(tile), :] = jnp.concatenate(outs, axis=0).T.astype(o_ref.dtype)

    ntiles = seq // ATT_TQ

    @pl.when(flag_ref[0] == 1)
    def _():
        def body(tile, carry):
            tile_body(tile, shifted=False)
            return carry
        lax.fori_loop(0, ntiles, body, 0)

    @pl.when(flag_ref[0] != 1)
    def _():
        def body(tile, carry):
            tile_body(tile, shifted=True)
            return carry
        lax.fori_loop(0, ntiles, body, 0)


def _attn_call(q, kp, vt, nrm, lam_params, subln_g, batch, seq, lam_init):
    n_pairs = N_HEADS // HEADS_PER_STEP
    tiles_per_seq = nrm.shape[0] // batch
    kern = functools.partial(_attn_kernel, lam_init=lam_init, seq=seq)
    pair_rows = HEADS_PER_STEP * VT_ROWS
    return pl.pallas_call(
        kern,
        grid=(batch, n_pairs),
        in_specs=[pl.BlockSpec((4, HEAD_DIM), lambda b, p: (0, 0)),
                  pl.BlockSpec((V_HEAD_DIM, 1), lambda b, p: (0, 0)),
                  pl.BlockSpec((tiles_per_seq, 16, LANES), lambda b, p: (b, 0, 0)),
                  pl.BlockSpec((seq, LANES), lambda b, p: (b, p)),
                  pl.BlockSpec((seq, KP_WIDTH), lambda b, p: (b, p)),
                  pl.BlockSpec((None, pair_rows, seq), lambda b, p: (b, p, 0))],
        out_specs=pl.BlockSpec((seq, LANES), lambda b, p: (b, p)),
        out_shape=jax.ShapeDtypeStruct((batch * seq, ATT_WIDTH), BF16),
        scratch_shapes=[pltpu.VMEM((seq, KP_WIDTH), BF16),
                        pltpu.VMEM((HEADS_PER_STEP, seq, 2 * ATT_TQ), BF16),
                        pltpu.SMEM((1,), jnp.int32)],
        compiler_params=_params("parallel", "parallel"),
        name="diff_attn",
    )(lam_params, subln_g.reshape(V_HEAD_DIM, 1), nrm, q, kp, vt)


FOLD_BLK = 256


def _dft_tables(seq):
    c = FGROUP_DIM
    ci = lax.broadcasted_iota(jnp.int32, (c, c), 0) * lax.broadcasted_iota(jnp.int32, (c, c), 1) % c
    ang_c = ci.astype(F32) * (2.0 * math.pi / c)
    cc, sc = jnp.cos(ang_c) * c ** -0.5, jnp.sin(ang_c) * c ** -0.5
    chan = jnp.concatenate([jnp.concatenate([cc, sc], axis=1), jnp.concatenate([cc, -sc], axis=1)], axis=0)
    half = seq // 2
    k = lax.broadcasted_iota(jnp.int32, (half, half), 0)
    n = lax.broadcasted_iota(jnp.int32, (half, half), 1) + 1
    ang_s = (k * n % seq).astype(F32) * (2.0 * math.pi / seq)
    weight = jnp.where(n == half, 0.5, 1.0) * seq ** -0.5
    pos_cos = (jnp.cos(ang_s) * weight).astype(BF16)
    pos_sin = (jnp.sin(ang_s) * weight).astype(BF16)
    n1 = lax.broadcasted_iota(jnp.int32, (BF16_SUBLANES, half), 1) + 1
    row = lax.broadcasted_iota(jnp.int32, (BF16_SUBLANES, half), 0)
    alt = jnp.where(row == 0, jnp.where(n1 % 2 == 0, 1.0, -1.0), 0.0)
    alt = (alt * jnp.where(n1 == half, 0.5, 1.0) * seq ** -0.5).astype(BF16)
    return chan.astype(BF16), pos_cos, pos_sin, alt


def _select(rows, cols, offset):
    r = lax.broadcasted_iota(jnp.int32, (rows, cols), 0)
    c = lax.broadcasted_iota(jnp.int32, (rows, cols), 1)
    return jnp.where(c == offset - r, 1.0, 0.0).astype(BF16)


def _fold_kernel(u_ref, t_ref, ae_ref, bo_ref, a0_ref, *, seq):
    half = seq // 2
    t = t_ref[...]
    r = lax.broadcasted_iota(jnp.int32, (FOLD_BLK, 2 * FOLD_BLK), 0)
    c = lax.broadcasted_iota(jnp.int32, (FOLD_BLK, 2 * FOLD_BLK), 1)
    shift_one = jnp.where(c == r + 1, 1.0, 0.0).astype(BF16)
    reverse = _select(FOLD_BLK, FOLD_BLK, FOLD_BLK - 1)
    for i in range(half // FOLD_BLK):
        lo = i * FOLD_BLK
        fwd = jnp.dot(shift_one, u_ref[lo:lo + 2 * FOLD_BLK, :], preferred_element_type=F32).astype(BF16)
        hi = seq - lo - FOLD_BLK
        bwd = jnp.dot(reverse, u_ref[hi:hi + FOLD_BLK, :], preferred_element_type=F32).astype(BF16)
        for gi in range(N_FGROUPS):
            cols = slice(gi * FGROUP_DIM, (gi + 1) * FGROUP_DIM)
            both = jnp.concatenate([fwd[:, cols], bwd[:, cols]], axis=1)
            res = jnp.dot(both, t, preferred_element_type=F32)
            ae_ref[lo:lo + FOLD_BLK, cols] = res[:, :FGROUP_DIM].astype(ae_ref.dtype)
            bo_ref[lo:lo + FOLD_BLK, cols] = res[:, FGROUP_DIM:].astype(bo_ref.dtype)
    for gi in range(N_FGROUPS):
        cols = slice(gi * FGROUP_DIM, (gi + 1) * FGROUP_DIM)
        first = jnp.dot(u_ref[0:BF16_SUBLANES, cols], t[:FGROUP_DIM, :FGROUP_DIM], preferred_element_type=F32)
        a0_ref[:, cols] = jnp.broadcast_to(first[0:1, :], (a0_ref.shape[0], FGROUP_DIM))


def _fold_call(u, chan_tab, batch, seq):
    half = seq // 2
    out = jax.ShapeDtypeStruct((batch * half, FOURIER_WIDTH), BF16)
    return pl.pallas_call(
        functools.partial(_fold_kernel, seq=seq),
        grid=(batch,),
        in_specs=[pl.BlockSpec((seq, FOURIER_WIDTH), lambda b: (b, 0)),
                  pl.BlockSpec((2 * FGROUP_DIM, 2 * FGROUP_DIM), lambda b: (0, 0))],
        out_specs=[pl.BlockSpec((half, FOURIER_WIDTH), lambda b: (b, 0)),
                   pl.BlockSpec((half, FOURIER_WIDTH), lambda b: (b, 0)),
                   pl.BlockSpec((None, 8, FOURIER_WIDTH), lambda b: (b, 0, 0))],
        out_shape=[out, out, jax.ShapeDtypeStruct((batch, 8, FOURIER_WIDTH), F32)],
        compiler_params=_params("parallel"),
        name="dft_fold",
    )(u, chan_tab)


def _pos_dft_kernel(c_ref, s_ref, alt_ref, ae_ref, bo_ref, a0_ref, o_ref, *, seq):
    half = seq // 2
    ae = ae_ref[...]
    dc = a0_ref[0:1, :] * seq ** -0.5
    even = jnp.dot(c_ref[...], ae, preferred_element_type=F32) + dc
    odd = jnp.dot(s_ref[...], bo_ref[...], preferred_element_type=F32)
    o_ref[0:half, :] = (even - odd).astype(o_ref.dtype)
    mirror = (even + odd).astype(BF16)
    middle = jnp.dot(alt_ref[...], ae, preferred_element_type=F32)[0:1, :] + dc
    middle = jnp.broadcast_to(middle, (FOLD_BLK, FOURIER_WIDTH)).astype(BF16)
    take = _select(FOLD_BLK, 2 * FOLD_BLK, FOLD_BLK)
    nblk = half // FOLD_BLK
    for bt in range(nblk):
        upper = middle if bt == 0 else mirror[(nblk - bt) * FOLD_BLK:(nblk - bt + 1) * FOLD_BLK, :]
        lower = mirror[(nblk - 1 - bt) * FOLD_BLK:(nblk - bt) * FOLD_BLK, :]
        blk = jnp.dot(take, jnp.concatenate([lower, upper], axis=0), preferred_element_type=F32)
        o_ref[half + bt * FOLD_BLK:half + (bt + 1) * FOLD_BLK, :] = blk.astype(o_ref.dtype)


def _pos_dft_call(pos_cos, pos_sin, alt, ae, bo, a0, batch, seq):
    half = seq // 2
    const = dict(pipeline_mode=pl.Buffered(1))
    return pl.pallas_call(
        functools.partial(_pos_dft_kernel, seq=seq),
        grid=(batch,),
        in_specs=[pl.BlockSpec((half, half), lambda b: (0, 0), **const),
                  pl.BlockSpec((half, half), lambda b: (0, 0), **const),
                  pl.BlockSpec((BF16_SUBLANES, half), lambda b: (0, 0)),
                  pl.BlockSpec((half, FOURIER_WIDTH), lambda b: (b, 0)),
                  pl.BlockSpec((half, FOURIER_WIDTH), lambda b: (b, 0)),
                  pl.BlockSpec((None, 8, FOURIER_WIDTH), lambda b: (b, 0, 0))],
        out_specs=pl.BlockSpec((seq, FOURIER_WIDTH), lambda b: (b, 0)),
        out_shape=jax.ShapeDtypeStruct((batch * seq, FOURIER_WIDTH), BF16),
        compiler_params=_params("parallel"),
        name="pos_dft",
    )(pos_cos, pos_sin, alt, ae, bo, a0)


FFN_CHUNKS = (1536, 1280)


def _tail_kernel(h_ref, a_ref, f_ref, wf_ref, bf_ref, wo_ref, g1_ref, b1_ref, wgu_ref, wd_ref, g2_ref, b2_ref,
                 o_ref):
    parts = []
    for gi in range(N_FGROUPS):
        cols = slice(gi * FGROUP_DIM, (gi + 1) * FGROUP_DIM)
        parts.append(jnp.dot(f_ref[:, cols], wf_ref[gi], preferred_element_type=F32))
    y = jnp.concatenate(parts, axis=-1) + bf_ref[...]
    mix = jnp.dot(a_ref[...], wo_ref[:ATT_WIDTH, :], preferred_element_type=F32)
    mix = mix + jnp.dot(y.astype(BF16), wo_ref[ATT_WIDTH:, :], preferred_element_type=F32)
    h1 = _layer_norm(ALPHA * h_ref[...] + mix, g1_ref[...], b1_ref[...])

    hb = h1.astype(BF16)
    acc = None
    lo = 0
    for width in FFN_CHUNKS:
        gate = jnp.dot(hb, wgu_ref[:, lo:lo + width], preferred_element_type=F32)
        up = jnp.dot(hb, wgu_ref[:, D_FF + lo:D_FF + lo + width], preferred_element_type=F32)
        act = (gate * jax.nn.sigmoid(gate) * up).astype(BF16)
        part = jnp.dot(act, wd_ref[lo:lo + width, :], preferred_element_type=F32)
        acc = part if acc is None else acc + part
        lo += width
    o_ref[...] = _layer_norm(ALPHA * h1 + acc, g2_ref[...], b2_ref[...])


def _tail_call(h2, a, f, wf_bf, b_f, wo_bf, g1, b1, wgu_bf, wd_bf, g2, b2, tm=512):
    m, d = h2.shape
    assert sum(FFN_CHUNKS) == D_FF
    const = dict(pipeline_mode=pl.Buffered(1))
    row = lambda i: (i, 0)
    fixed = lambda i: (0, 0)
    return pl.pallas_call(
        _tail_kernel,
        grid=(m // tm,),
        in_specs=[pl.BlockSpec((tm, d), row),
                  pl.BlockSpec((tm, ATT_WIDTH), row),
                  pl.BlockSpec((tm, FOURIER_WIDTH), row),
                  pl.BlockSpec((N_FGROUPS, FGROUP_DIM, FGROUP_DIM), lambda i: (0, 0, 0), **const),
                  pl.BlockSpec((1, FOURIER_WIDTH), fixed),
                  pl.BlockSpec((d, d), fixed, **const),
                  pl.BlockSpec((1, d), fixed),
                  pl.BlockSpec((1, d), fixed),
                  pl.BlockSpec((d, 2 * D_FF), fixed, **const),
                  pl.BlockSpec((D_FF, d), fixed, **const),
                  pl.BlockSpec((1, d), fixed),
                  pl.BlockSpec((1, d), fixed)],
        out_specs=pl.BlockSpec((tm, d), row),
        out_shape=jax.ShapeDtypeStruct((m, d), F32),
        compiler_params=_params("parallel"),
        name="mix_ffn",
    )(h2, a, f, wf_bf, b_f.reshape(1, -1), wo_bf, g1.reshape(1, d), b1.reshape(1, d),
      wgu_bf, wd_bf, g2.reshape(1, d), b2.reshape(1, d))


def kernel(x, ln_in_g, ln_in_b, w_in, lam_params, subln_g, w_f, b_f, w_o, ln1_g, ln1_b, w_gu, w_down, ln2_g, ln2_b):
    batch, seq, d = x.shape
    assert d == D_MODEL and seq % ATT_TQ == 0
    chan_tab, pos_cos, pos_sin, pos_alt = _dft_tables(seq)
    h = x.reshape(batch * seq, d)
    for l in range(DEPTH):
        if l == 0:
            h, q, kp, vt, nrm, u = _proj_call(h, w_in[l].astype(BF16), batch, seq, ln=(ln_in_g, ln_in_b))
        else:
            q, kp, vt, nrm, u = _proj_call(h, w_in[l].astype(BF16), batch, seq)
        a = _attn_call(q, kp, vt, nrm, lam_params[l], subln_g[l], batch, seq, _lambda_init(l))
        fe, fo, f0 = _fold_call(u, chan_tab, batch, seq)
        f = _pos_dft_call(pos_cos, pos_sin, pos_alt, fe, fo, f0, batch, seq)
        h = _tail_call(h, a, f, w_f[l].astype(BF16), b_f[l], w_o[l].astype(BF16), ln1_g[l], ln1_b[l],
                       w_gu[l].astype(BF16), w_down[l].astype(BF16), ln2_g[l], ln2_b[l])
    return h.reshape(batch, seq, d)
```

```python
import functools
import math

import jax
import jax.numpy as jnp
from jax import lax
from jax.experimental import pallas as pl
from jax.experimental.pallas import tpu as pltpu

D_MODEL = 1024
DEPTH = 2
ATT_WIDTH = 512
FOURIER_WIDTH = 512
N_HEADS = 8
HEAD_DIM = 32
V_HEAD_DIM = 64
N_FGROUPS = 4
FGROUP_DIM = 128
IN_COLS = 2048
D_FF = 2816
LN_EPS = 1e-5
SUBLN_EPS = 1e-5
ALPHA = (2.0 * DEPTH) ** 0.25
LOG2E = math.log2(math.e)

LANES = 128
BF16_SUBLANES = 16
VMEM_LIMIT = 56 * 1024 * 1024

BF16 = jnp.bfloat16
F32 = jnp.float32


def _lambda_init(layer_idx):
    return 0.8 - 0.6 * math.exp(-0.3 * layer_idx)


def _layer_norm(x, g, b):
    mu = jnp.mean(x, axis=-1, keepdims=True)
    xc = x - mu
    var = jnp.mean(xc * xc, axis=-1, keepdims=True)
    return xc * lax.rsqrt(var + LN_EPS) * g + b


def _params(*sem):
    return pltpu.CompilerParams(dimension_semantics=sem, vmem_limit_bytes=VMEM_LIMIT)


def _split3(x):
    hi = x.astype(BF16).astype(F32)
    r = x - hi
    mid = r.astype(BF16).astype(F32)
    lo = (r - mid).astype(BF16).astype(F32)
    return hi, mid, lo


ATT_TQ = 256
HEADS_PER_STEP = LANES // V_HEAD_DIM
CHAINS_PER_STEP = 2 * HEADS_PER_STEP
VT_ROWS = V_HEAD_DIM + BF16_SUBLANES
KP_WIDTH = 2 * LANES
MASK_LANE = 6
MASK_BIG = 2.0 ** 100
NORM_ROUND_UP = 1.05
MAX_UNSHIFTED_SCORE = 50.0


def _slice_norms(x_bf):
    x32 = x_bf.astype(F32)
    seg = lax.broadcasted_iota(jnp.int32, (ATT_WIDTH, LANES), 0) // HEAD_DIM
    col = lax.broadcasted_iota(jnp.int32, (ATT_WIDTH, LANES), 1)
    ind = jnp.where(seg == col, 1.0, 0.0).astype(BF16)
    return jnp.dot((x32 * x32).astype(BF16), ind, preferred_element_type=F32) * NORM_ROUND_UP


def _row_max(x, groups=8):
    r, c = x.shape
    return jnp.max(jnp.max(x.reshape(groups, r // groups, c), axis=0), axis=0, keepdims=True)


def _proj_kernel(*refs, seq, tm, pre_ln):
    if pre_ln:
        x_ref, g_ref, b_ref, w_ref, h_ref, q_ref, kp_ref, vt_ref, vtc_ref, nrm_ref, u_ref = refs
        h = _layer_norm(x_ref[...], g_ref[...], b_ref[...])
        h_ref[...] = h
    else:
        x_ref, w_ref, q_ref, kp_ref, vt_ref, vtc_ref, nrm_ref, u_ref = refs
        h = x_ref[...]
    i = pl.program_id(0)
    hb = h.astype(BF16)
    qf = jnp.dot(hb, w_ref[:, 0:ATT_WIDTH], preferred_element_type=F32)
    qb = (qf * (HEAD_DIM ** -0.5 * LOG2E)).astype(BF16)
    q_ref[...] = qb

    kf = jnp.dot(hb, w_ref[:, ATT_WIDTH:2 * ATT_WIDTH], preferred_element_type=F32).astype(BF16)
    pos = (i * tm) % seq + lax.broadcasted_iota(jnp.int32, (tm, 1), 0)
    hi, mid, lo = _split3((pos - seq // 2).astype(F32) * LOG2E)
    lane = lax.broadcasted_iota(jnp.int32, (tm, LANES), 1)
    aux = jnp.where(lane < 3, 1.0,
                    jnp.where(lane == 3, hi, jnp.where(lane == 4, mid, jnp.where(lane == 5, lo, 0.0))))
    aux = aux.astype(BF16)
    for p in range(N_HEADS // HEADS_PER_STEP):
        kp_ref[:, p * KP_WIDTH:p * KP_WIDTH + LANES] = kf[:, p * LANES:(p + 1) * LANES]
        kp_ref[:, p * KP_WIDTH + LANES:(p + 1) * KP_WIDTH] = aux

    nrm_ref[0:8, :] = jnp.broadcast_to(_row_max(_slice_norms(qb)), (8, LANES))
    nrm_ref[8:16, :] = jnp.broadcast_to(_row_max(_slice_norms(kf)), (8, LANES))

    vf = jnp.dot(hb, w_ref[:, 2 * ATT_WIDTH:3 * ATT_WIDTH], preferred_element_type=F32)
    vtr = vf.T.astype(BF16)
    ones = jnp.ones((BF16_SUBLANES, tm), BF16)
    for hd in range(N_HEADS):
        v_rows = slice(hd * VT_ROWS, hd * VT_ROWS + V_HEAD_DIM)
        one_rows = slice(hd * VT_ROWS + V_HEAD_DIM, (hd + 1) * VT_ROWS)
        v_head = vtr[hd * V_HEAD_DIM:(hd + 1) * V_HEAD_DIM, :]
        vt_ref[v_rows, :] = v_head
        vt_ref[one_rows, :] = ones
        for cc in range(tm // ATT_TQ):
            vtc_ref[cc, v_rows, :] = v_head[:, cc * ATT_TQ:(cc + 1) * ATT_TQ]
            vtc_ref[cc, one_rows, :] = ones[:, :ATT_TQ]

    u_ref[...] = jnp.dot(hb, w_ref[:, 3 * ATT_WIDTH:], preferred_element_type=F32).astype(BF16)


def _proj_call(x2, w_bf, batch, seq, ln=None, tm=1024):
    m, d = x2.shape
    n = w_bf.shape[1]
    tiles_per_seq = seq // tm
    kern = functools.partial(_proj_kernel, seq=seq, tm=tm, pre_ln=ln is not None)
    row = lambda i: (i, 0)
    fixed = lambda i: (0, 0)
    in_specs = [pl.BlockSpec((tm, d), row), pl.BlockSpec((d, n), fixed)]
    args = [x2, w_bf]
    out_specs = [pl.BlockSpec((tm, ATT_WIDTH), row),
                 pl.BlockSpec((tm, 4 * KP_WIDTH), row),
                 pl.BlockSpec((None, N_HEADS * VT_ROWS, tm), lambda i: (i // tiles_per_seq, 0, i % tiles_per_seq)),
                 pl.BlockSpec((None, tm // ATT_TQ, N_HEADS * VT_ROWS, ATT_TQ),
                              lambda i: (i // tiles_per_seq, i % tiles_per_seq, 0, 0)),
                 pl.BlockSpec((None, 16, LANES), lambda i: (i, 0, 0)),
                 pl.BlockSpec((tm, FOURIER_WIDTH), row)]
    out_shape = [jax.ShapeDtypeStruct((m, ATT_WIDTH), BF16),
                 jax.ShapeDtypeStruct((m, 4 * KP_WIDTH), BF16),
                 jax.ShapeDtypeStruct((batch, N_HEADS * VT_ROWS, seq), BF16),
                 jax.ShapeDtypeStruct((batch, seq // ATT_TQ, N_HEADS * VT_ROWS, ATT_TQ), BF16),
                 jax.ShapeDtypeStruct((m // tm, 16, LANES), F32),
                 jax.ShapeDtypeStruct((m, FOURIER_WIDTH), BF16)]
    if ln is not None:
        in_specs[1:1] = [pl.BlockSpec((1, d), fixed), pl.BlockSpec((1, d), fixed)]
        args[1:1] = [ln[0].reshape(1, d), ln[1].reshape(1, d)]
        out_specs.insert(0, pl.BlockSpec((tm, d), row))
        out_shape.insert(0, jax.ShapeDtypeStruct((m, d), F32))
    return pl.pallas_call(
        kern,
        grid=(m // tm,),
        in_specs=in_specs,
        out_specs=out_specs,
        out_shape=out_shape,
        compiler_params=_params("parallel"),
        name="proj",
    )(*args)


def _attn_kernel(lam_ref, g_ref, nrm_ref, q_ref, kp_ref, vt_ref, vtc_ref, o_ref, ks_ref, flag_ref, *, lam_init, seq):
    pair = pl.program_id(1)
    lane_k = lax.broadcasted_iota(jnp.int32, (ATT_TQ, LANES), 1)
    diag_cols = jnp.where(lane_k == MASK_LANE, -MASK_BIG, 0.0).astype(BF16)

    def tile_rows(tile):
        return pl.ds(pl.multiple_of(tile * ATT_TQ, ATT_TQ), ATT_TQ)

    def enter_tile(tile):
        prev = tile_rows(jnp.maximum(tile - 1, 0))
        ks_ref[prev, LANES:] = kp_ref[prev, LANES:]
        ks_ref[tile_rows(tile), LANES:] = diag_cols

    ks_ref[:, :LANES] = kp_ref[:, :LANES]
    ks_ref[:, LANES:] = -kp_ref[:, LANES:]
    nrm = jnp.max(nrm_ref[...], axis=0)
    lane = lax.broadcasted_iota(jnp.int32, (1, LANES), 1)
    prod = jnp.where(lane // CHAINS_PER_STEP == pair, nrm[0:1, :] * nrm[8:9, :], 0.0)
    flag_ref[0] = (jnp.max(prod) <= MAX_UNSHIFTED_SCORE ** 2).astype(jnp.int32)

    lp = lam_ref[...]
    lam = (jnp.exp(jnp.sum(lp[0:1] * lp[1:2], axis=-1, keepdims=True))
           - jnp.exp(jnp.sum(lp[2:3] * lp[3:4], axis=-1, keepdims=True)) + lam_init)

    lane_q = lax.broadcasted_iota(jnp.int32, (ATT_TQ, LANES), 1)
    zeros = jnp.zeros((ATT_TQ, LANES), BF16)
    rr = lax.broadcasted_iota(jnp.int32, (ATT_TQ, ATT_TQ), 0)
    cc = lax.broadcasted_iota(jnp.int32, (ATT_TQ, ATT_TQ), 1)
    diag_dist = jnp.abs(rr - cc).astype(F32) * LOG2E
    diag_dist = jnp.concatenate([diag_dist, diag_dist], axis=1)
    nt_dims = (((1,), (1,)), ((), ()))
    g_col = g_ref[...]

    def stage_operands(tile, hh):
        q32 = q_ref[tile_rows(tile), :].astype(F32)
        pos = tile * ATT_TQ + lax.broadcasted_iota(jnp.int32, (ATT_TQ, 1), 0)
        hi, mid, lo = _split3((pos - seq // 2).astype(F32) * LOG2E)
        base = jnp.where(lane_q == 0, -hi,
                         jnp.where(lane_q == 1, -mid,
                                   jnp.where(lane_q == 2, -lo, jnp.where(lane_q < 6, 1.0, 0.0))))
        head = pair * HEADS_PER_STEP + hh
        slope = jnp.exp2(-jnp.full((1, 1), head + 1, jnp.int32).astype(F32))
        aux = jnp.where(lane_q == MASK_LANE, 1.0, base * slope).astype(BF16)
        q_main, q_diag = [], []
        for c in range(2):
            chain = hh * 2 + c
            qm = jnp.where(lane_q // HEAD_DIM == chain, q32, 0.0).astype(BF16)
            q_main.append(jnp.concatenate([qm, aux], axis=1))
            q_diag.append(jnp.concatenate([qm, zeros], axis=1))
        q_main = jnp.concatenate(q_main, axis=0)
        q_diag = jnp.concatenate(q_diag, axis=0)
        sd = lax.dot_general(kp_ref[tile_rows(tile), :], q_diag, nt_dims, preferred_element_type=F32)
        sd = sd - slope * diag_dist
        return q_main, sd

    def attend(tile, hh, shifted):
        q_main, sd = stage_operands(tile, hh)
        st = lax.dot_general(ks_ref[...], q_main, nt_dims, preferred_element_type=F32)
        if shifted:
            m = jnp.maximum(jnp.max(sd, axis=0, keepdims=True), jnp.max(st, axis=0, keepdims=True))
            st, sd = st - m, sd - m
        head_rows = slice(hh * VT_ROWS, (hh + 1) * VT_ROWS)
        return (jnp.dot(vt_ref[head_rows, :], jnp.exp2(st).astype(BF16), preferred_element_type=F32)
                + jnp.dot(vtc_ref[tile, head_rows, :], jnp.exp2(sd).astype(BF16), preferred_element_type=F32))

    def finish(tile, accs):
        outs = []
        for acc in accs:
            norm = []
            for c in range(2):
                part = acc[:, c * ATT_TQ:(c + 1) * ATT_TQ]
                denom = part[V_HEAD_DIM:V_HEAD_DIM + 1, :]
                norm.append(part[:V_HEAD_DIM, :] * (1.0 / denom))
            a_t = norm[0] - lam * norm[1]
            a_t = a_t * lax.rsqrt(jnp.mean(a_t * a_t, axis=0, keepdims=True) + SUBLN_EPS)
            outs.append(a_t * g_col * (1.0 - lam_init))
        o_ref[tile_rows(tile), :] = jnp.concatenate(outs, axis=0).T.astype(o_ref.dtype)

    ntiles = seq // ATT_TQ

    def run(shifted):
        def body(tile, accs):
            finish(jnp.maximum(tile - 1, 0), accs)
            enter_tile(tile)
            return tuple(attend(tile, hh, shifted) for hh in range(HEADS_PER_STEP))
        dummy = tuple(jnp.ones((VT_ROWS, 2 * ATT_TQ), F32) for _ in range(HEADS_PER_STEP))
        finish(ntiles - 1, lax.fori_loop(0, ntiles, body, dummy))

    @pl.when(flag_ref[0] == 1)
    def _():
        run(shifted=False)

    @pl.when(flag_ref[0] != 1)
    def _():
        run(shifted=True)


def _attn_call(q, kp, vt, vtc, nrm, lam_params, subln_g, batch, seq, lam_init):
    n_pairs = N_HEADS // HEADS_PER_STEP
    tiles_per_seq = nrm.shape[0] // batch
    kern = functools.partial(_attn_kernel, lam_init=lam_init, seq=seq)
    pair_rows = HEADS_PER_STEP * VT_ROWS
    return pl.pallas_call(
        kern,
        grid=(batch, n_pairs),
        in_specs=[pl.BlockSpec((4, HEAD_DIM), lambda b, p: (0, 0)),
                  pl.BlockSpec((V_HEAD_DIM, 1), lambda b, p: (0, 0)),
                  pl.BlockSpec((tiles_per_seq, 16, LANES), lambda b, p: (b, 0, 0)),
                  pl.BlockSpec((seq, LANES), lambda b, p: (b, p)),
                  pl.BlockSpec((seq, KP_WIDTH), lambda b, p: (b, p)),
                  pl.BlockSpec((None, pair_rows, seq), lambda b, p: (b, p, 0)),
                  pl.BlockSpec((None, seq // ATT_TQ, pair_rows, ATT_TQ), lambda b, p: (b, 0, p, 0))],
        out_specs=pl.BlockSpec((seq, LANES), lambda b, p: (b, p)),
        out_shape=jax.ShapeDtypeStruct((batch * seq, ATT_WIDTH), BF16),
        scratch_shapes=[pltpu.VMEM((seq, KP_WIDTH), BF16), pltpu.SMEM((1,), jnp.int32)],
        compiler_params=_params("parallel", "parallel"),
        name="diff_attn",
    )(lam_params, subln_g.reshape(V_HEAD_DIM, 1), nrm, q, kp, vt, vtc)


FOLD_BLK = 256


def _dft_tables(seq):
    c = FGROUP_DIM
    ci = lax.broadcasted_iota(jnp.int32, (c, c), 0) * lax.broadcasted_iota(jnp.int32, (c, c), 1) % c
    ang_c = ci.astype(F32) * (2.0 * math.pi / c)
    cc, sc = jnp.cos(ang_c) * c ** -0.5, jnp.sin(ang_c) * c ** -0.5
    chan = jnp.concatenate([jnp.concatenate([cc, sc], axis=1), jnp.concatenate([cc, -sc], axis=1)], axis=0)
    half = seq // 2
    k = lax.broadcasted_iota(jnp.int32, (half, half), 0)
    n = lax.broadcasted_iota(jnp.int32, (half, half), 1) + 1
    ang_s = (k * n % seq).astype(F32) * (2.0 * math.pi / seq)
    weight = jnp.where(n == half, 0.5, 1.0) * seq ** -0.5
    pos_cos = (jnp.cos(ang_s) * weight).astype(BF16)
    pos_sin = (jnp.sin(ang_s) * weight).astype(BF16)
    n1 = lax.broadcasted_iota(jnp.int32, (BF16_SUBLANES, half), 1) + 1
    row = lax.broadcasted_iota(jnp.int32, (BF16_SUBLANES, half), 0)
    alt = jnp.where(row == 0, jnp.where(n1 % 2 == 0, 1.0, -1.0), 0.0)
    alt = (alt * jnp.where(n1 == half, 0.5, 1.0) * seq ** -0.5).astype(BF16)
    return chan.astype(BF16), pos_cos, pos_sin, alt


def _select(rows, cols, offset):
    r = lax.broadcasted_iota(jnp.int32, (rows, cols), 0)
    c = lax.broadcasted_iota(jnp.int32, (rows, cols), 1)
    return jnp.where(c == offset - r, 1.0, 0.0).astype(BF16)


def _fold_kernel(u_ref, t_ref, ae_ref, bo_ref, a0_ref, *, seq):
    half = seq // 2
    t = t_ref[...]
    r = lax.broadcasted_iota(jnp.int32, (FOLD_BLK, 2 * FOLD_BLK), 0)
    c = lax.broadcasted_iota(jnp.int32, (FOLD_BLK, 2 * FOLD_BLK), 1)
    shift_one = jnp.where(c == r + 1, 1.0, 0.0).astype(BF16)
    reverse = _select(FOLD_BLK, FOLD_BLK, FOLD_BLK - 1)
    for i in range(half // FOLD_BLK):
        lo = i * FOLD_BLK
        fwd = jnp.dot(shift_one, u_ref[lo:lo + 2 * FOLD_BLK, :], preferred_element_type=F32).astype(BF16)
        hi = seq - lo - FOLD_BLK
        bwd = jnp.dot(reverse, u_ref[hi:hi + FOLD_BLK, :], preferred_element_type=F32).astype(BF16)
        for gi in range(N_FGROUPS):
            cols = slice(gi * FGROUP_DIM, (gi + 1) * FGROUP_DIM)
            both = jnp.concatenate([fwd[:, cols], bwd[:, cols]], axis=1)
            res = jnp.dot(both, t, preferred_element_type=F32)
            ae_ref[lo:lo + FOLD_BLK, cols] = res[:, :FGROUP_DIM].astype(ae_ref.dtype)
            bo_ref[lo:lo + FOLD_BLK, cols] = res[:, FGROUP_DIM:].astype(bo_ref.dtype)
    for gi in range(N_FGROUPS):
        cols = slice(gi * FGROUP_DIM, (gi + 1) * FGROUP_DIM)
        first = jnp.dot(u_ref[0:BF16_SUBLANES, cols], t[:FGROUP_DIM, :FGROUP_DIM], preferred_element_type=F32)
        a0_ref[:, cols] = jnp.broadcast_to(first[0:1, :], (a0_ref.shape[0], FGROUP_DIM))


def _fold_call(u, chan_tab, batch, seq):
    half = seq // 2
    out = jax.ShapeDtypeStruct((batch * half, FOURIER_WIDTH), BF16)
    return pl.pallas_call(
        functools.partial(_fold_kernel, seq=seq),
        grid=(batch,),
        in_specs=[pl.BlockSpec((seq, FOURIER_WIDTH), lambda b: (b, 0)),
                  pl.BlockSpec((2 * FGROUP_DIM, 2 * FGROUP_DIM), lambda b: (0, 0))],
        out_specs=[pl.BlockSpec((half, FOURIER_WIDTH), lambda b: (b, 0)),
                   pl.BlockSpec((half, FOURIER_WIDTH), lambda b: (b, 0)),
                   pl.BlockSpec((None, 8, FOURIER_WIDTH), lambda b: (b, 0, 0))],
        out_shape=[out, out, jax.ShapeDtypeStruct((batch, 8, FOURIER_WIDTH), F32)],
        compiler_params=_params("parallel"),
        name="dft_fold",
    )(u, chan_tab)


def _pos_dft_kernel(c_ref, s_ref, alt_ref, ae_ref, bo_ref, a0_ref, o_ref, *, seq):
    half = seq // 2
    ae = ae_ref[...]
    dc = a0_ref[0:1, :] * seq ** -0.5
    even = jnp.dot(c_ref[...], ae, preferred_element_type=F32) + dc
    odd = jnp.dot(s_ref[...], bo_ref[...], preferred_element_type=F32)
    o_ref[0:half, :] = (even - odd).astype(o_ref.dtype)
    mirror = (even + odd).astype(BF16)
    middle = jnp.dot(alt_ref[...], ae, preferred_element_type=F32)[0:1, :] + dc
    middle = jnp.broadcast_to(middle, (FOLD_BLK, FOURIER_WIDTH)).astype(BF16)
    take = _select(FOLD_BLK, 2 * FOLD_BLK, FOLD_BLK)
    nblk = half // FOLD_BLK
    for bt in range(nblk):
        upper = middle if bt == 0 else mirror[(nblk - bt) * FOLD_BLK:(nblk - bt + 1) * FOLD_BLK, :]
        lower = mirror[(nblk - 1 - bt) * FOLD_BLK:(nblk - bt) * FOLD_BLK, :]
        blk = jnp.dot(take, jnp.concatenate([lower, upper], axis=0), preferred_element_type=F32)
        o_ref[half + bt * FOLD_BLK:half + (bt + 1) * FOLD_BLK, :] = blk.astype(o_ref.dtype)


def _pos_dft_call(pos_cos, pos_sin, alt, ae, bo, a0, batch, seq):
    half = seq // 2
    const = dict(pipeline_mode=pl.Buffered(1))
    return pl.pallas_call(
        functools.partial(_pos_dft_kernel, seq=seq),
        grid=(batch,),
        in_specs=[pl.BlockSpec((half, half), lambda b: (0, 0), **const),
                  pl.BlockSpec((half, half), lambda b: (0, 0), **const),
                  pl.BlockSpec((BF16_SUBLANES, half), lambda b: (0, 0)),
                  pl.BlockSpec((half, FOURIER_WIDTH), lambda b: (b, 0)),
                  pl.BlockSpec((half, FOURIER_WIDTH), lambda b: (b, 0)),
                  pl.BlockSpec((None, 8, FOURIER_WIDTH), lambda b: (b, 0, 0))],
        out_specs=pl.BlockSpec((seq, FOURIER_WIDTH), lambda b: (b, 0)),
        out_shape=jax.ShapeDtypeStruct((batch * seq, FOURIER_WIDTH), BF16),
        compiler_params=_params("parallel"),
        name="pos_dft",
    )(pos_cos, pos_sin, alt, ae, bo, a0)


FFN_CHUNKS = (1536, 1280)


def _tail_kernel(h_ref, a_ref, f_ref, wf_ref, bf_ref, wo_ref, g1_ref, b1_ref, wgu_ref, wd_ref, g2_ref, b2_ref,
                 o_ref):
    parts = []
    for gi in range(N_FGROUPS):
        cols = slice(gi * FGROUP_DIM, (gi + 1) * FGROUP_DIM)
        parts.append(jnp.dot(f_ref[:, cols], wf_ref[gi], preferred_element_type=F32))
    y = jnp.concatenate(parts, axis=-1) + bf_ref[...]
    mix = jnp.dot(a_ref[...], wo_ref[:ATT_WIDTH, :], preferred_element_type=F32)
    mix = mix + jnp.dot(y.astype(BF16), wo_ref[ATT_WIDTH:, :], preferred_element_type=F32)
    h1 = _layer_norm(ALPHA * h_ref[...] + mix, g1_ref[...], b1_ref[...])

    hb = h1.astype(BF16)
    acc = None
    lo = 0
    for width in FFN_CHUNKS:
        gate = jnp.dot(hb, wgu_ref[:, lo:lo + width], preferred_element_type=F32)
        up = jnp.dot(hb, wgu_ref[:, D_FF + lo:D_FF + lo + width], preferred_element_type=F32)
        act = (gate * jax.nn.sigmoid(gate) * up).astype(BF16)
        part = jnp.dot(act, wd_ref[lo:lo + width, :], preferred_element_type=F32)
        acc = part if acc is None else acc + part
        lo += width
    o_ref[...] = _layer_norm(ALPHA * h1 + acc, g2_ref[...], b2_ref[...])


def _tail_call(h2, a, f, wf_bf, b_f, wo_bf, g1, b1, wgu_bf, wd_bf, g2, b2, tm=512):
    m, d = h2.shape
    assert sum(FFN_CHUNKS) == D_FF
    const = dict(pipeline_mode=pl.Buffered(1))
    row = lambda i: (i, 0)
    fixed = lambda i: (0, 0)
    return pl.pallas_call(
        _tail_kernel,
        grid=(m // tm,),
        in_specs=[pl.BlockSpec((tm, d), row),
                  pl.BlockSpec((tm, ATT_WIDTH), row),
                  pl.BlockSpec((tm, FOURIER_WIDTH), row),
                  pl.BlockSpec((N_FGROUPS, FGROUP_DIM, FGROUP_DIM), lambda i: (0, 0, 0), **const),
                  pl.BlockSpec((1, FOURIER_WIDTH), fixed),
                  pl.BlockSpec((d, d), fixed, **const),
                  pl.BlockSpec((1, d), fixed),
                  pl.BlockSpec((1, d), fixed),
                  pl.BlockSpec((d, 2 * D_FF), fixed, **const),
                  pl.BlockSpec((D_FF, d), fixed, **const),
                  pl.BlockSpec((1, d), fixed),
                  pl.BlockSpec((1, d), fixed)],
        out_specs=pl.BlockSpec((tm, d), row),
        out_shape=jax.ShapeDtypeStruct((m, d), F32),
        compiler_params=_params("parallel"),
        name="mix_ffn",
    )(h2, a, f, wf_bf, b_f.reshape(1, -1), wo_bf, g1.reshape(1, d), b1.reshape(1, d),
      wgu_bf, wd_bf, g2.reshape(1, d), b2.reshape(1, d))


def kernel(x, ln_in_g, ln_in_b, w_in, lam_params, subln_g, w_f, b_f, w_o, ln1_g, ln1_b, w_gu, w_down, ln2_g, ln2_b):
    batch, seq, d = x.shape
    assert d == D_MODEL and seq % ATT_TQ == 0
    chan_tab, pos_cos, pos_sin, pos_alt = _dft_tables(seq)
    h = x.reshape(batch * seq, d)
    for l in range(DEPTH):
        if l == 0:
            h, q, kp, vt, vtc, nrm, u = _proj_call(h, w_in[l].astype(BF16), batch, seq, ln=(ln_in_g, ln_in_b))
        else:
            q, kp, vt, vtc, nrm, u = _proj_call(h, w_in[l].astype(BF16), batch, seq)
        a = _attn_call(q, kp, vt, vtc, nrm, lam_params[l], subln_g[l], batch, seq, _lambda_init(l))
        fe, fo, f0 = _fold_call(u, chan_tab, batch, seq)
        f = _pos_dft_call(pos_cos, pos_sin, pos_alt, fe, fo, f0, batch, seq)
        h = _tail_call(h, a, f, w_f[l].astype(BF16), b_f[l], w_o[l].astype(BF16), ln1_g[l], ln1_b[l],
                       w_gu[l].astype(BF16), w_down[l].astype(BF16), ln2_g[l], ln2_b[l])
    return h.reshape(batch, seq, d)
```

```python
import functools
import math

import jax
import jax.numpy as jnp
from jax import lax
from jax.experimental import pallas as pl
from jax.experimental.pallas import tpu as pltpu

D_MODEL = 1024
DEPTH = 2
ATT_WIDTH = 512
FOURIER_WIDTH = 512
N_HEADS = 8
HEAD_DIM = 32
V_HEAD_DIM = 64
N_FGROUPS = 4
FGROUP_DIM = 128
IN_COLS = 2048
D_FF = 2816
LN_EPS = 1e-5
SUBLN_EPS = 1e-5
ALPHA = (2.0 * DEPTH) ** 0.25
LOG2E = math.log2(math.e)

LANES = 128
BF16_SUBLANES = 16
VMEM_LIMIT = 56 * 1024 * 1024

BF16 = jnp.bfloat16
F32 = jnp.float32


def _lambda_init(layer_idx):
    return 0.8 - 0.6 * math.exp(-0.3 * layer_idx)


def _layer_norm(x, g, b):
    mu = jnp.mean(x, axis=-1, keepdims=True)
    xc = x - mu
    var = jnp.mean(xc * xc, axis=-1, keepdims=True)
    return xc * lax.rsqrt(var + LN_EPS) * g + b


def _params(*sem):
    return pltpu.CompilerParams(dimension_semantics=sem, vmem_limit_bytes=VMEM_LIMIT)


def _split3(x):
    hi = x.astype(BF16).astype(F32)
    r = x - hi
    mid = r.astype(BF16).astype(F32)
    lo = (r - mid).astype(BF16).astype(F32)
    return hi, mid, lo


ATT_TQ = 256
HEADS_PER_STEP = LANES // V_HEAD_DIM
CHAINS_PER_STEP = 2 * HEADS_PER_STEP
VT_ROWS = 128
KP_WIDTH = 2 * LANES
MASK_LANE = 6
MASK_BIG = 2.0 ** 100
NORM_ROUND_UP = 1.05
MAX_UNSHIFTED_SCORE = 50.0


def _slice_norms(x_bf):
    x32 = x_bf.astype(F32)
    seg = lax.broadcasted_iota(jnp.int32, (ATT_WIDTH, LANES), 0) // HEAD_DIM
    col = lax.broadcasted_iota(jnp.int32, (ATT_WIDTH, LANES), 1)
    ind = jnp.where(seg == col, 1.0, 0.0).astype(BF16)
    return jnp.dot((x32 * x32).astype(BF16), ind, preferred_element_type=F32) * NORM_ROUND_UP


def _row_max(x, groups=8):
    r, c = x.shape
    return jnp.max(jnp.max(x.reshape(groups, r // groups, c), axis=0), axis=0, keepdims=True)


def _proj_kernel(*refs, seq, tm, pre_ln):
    if pre_ln:
        x_ref, g_ref, b_ref, w_ref, h_ref, q_ref, kp_ref, vt_ref, vtc_ref, nrm_ref, u_ref = refs
        h = _layer_norm(x_ref[...], g_ref[...], b_ref[...])
        h_ref[...] = h
    else:
        x_ref, w_ref, q_ref, kp_ref, vt_ref, vtc_ref, nrm_ref, u_ref = refs
        h = x_ref[...]
    i = pl.program_id(0)
    hb = h.astype(BF16)
    qf = jnp.dot(hb, w_ref[:, 0:ATT_WIDTH], preferred_element_type=F32)
    qb = (qf * (HEAD_DIM ** -0.5 * LOG2E)).astype(BF16)
    q_ref[...] = qb

    kf = jnp.dot(hb, w_ref[:, ATT_WIDTH:2 * ATT_WIDTH], preferred_element_type=F32).astype(BF16)
    pos = (i * tm) % seq + lax.broadcasted_iota(jnp.int32, (tm, 1), 0)
    hi, mid, lo = _split3((pos - seq // 2).astype(F32) * LOG2E)
    lane = lax.broadcasted_iota(jnp.int32, (tm, LANES), 1)
    aux = jnp.where(lane < 3, 1.0,
                    jnp.where(lane == 3, hi, jnp.where(lane == 4, mid, jnp.where(lane == 5, lo, 0.0))))
    aux = aux.astype(BF16)
    for p in range(N_HEADS // HEADS_PER_STEP):
        kp_ref[:, p * KP_WIDTH:p * KP_WIDTH + LANES] = kf[:, p * LANES:(p + 1) * LANES]
        kp_ref[:, p * KP_WIDTH + LANES:(p + 1) * KP_WIDTH] = aux

    nrm_ref[0:8, :] = jnp.broadcast_to(_row_max(_slice_norms(qb)), (8, LANES))
    nrm_ref[8:16, :] = jnp.broadcast_to(_row_max(_slice_norms(kf)), (8, LANES))

    vf = jnp.dot(hb, w_ref[:, 2 * ATT_WIDTH:3 * ATT_WIDTH], preferred_element_type=F32)
    vtr = vf.T.astype(BF16)
    ones = jnp.ones((VT_ROWS - V_HEAD_DIM, tm), BF16)
    for hd in range(N_HEADS):
        v_rows = slice(hd * VT_ROWS, hd * VT_ROWS + V_HEAD_DIM)
        one_rows = slice(hd * VT_ROWS + V_HEAD_DIM, (hd + 1) * VT_ROWS)
        v_head = vtr[hd * V_HEAD_DIM:(hd + 1) * V_HEAD_DIM, :]
        vt_ref[v_rows, :] = v_head
        vt_ref[one_rows, :] = ones
        for cc in range(tm // ATT_TQ):
            vtc_ref[cc, v_rows, :] = v_head[:, cc * ATT_TQ:(cc + 1) * ATT_TQ]
            vtc_ref[cc, one_rows, :] = ones[:, :ATT_TQ]

    u_ref[...] = jnp.dot(hb, w_ref[:, 3 * ATT_WIDTH:], preferred_element_type=F32).astype(BF16)


def _proj_call(x2, w_bf, batch, seq, ln=None, tm=1024):
    m, d = x2.shape
    n = w_bf.shape[1]
    tiles_per_seq = seq // tm
    kern = functools.partial(_proj_kernel, seq=seq, tm=tm, pre_ln=ln is not None)
    row = lambda i: (i, 0)
    fixed = lambda i: (0, 0)
    in_specs = [pl.BlockSpec((tm, d), row), pl.BlockSpec((d, n), fixed)]
    args = [x2, w_bf]
    out_specs = [pl.BlockSpec((tm, ATT_WIDTH), row),
                 pl.BlockSpec((tm, 4 * KP_WIDTH), row),
                 pl.BlockSpec((None, N_HEADS * VT_ROWS, tm), lambda i: (i // tiles_per_seq, 0, i % tiles_per_seq)),
                 pl.BlockSpec((None, tm // ATT_TQ, N_HEADS * VT_ROWS, ATT_TQ),
                              lambda i: (i // tiles_per_seq, i % tiles_per_seq, 0, 0)),
                 pl.BlockSpec((None, 16, LANES), lambda i: (i, 0, 0)),
                 pl.BlockSpec((tm, FOURIER_WIDTH), row)]
    out_shape = [jax.ShapeDtypeStruct((m, ATT_WIDTH), BF16),
                 jax.ShapeDtypeStruct((m, 4 * KP_WIDTH), BF16),
                 jax.ShapeDtypeStruct((batch, N_HEADS * VT_ROWS, seq), BF16),
                 jax.ShapeDtypeStruct((batch, seq // ATT_TQ, N_HEADS * VT_ROWS, ATT_TQ), BF16),
                 jax.ShapeDtypeStruct((m // tm, 16, LANES), F32),
                 jax.ShapeDtypeStruct((m, FOURIER_WIDTH), BF16)]
    if ln is not None:
        in_specs[1:1] = [pl.BlockSpec((1, d), fixed), pl.BlockSpec((1, d), fixed)]
        args[1:1] = [ln[0].reshape(1, d), ln[1].reshape(1, d)]
        out_specs.insert(0, pl.BlockSpec((tm, d), row))
        out_shape.insert(0, jax.ShapeDtypeStruct((m, d), F32))
    return pl.pallas_call(
        kern,
        grid=(m // tm,),
        in_specs=in_specs,
        out_specs=out_specs,
        out_shape=out_shape,
        compiler_params=_params("parallel"),
        name="proj",
    )(*args)


def _attn_kernel(lam_ref, g_ref, nrm_ref, q_ref, kp_ref, vt_ref, vtc_ref, o_ref, ks_ref, flag_ref, *, lam_init, seq):
    pair = pl.program_id(1)
    lane_k = lax.broadcasted_iota(jnp.int32, (ATT_TQ, LANES), 1)
    diag_cols = jnp.where(lane_k == MASK_LANE, -MASK_BIG, 0.0).astype(BF16)

    def tile_rows(tile):
        return pl.ds(pl.multiple_of(tile * ATT_TQ, ATT_TQ), ATT_TQ)

    def enter_tile(tile):
        prev = tile_rows(jnp.maximum(tile - 1, 0))
        ks_ref[prev, LANES:] = kp_ref[prev, LANES:]
        ks_ref[tile_rows(tile), LANES:] = diag_cols

    ks_ref[:, :LANES] = kp_ref[:, :LANES]
    ks_ref[:, LANES:] = -kp_ref[:, LANES:]
    nrm = jnp.max(nrm_ref[...], axis=0)
    lane = lax.broadcasted_iota(jnp.int32, (1, LANES), 1)
    prod = jnp.where(lane // CHAINS_PER_STEP == pair, nrm[0:1, :] * nrm[8:9, :], 0.0)
    flag_ref[0] = (jnp.max(prod) <= MAX_UNSHIFTED_SCORE ** 2).astype(jnp.int32)

    lp = lam_ref[...]
    lam = (jnp.exp(jnp.sum(lp[0:1] * lp[1:2], axis=-1, keepdims=True))
           - jnp.exp(jnp.sum(lp[2:3] * lp[3:4], axis=-1, keepdims=True)) + lam_init)

    lane_q = lax.broadcasted_iota(jnp.int32, (ATT_TQ, LANES), 1)
    zeros = jnp.zeros((ATT_TQ, LANES), BF16)
    rr = lax.broadcasted_iota(jnp.int32, (ATT_TQ, ATT_TQ), 0)
    cc = lax.broadcasted_iota(jnp.int32, (ATT_TQ, ATT_TQ), 1)
    diag_dist = jnp.abs(rr - cc).astype(F32) * LOG2E
    diag_dist = jnp.concatenate([diag_dist, diag_dist], axis=1)
    nt_dims = (((1,), (1,)), ((), ()))
    g_col = g_ref[...]

    def stage_operands(tile, hh):
        q32 = q_ref[tile_rows(tile), :].astype(F32)
        pos = tile * ATT_TQ + lax.broadcasted_iota(jnp.int32, (ATT_TQ, 1), 0)
        hi, mid, lo = _split3((pos - seq // 2).astype(F32) * LOG2E)
        base = jnp.where(lane_q == 0, -hi,
                         jnp.where(lane_q == 1, -mid,
                                   jnp.where(lane_q == 2, -lo, jnp.where(lane_q < 6, 1.0, 0.0))))
        head = pair * HEADS_PER_STEP + hh
        slope = jnp.exp2(-jnp.full((1, 1), head + 1, jnp.int32).astype(F32))
        aux = jnp.where(lane_q == MASK_LANE, 1.0, base * slope).astype(BF16)
        q_main, q_diag = [], []
        for c in range(2):
            chain = hh * 2 + c
            qm = jnp.where(lane_q // HEAD_DIM == chain, q32, 0.0).astype(BF16)
            q_main.append(jnp.concatenate([qm, aux], axis=1))
            q_diag.append(jnp.concatenate([qm, zeros], axis=1))
        q_main = jnp.concatenate(q_main, axis=0)
        q_diag = jnp.concatenate(q_diag, axis=0)
        sd = lax.dot_general(kp_ref[tile_rows(tile), :], q_diag, nt_dims, preferred_element_type=F32)
        sd = sd - slope * diag_dist
        return q_main, sd

    def attend(tile, hh, shifted):
        q_main, sd = stage_operands(tile, hh)
        st = lax.dot_general(ks_ref[...], q_main, nt_dims, preferred_element_type=F32)
        if shifted:
            m = jnp.maximum(jnp.max(sd, axis=0, keepdims=True), jnp.max(st, axis=0, keepdims=True))
            st, sd = st - m, sd - m
        head_rows = slice(hh * VT_ROWS, (hh + 1) * VT_ROWS)
        return (jnp.dot(vt_ref[head_rows, :], jnp.exp2(st).astype(BF16), preferred_element_type=F32)
                + jnp.dot(vtc_ref[tile, head_rows, :], jnp.exp2(sd).astype(BF16), preferred_element_type=F32))

    def finish(tile, accs):
        outs = []
        for acc in accs:
            norm = []
            for c in range(2):
                part = acc[:, c * ATT_TQ:(c + 1) * ATT_TQ]
                denom = part[V_HEAD_DIM:V_HEAD_DIM + 1, :]
                norm.append(part[:V_HEAD_DIM, :] * (1.0 / denom))
            a_t = norm[0] - lam * norm[1]
            a_t = a_t * lax.rsqrt(jnp.mean(a_t * a_t, axis=0, keepdims=True) + SUBLN_EPS)
            outs.append(a_t * g_col * (1.0 - lam_init))
        o_ref[tile_rows(tile), :] = jnp.concatenate(outs, axis=0).T.astype(o_ref.dtype)

    ntiles = seq // ATT_TQ

    def run(shifted):
        def body(tile, accs):
            finish(jnp.maximum(tile - 1, 0), accs)
            enter_tile(tile)
            return tuple(attend(tile, hh, shifted) for hh in range(HEADS_PER_STEP))
        dummy = tuple(jnp.ones((VT_ROWS, 2 * ATT_TQ), F32) for _ in range(HEADS_PER_STEP))
        finish(ntiles - 1, lax.fori_loop(0, ntiles, body, dummy))

    @pl.when(flag_ref[0] == 1)
    def _():
        run(shifted=False)

    @pl.when(flag_ref[0] != 1)
    def _():
        run(shifted=True)


def _attn_call(q, kp, vt, vtc, nrm, lam_params, subln_g, batch, seq, lam_init):
    n_pairs = N_HEADS // HEADS_PER_STEP
    tiles_per_seq = nrm.shape[0] // batch
    kern = functools.partial(_attn_kernel, lam_init=lam_init, seq=seq)
    pair_rows = HEADS_PER_STEP * VT_ROWS
    return pl.pallas_call(
        kern,
        grid=(batch, n_pairs),
        in_specs=[pl.BlockSpec((4, HEAD_DIM), lambda b, p: (0, 0)),
                  pl.BlockSpec((V_HEAD_DIM, 1), lambda b, p: (0, 0)),
                  pl.BlockSpec((tiles_per_seq, 16, LANES), lambda b, p: (b, 0, 0)),
                  pl.BlockSpec((seq, LANES), lambda b, p: (b, p)),
                  pl.BlockSpec((seq, KP_WIDTH), lambda b, p: (b, p)),
                  pl.BlockSpec((None, pair_rows, seq), lambda b, p: (b, p, 0)),
                  pl.BlockSpec((None, seq // ATT_TQ, pair_rows, ATT_TQ), lambda b, p: (b, 0, p, 0))],
        out_specs=pl.BlockSpec((seq, LANES), lambda b, p: (b, p)),
        out_shape=jax.ShapeDtypeStruct((batch * seq, ATT_WIDTH), BF16),
        scratch_shapes=[pltpu.VMEM((seq, KP_WIDTH), BF16), pltpu.SMEM((1,), jnp.int32)],
        compiler_params=_params("parallel", "parallel"),
        name="diff_attn",
    )(lam_params, subln_g.reshape(V_HEAD_DIM, 1), nrm, q, kp, vt, vtc)


FOLD_BLK = 256


def _dft_tables(seq):
    c = FGROUP_DIM
    ci = lax.broadcasted_iota(jnp.int32, (c, c), 0) * lax.broadcasted_iota(jnp.int32, (c, c), 1) % c
    ang_c = ci.astype(F32) * (2.0 * math.pi / c)
    cc, sc = jnp.cos(ang_c) * c ** -0.5, jnp.sin(ang_c) * c ** -0.5
    chan = jnp.concatenate([jnp.concatenate([cc, sc], axis=1), jnp.concatenate([cc, -sc], axis=1)], axis=0)
    half = seq // 2
    k = lax.broadcasted_iota(jnp.int32, (half, half), 0)
    n = lax.broadcasted_iota(jnp.int32, (half, half), 1) + 1
    ang_s = (k * n % seq).astype(F32) * (2.0 * math.pi / seq)
    weight = jnp.where(n == half, 0.5, 1.0) * seq ** -0.5
    pos_cos = (jnp.cos(ang_s) * weight).astype(BF16)
    pos_sin = (jnp.sin(ang_s) * weight).astype(BF16)
    n1 = lax.broadcasted_iota(jnp.int32, (BF16_SUBLANES, half), 1) + 1
    row = lax.broadcasted_iota(jnp.int32, (BF16_SUBLANES, half), 0)
    alt = jnp.where(row == 0, jnp.where(n1 % 2 == 0, 1.0, -1.0), 0.0)
    alt = (alt * jnp.where(n1 == half, 0.5, 1.0) * seq ** -0.5).astype(BF16)
    return chan.astype(BF16), pos_cos, pos_sin, alt


def _select(rows, cols, offset):
    r = lax.broadcasted_iota(jnp.int32, (rows, cols), 0)
    c = lax.broadcasted_iota(jnp.int32, (rows, cols), 1)
    return jnp.where(c == offset - r, 1.0, 0.0).astype(BF16)


def _fold_kernel(u_ref, t_ref, ae_ref, bo_ref, a0_ref, *, seq):
    half = seq // 2
    t = t_ref[...]
    r = lax.broadcasted_iota(jnp.int32, (FOLD_BLK, 2 * FOLD_BLK), 0)
    c = lax.broadcasted_iota(jnp.int32, (FOLD_BLK, 2 * FOLD_BLK), 1)
    shift_one = jnp.where(c == r + 1, 1.0, 0.0).astype(BF16)
    reverse = _select(FOLD_BLK, FOLD_BLK, FOLD_BLK - 1)
    for i in range(half // FOLD_BLK):
        lo = i * FOLD_BLK
        fwd = jnp.dot(shift_one, u_ref[lo:lo + 2 * FOLD_BLK, :], preferred_element_type=F32).astype(BF16)
        hi = seq - lo - FOLD_BLK
        bwd = jnp.dot(reverse, u_ref[hi:hi + FOLD_BLK, :], preferred_element_type=F32).astype(BF16)
        for gi in range(N_FGROUPS):
            cols = slice(gi * FGROUP_DIM, (gi + 1) * FGROUP_DIM)
            both = jnp.concatenate([fwd[:, cols], bwd[:, cols]], axis=1)
            res = jnp.dot(both, t, preferred_element_type=F32)
            ae_ref[lo:lo + FOLD_BLK, cols] = res[:, :FGROUP_DIM].astype(ae_ref.dtype)
            bo_ref[lo:lo + FOLD_BLK, cols] = res[:, FGROUP_DIM:].astype(bo_ref.dtype)
    for gi in range(N_FGROUPS):
        cols = slice(gi * FGROUP_DIM, (gi + 1) * FGROUP_DIM)
        first = jnp.dot(u_ref[0:BF16_SUBLANES, cols], t[:FGROUP_DIM, :FGROUP_DIM], preferred_element_type=F32)
        a0_ref[:, cols] = jnp.broadcast_to(first[0:1, :], (a0_ref.shape[0], FGROUP_DIM))


def _fold_call(u, chan_tab, batch, seq):
    half = seq // 2
    out = jax.ShapeDtypeStruct((batch * half, FOURIER_WIDTH), BF16)
    return pl.pallas_call(
        functools.partial(_fold_kernel, seq=seq),
        grid=(batch,),
        in_specs=[pl.BlockSpec((seq, FOURIER_WIDTH), lambda b: (b, 0)),
                  pl.BlockSpec((2 * FGROUP_DIM, 2 * FGROUP_DIM), lambda b: (0, 0))],
        out_specs=[pl.BlockSpec((half, FOURIER_WIDTH), lambda b: (b, 0)),
                   pl.BlockSpec((half, FOURIER_WIDTH), lambda b: (b, 0)),
                   pl.BlockSpec((None, 8, FOURIER_WIDTH), lambda b: (b, 0, 0))],
        out_shape=[out, out, jax.ShapeDtypeStruct((batch, 8, FOURIER_WIDTH), F32)],
        compiler_params=_params("parallel"),
        name="dft_fold",
    )(u, chan_tab)


def _pos_dft_kernel(c_ref, s_ref, alt_ref, ae_ref, bo_ref, a0_ref, o_ref, *, seq):
    half = seq // 2
    ae = ae_ref[...]
    dc = a0_ref[0:1, :] * seq ** -0.5
    even = jnp.dot(c_ref[...], ae, preferred_element_type=F32) + dc
    odd = jnp.dot(s_ref[...], bo_ref[...], preferred_element_type=F32)
    o_ref[0:half, :] = (even - odd).astype(o_ref.dtype)
    mirror = (even + odd).astype(BF16)
    middle = jnp.dot(alt_ref[...], ae, preferred_element_type=F32)[0:1, :] + dc
    middle = jnp.broadcast_to(middle, (FOLD_BLK, FOURIER_WIDTH)).astype(BF16)
    take = _select(FOLD_BLK, 2 * FOLD_BLK, FOLD_BLK)
    nblk = half // FOLD_BLK
    for bt in range(nblk):
        upper = middle if bt == 0 else mirror[(nblk - bt) * FOLD_BLK:(nblk - bt + 1) * FOLD_BLK, :]
        lower = mirror[(nblk - 1 - bt) * FOLD_BLK:(nblk - bt) * FOLD_BLK, :]
        blk = jnp.dot(take, jnp.concatenate([lower, upper], axis=0), preferred_element_type=F32)
        o_ref[half + bt * FOLD_BLK:half + (bt + 1) * FOLD_BLK, :] = blk.astype(o_ref.dtype)


def _pos_dft_call(pos_cos, pos_sin, alt, ae, bo, a0, batch, seq):
    half = seq // 2
    const = dict(pipeline_mode=pl.Buffered(1))
    return pl.pallas_call(
        functools.partial(_pos_dft_kernel, seq=seq),
        grid=(batch,),
        in_specs=[pl.BlockSpec((half, half), lambda b: (0, 0), **const),
                  pl.BlockSpec((half, half), lambda b: (0, 0), **const),
                  pl.BlockSpec((BF16_SUBLANES, half), lambda b: (0, 0)),
                  pl.BlockSpec((half, FOURIER_WIDTH), lambda b: (b, 0)),
                  pl.BlockSpec((half, FOURIER_WIDTH), lambda b: (b, 0)),
                  pl.BlockSpec((None, 8, FOURIER_WIDTH), lambda b: (b, 0, 0))],
        out_specs=pl.BlockSpec((seq, FOURIER_WIDTH), lambda b: (b, 0)),
        out_shape=jax.ShapeDtypeStruct((batch * seq, FOURIER_WIDTH), BF16),
        compiler_params=_params("parallel"),
        name="pos_dft",
    )(pos_cos, pos_sin, alt, ae, bo, a0)


FFN_CHUNKS = (1536, 1280)


def _tail_kernel(h_ref, a_ref, f_ref, wf_ref, bf_ref, wo_ref, g1_ref, b1_ref, wgu_ref, wd_ref, g2_ref, b2_ref,
                 o_ref):
    parts = []
    for gi in range(N_FGROUPS):
        cols = slice(gi * FGROUP_DIM, (gi + 1) * FGROUP_DIM)
        parts.append(jnp.dot(f_ref[:, cols], wf_ref[gi], preferred_element_type=F32))
    y = jnp.concatenate(parts, axis=-1) + bf_ref[...]
    mix = jnp.dot(a_ref[...], wo_ref[:ATT_WIDTH, :], preferred_element_type=F32)
    mix = mix + jnp.dot(y.astype(BF16), wo_ref[ATT_WIDTH:, :], preferred_element_type=F32)
    h1 = _layer_norm(ALPHA * h_ref[...] + mix, g1_ref[...], b1_ref[...])

    hb = h1.astype(BF16)
    acc = None
    lo = 0
    for width in FFN_CHUNKS:
        gate = jnp.dot(hb, wgu_ref[:, lo:lo + width], preferred_element_type=F32)
        up = jnp.dot(hb, wgu_ref[:, D_FF + lo:D_FF + lo + width], preferred_element_type=F32)
        act = (gate * jax.nn.sigmoid(gate) * up).astype(BF16)
        part = jnp.dot(act, wd_ref[lo:lo + width, :], preferred_element_type=F32)
        acc = part if acc is None else acc + part
        lo += width
    o_ref[...] = _layer_norm(ALPHA * h1 + acc, g2_ref[...], b2_ref[...])


def _tail_call(h2, a, f, wf_bf, b_f, wo_bf, g1, b1, wgu_bf, wd_bf, g2, b2, tm=512):
    m, d = h2.shape
    assert sum(FFN_CHUNKS) == D_FF
    const = dict(pipeline_mode=pl.Buffered(1))
    row = lambda i: (i, 0)
    fixed = lambda i: (0, 0)
    return pl.pallas_call(
        _tail_kernel,
        grid=(m // tm,),
        in_specs=[pl.BlockSpec((tm, d), row),
                  pl.BlockSpec((tm, ATT_WIDTH), row),
                  pl.BlockSpec((tm, FOURIER_WIDTH), row),
                  pl.BlockSpec((N_FGROUPS, FGROUP_DIM, FGROUP_DIM), lambda i: (0, 0, 0), **const),
                  pl.BlockSpec((1, FOURIER_WIDTH), fixed),
                  pl.BlockSpec((d, d), fixed, **const),
                  pl.BlockSpec((1, d), fixed),
                  pl.BlockSpec((1, d), fixed),
                  pl.BlockSpec((d, 2 * D_FF), fixed, **const),
                  pl.BlockSpec((D_FF, d), fixed, **const),
                  pl.BlockSpec((1, d), fixed),
                  pl.BlockSpec((1, d), fixed)],
        out_specs=pl.BlockSpec((tm, d), row),
        out_shape=jax.ShapeDtypeStruct((m, d), F32),
        compiler_params=_params("parallel"),
        name="mix_ffn",
    )(h2, a, f, wf_bf, b_f.reshape(1, -1), wo_bf, g1.reshape(1, d), b1.reshape(1, d),
      wgu_bf, wd_bf, g2.reshape(1, d), b2.reshape(1, d))


def kernel(x, ln_in_g, ln_in_b, w_in, lam_params, subln_g, w_f, b_f, w_o, ln1_g, ln1_b, w_gu, w_down, ln2_g, ln2_b):
    batch, seq, d = x.shape
    assert d == D_MODEL and seq % ATT_TQ == 0
    chan_tab, pos_cos, pos_sin, pos_alt = _dft_tables(seq)
    h = x.reshape(batch * seq, d)
    for l in range(DEPTH):
        if l == 0:
            h, q, kp, vt, vtc, nrm, u = _proj_call(h, w_in[l].astype(BF16), batch, seq, ln=(ln_in_g, ln_in_b))
        else:
            q, kp, vt, vtc, nrm, u = _proj_call(h, w_in[l].astype(BF16), batch, seq)
        a = _attn_call(q, kp, vt, vtc, nrm, lam_params[l], subln_g[l], batch, seq, _lambda_init(l))
        fe, fo, f0 = _fold_call(u, chan_tab, batch, seq)
        f = _pos_dft_call(pos_cos, pos_sin, pos_alt, fe, fo, f0, batch, seq)
        h = _tail_call(h, a, f, w_f[l].astype(BF16), b_f[l], w_o[l].astype(BF16), ln1_g[l], ln1_b[l],
                       w_gu[l].astype(BF16), w_down[l].astype(BF16), ln2_g[l], ln2_b[l])
    return h.reshape(batch, seq, d)
```

```python
import functools
import math

import jax
import jax.numpy as jnp
from jax import lax
from jax.experimental import pallas as pl
from jax.experimental.pallas import tpu as pltpu

D_MODEL = 1024
DEPTH = 2
ATT_WIDTH = 512
FOURIER_WIDTH = 512
N_HEADS = 8
HEAD_DIM = 32
V_HEAD_DIM = 64
N_FGROUPS = 4
FGROUP_DIM = 128
IN_COLS = 2048
D_FF = 2816
LN_EPS = 1e-5
SUBLN_EPS = 1e-5
ALPHA = (2.0 * DEPTH) ** 0.25
LOG2E = math.log2(math.e)

LANES = 128
BF16_SUBLANES = 16
VMEM_LIMIT = 56 * 1024 * 1024

BF16 = jnp.bfloat16
F32 = jnp.float32


def _lambda_init(layer_idx):
    return 0.8 - 0.6 * math.exp(-0.3 * layer_idx)


def _layer_norm(x, g, b):
    mu = jnp.mean(x, axis=-1, keepdims=True)
    xc = x - mu
    var = jnp.mean(xc * xc, axis=-1, keepdims=True)
    return xc * lax.rsqrt(var + LN_EPS) * g + b


def _params(*sem):
    return pltpu.CompilerParams(dimension_semantics=sem, vmem_limit_bytes=VMEM_LIMIT)


def _split3(x):
    hi = x.astype(BF16).astype(F32)
    r = x - hi
    mid = r.astype(BF16).astype(F32)
    lo = (r - mid).astype(BF16).astype(F32)
    return hi, mid, lo


ATT_TQ = 256
HEADS_PER_STEP = LANES // V_HEAD_DIM
CHAINS_PER_STEP = 2 * HEADS_PER_STEP
VT_ROWS = 128
KP_WIDTH = 2 * LANES
MASK_LANE = 6
MASK_BIG = 2.0 ** 100
NORM_ROUND_UP = 1.05
MAX_UNSHIFTED_SCORE = 50.0


def _slice_norms(x_bf):
    x32 = x_bf.astype(F32)
    seg = lax.broadcasted_iota(jnp.int32, (ATT_WIDTH, LANES), 0) // HEAD_DIM
    col = lax.broadcasted_iota(jnp.int32, (ATT_WIDTH, LANES), 1)
    ind = jnp.where(seg == col, 1.0, 0.0).astype(BF16)
    return jnp.dot((x32 * x32).astype(BF16), ind, preferred_element_type=F32) * NORM_ROUND_UP


def _row_max(x, groups=8):
    r, c = x.shape
    return jnp.max(jnp.max(x.reshape(groups, r // groups, c), axis=0), axis=0, keepdims=True)


def _proj_kernel(*refs, seq, tm, pre_ln):
    if pre_ln:
        x_ref, g_ref, b_ref, w_ref, h_ref, q_ref, kp_ref, vt_ref, vtc_ref, nrm_ref, u_ref = refs
        h = _layer_norm(x_ref[...], g_ref[...], b_ref[...])
        h_ref[...] = h
    else:
        x_ref, w_ref, q_ref, kp_ref, vt_ref, vtc_ref, nrm_ref, u_ref = refs
        h = x_ref[...]
    i = pl.program_id(0)
    hb = h.astype(BF16)
    qf = jnp.dot(hb, w_ref[:, 0:ATT_WIDTH], preferred_element_type=F32)
    qb = (qf * (HEAD_DIM ** -0.5 * LOG2E)).astype(BF16)
    q_ref[...] = qb

    kf = jnp.dot(hb, w_ref[:, ATT_WIDTH:2 * ATT_WIDTH], preferred_element_type=F32).astype(BF16)
    pos = (i * tm) % seq + lax.broadcasted_iota(jnp.int32, (tm, 1), 0)
    hi, mid, lo = _split3((pos - seq // 2).astype(F32) * LOG2E)
    lane = lax.broadcasted_iota(jnp.int32, (tm, LANES), 1)
    aux = jnp.where(lane < 3, 1.0,
                    jnp.where(lane == 3, hi, jnp.where(lane == 4, mid, jnp.where(lane == 5, lo, 0.0))))
    aux = aux.astype(BF16)
    for p in range(N_HEADS // HEADS_PER_STEP):
        kp_ref[:, p * KP_WIDTH:p * KP_WIDTH + LANES] = kf[:, p * LANES:(p + 1) * LANES]
        kp_ref[:, p * KP_WIDTH + LANES:(p + 1) * KP_WIDTH] = aux

    nrm_ref[0:8, :] = jnp.broadcast_to(_row_max(_slice_norms(qb)), (8, LANES))
    nrm_ref[8:16, :] = jnp.broadcast_to(_row_max(_slice_norms(kf)), (8, LANES))

    vf = jnp.dot(hb, w_ref[:, 2 * ATT_WIDTH:3 * ATT_WIDTH], preferred_element_type=F32)
    vtr = vf.T.astype(BF16)
    ones = jnp.ones((VT_ROWS - V_HEAD_DIM, tm), BF16)
    for hd in range(N_HEADS):
        v_rows = slice(hd * VT_ROWS, hd * VT_ROWS + V_HEAD_DIM)
        one_rows = slice(hd * VT_ROWS + V_HEAD_DIM, (hd + 1) * VT_ROWS)
        v_head = vtr[hd * V_HEAD_DIM:(hd + 1) * V_HEAD_DIM, :]
        vt_ref[v_rows, :] = v_head
        vt_ref[one_rows, :] = ones
        for cc in range(tm // ATT_TQ):
            vtc_ref[cc, v_rows, :] = v_head[:, cc * ATT_TQ:(cc + 1) * ATT_TQ]
            vtc_ref[cc, one_rows, :] = ones[:, :ATT_TQ]

    u_ref[...] = jnp.dot(hb, w_ref[:, 3 * ATT_WIDTH:], preferred_element_type=F32).astype(BF16)


def _proj_call(x2, w_bf, layer, batch, seq, ln=None, tm=1024):
    m, d = x2.shape
    n = w_bf.shape[2]
    tiles_per_seq = seq // tm
    kern = functools.partial(_proj_kernel, seq=seq, tm=tm, pre_ln=ln is not None)
    row = lambda i: (i, 0)
    fixed = lambda i: (0, 0)
    in_specs = [pl.BlockSpec((tm, d), row), pl.BlockSpec((None, d, n), lambda i: (layer, 0, 0))]
    args = [x2, w_bf]
    out_specs = [pl.BlockSpec((tm, ATT_WIDTH), row),
                 pl.BlockSpec((tm, 4 * KP_WIDTH), row),
                 pl.BlockSpec((None, N_HEADS * VT_ROWS, tm), lambda i: (i // tiles_per_seq, 0, i % tiles_per_seq)),
                 pl.BlockSpec((None, tm // ATT_TQ, N_HEADS * VT_ROWS, ATT_TQ),
                              lambda i: (i // tiles_per_seq, i % tiles_per_seq, 0, 0)),
                 pl.BlockSpec((None, 16, LANES), lambda i: (i, 0, 0)),
                 pl.BlockSpec((tm, FOURIER_WIDTH), row)]
    out_shape = [jax.ShapeDtypeStruct((m, ATT_WIDTH), BF16),
                 jax.ShapeDtypeStruct((m, 4 * KP_WIDTH), BF16),
                 jax.ShapeDtypeStruct((batch, N_HEADS * VT_ROWS, seq), BF16),
                 jax.ShapeDtypeStruct((batch, seq // ATT_TQ, N_HEADS * VT_ROWS, ATT_TQ), BF16),
                 jax.ShapeDtypeStruct((m // tm, 16, LANES), F32),
                 jax.ShapeDtypeStruct((m, FOURIER_WIDTH), BF16)]
    if ln is not None:
        in_specs[1:1] = [pl.BlockSpec((1, d), fixed), pl.BlockSpec((1, d), fixed)]
        args[1:1] = [ln[0].reshape(1, d), ln[1].reshape(1, d)]
        out_specs.insert(0, pl.BlockSpec((tm, d), row))
        out_shape.insert(0, jax.ShapeDtypeStruct((m, d), F32))
    return pl.pallas_call(
        kern,
        grid=(m // tm,),
        in_specs=in_specs,
        out_specs=out_specs,
        out_shape=out_shape,
        compiler_params=_params("parallel"),
        name="proj",
    )(*args)


def _attn_kernel(lam_ref, g_ref, nrm_ref, q_ref, kp_ref, vt_ref, vtc_ref, o_ref, ks_ref, flag_ref, *, lam_init, seq):
    pair = pl.program_id(1)
    lane_k = lax.broadcasted_iota(jnp.int32, (ATT_TQ, LANES), 1)
    diag_cols = jnp.where(lane_k == MASK_LANE, -MASK_BIG, 0.0).astype(BF16)

    def tile_rows(tile):
        return pl.ds(pl.multiple_of(tile * ATT_TQ, ATT_TQ), ATT_TQ)

    def enter_tile(tile):
        prev = tile_rows(jnp.maximum(tile - 1, 0))
        ks_ref[prev, LANES:] = kp_ref[prev, LANES:]
        ks_ref[tile_rows(tile), LANES:] = diag_cols

    ks_ref[:, :LANES] = kp_ref[:, :LANES]
    ks_ref[:, LANES:] = -kp_ref[:, LANES:]
    nrm = jnp.max(nrm_ref[...], axis=0)
    lane = lax.broadcasted_iota(jnp.int32, (1, LANES), 1)
    prod = jnp.where(lane // CHAINS_PER_STEP == pair, nrm[0:1, :] * nrm[8:9, :], 0.0)
    flag_ref[0] = (jnp.max(prod) <= MAX_UNSHIFTED_SCORE ** 2).astype(jnp.int32)

    lp = lam_ref[...]
    lam = (jnp.exp(jnp.sum(lp[0:1] * lp[1:2], axis=-1, keepdims=True))
           - jnp.exp(jnp.sum(lp[2:3] * lp[3:4], axis=-1, keepdims=True)) + lam_init)

    lane_q = lax.broadcasted_iota(jnp.int32, (ATT_TQ, LANES), 1)
    zeros = jnp.zeros((ATT_TQ, LANES), BF16)
    rr = lax.broadcasted_iota(jnp.int32, (ATT_TQ, ATT_TQ), 0)
    cc = lax.broadcasted_iota(jnp.int32, (ATT_TQ, ATT_TQ), 1)
    diag_dist = jnp.abs(rr - cc).astype(F32) * LOG2E
    diag_dist = jnp.concatenate([diag_dist, diag_dist], axis=1)
    nt_dims = (((1,), (1,)), ((), ()))
    g_col = g_ref[...]

    def stage_operands(tile, hh):
        q32 = q_ref[tile_rows(tile), :].astype(F32)
        pos = tile * ATT_TQ + lax.broadcasted_iota(jnp.int32, (ATT_TQ, 1), 0)
        hi, mid, lo = _split3((pos - seq // 2).astype(F32) * LOG2E)
        base = jnp.where(lane_q == 0, -hi,
                         jnp.where(lane_q == 1, -mid,
                                   jnp.where(lane_q == 2, -lo, jnp.where(lane_q < 6, 1.0, 0.0))))
        head = pair * HEADS_PER_STEP + hh
        slope = jnp.exp2(-jnp.full((1, 1), head + 1, jnp.int32).astype(F32))
        aux = jnp.where(lane_q == MASK_LANE, 1.0, base * slope).astype(BF16)
        q_main, q_diag = [], []
        for c in range(2):
            chain = hh * 2 + c
            qm = jnp.where(lane_q // HEAD_DIM == chain, q32, 0.0).astype(BF16)
            q_main.append(jnp.concatenate([qm, aux], axis=1))
            q_diag.append(jnp.concatenate([qm, zeros], axis=1))
        q_main = jnp.concatenate(q_main, axis=0)
        q_diag = jnp.concatenate(q_diag, axis=0)
        sd = lax.dot_general(kp_ref[tile_rows(tile), :], q_diag, nt_dims, preferred_element_type=F32)
        sd = sd - slope * diag_dist
        return q_main, sd

    def attend(tile, hh, shifted):
        q_main, sd = stage_operands(tile, hh)
        st = lax.dot_general(ks_ref[...], q_main, nt_dims, preferred_element_type=F32)
        if shifted:
            m = jnp.maximum(jnp.max(sd, axis=0, keepdims=True), jnp.max(st, axis=0, keepdims=True))
            st, sd = st - m, sd - m
        head_rows = slice(hh * VT_ROWS, (hh + 1) * VT_ROWS)
        return (jnp.dot(vt_ref[head_rows, :], jnp.exp2(st).astype(BF16), preferred_element_type=F32)
                + jnp.dot(vtc_ref[tile, head_rows, :], jnp.exp2(sd).astype(BF16), preferred_element_type=F32))

    def finish(tile, accs):
        outs = []
        for acc in accs:
            norm = []
            for c in range(2):
                part = acc[:, c * ATT_TQ:(c + 1) * ATT_TQ]
                denom = part[V_HEAD_DIM:V_HEAD_DIM + 1, :]
                norm.append(part[:V_HEAD_DIM, :] * (1.0 / denom))
            a_t = norm[0] - lam * norm[1]
            a_t = a_t * lax.rsqrt(jnp.mean(a_t * a_t, axis=0, keepdims=True) + SUBLN_EPS)
            outs.append(a_t * g_col * (1.0 - lam_init))
        o_ref[tile_rows(tile), :] = jnp.concatenate(outs, axis=0).T.astype(o_ref.dtype)

    ntiles = seq // ATT_TQ

    def run(shifted):
        def body(tile, accs):
            finish(jnp.maximum(tile - 1, 0), accs)
            enter_tile(tile)
            return tuple(attend(tile, hh, shifted) for hh in range(HEADS_PER_STEP))
        dummy = tuple(jnp.ones((VT_ROWS, 2 * ATT_TQ), F32) for _ in range(HEADS_PER_STEP))
        finish(ntiles - 1, lax.fori_loop(0, ntiles, body, dummy, unroll=1 if shifted else 2))

    @pl.when(flag_ref[0] == 1)
    def _():
        run(shifted=False)

    @pl.when(flag_ref[0] != 1)
    def _():
        run(shifted=True)


def _attn_call(q, kp, vt, vtc, nrm, lam_params, subln_g, batch, seq, lam_init):
    n_pairs = N_HEADS // HEADS_PER_STEP
    tiles_per_seq = nrm.shape[0] // batch
    kern = functools.partial(_attn_kernel, lam_init=lam_init, seq=seq)
    pair_rows = HEADS_PER_STEP * VT_ROWS
    return pl.pallas_call(
        kern,
        grid=(batch, n_pairs),
        in_specs=[pl.BlockSpec((4, HEAD_DIM), lambda b, p: (0, 0)),
                  pl.BlockSpec((V_HEAD_DIM, 1), lambda b, p: (0, 0)),
                  pl.BlockSpec((tiles_per_seq, 16, LANES), lambda b, p: (b, 0, 0)),
                  pl.BlockSpec((seq, LANES), lambda b, p: (b, p)),
                  pl.BlockSpec((seq, KP_WIDTH), lambda b, p: (b, p)),
                  pl.BlockSpec((None, pair_rows, seq), lambda b, p: (b, p, 0)),
                  pl.BlockSpec((None, seq // ATT_TQ, pair_rows, ATT_TQ), lambda b, p: (b, 0, p, 0))],
        out_specs=pl.BlockSpec((seq, LANES), lambda b, p: (b, p)),
        out_shape=jax.ShapeDtypeStruct((batch * seq, ATT_WIDTH), BF16),
        scratch_shapes=[pltpu.VMEM((seq, KP_WIDTH), BF16), pltpu.SMEM((1,), jnp.int32)],
        compiler_params=_params("parallel", "parallel"),
        name="diff_attn",
    )(lam_params, subln_g.reshape(V_HEAD_DIM, 1), nrm, q, kp, vt, vtc)


FOLD_BLK = 256
TABLE_SPLIT = 64


def _dft_tables(seq):
    c = FGROUP_DIM
    ci = lax.broadcasted_iota(jnp.int32, (c, c), 0) * lax.broadcasted_iota(jnp.int32, (c, c), 1) % c
    ang_c = ci.astype(F32) * (2.0 * math.pi / c)
    cc, sc = jnp.cos(ang_c) * c ** -0.5, jnp.sin(ang_c) * c ** -0.5
    chan = jnp.concatenate([jnp.concatenate([cc, sc], axis=1), jnp.concatenate([cc, -sc], axis=1)], axis=0)
    half = seq // 2
    n = lax.broadcasted_iota(jnp.int32, (1, half), 1) + 1
    weight = jnp.where(n == half, 0.5, 1.0) * seq ** -0.5

    def thin(count, stride):
        kk = lax.broadcasted_iota(jnp.int32, (count, half), 0) * stride
        ang = (kk * n % seq).astype(F32) * (2.0 * math.pi / seq)
        return jnp.cos(ang), jnp.sin(ang)

    ca, sa = thin(half // TABLE_SPLIT, TABLE_SPLIT)
    cb, sb = thin(TABLE_SPLIT, 1)
    cb, sb = cb * weight, sb * weight
    pos_cos = (ca[:, None, :] * cb[None] - sa[:, None, :] * sb[None]).reshape(half, half).astype(BF16)
    pos_sin = (sa[:, None, :] * cb[None] + ca[:, None, :] * sb[None]).reshape(half, half).astype(BF16)
    n1 = lax.broadcasted_iota(jnp.int32, (BF16_SUBLANES, half), 1) + 1
    row = lax.broadcasted_iota(jnp.int32, (BF16_SUBLANES, half), 0)
    alt = jnp.where(row == 0, jnp.where(n1 % 2 == 0, 1.0, -1.0), 0.0)
    alt = (alt * jnp.where(n1 == half, 0.5, 1.0) * seq ** -0.5).astype(BF16)
    return chan.astype(BF16), pos_cos, pos_sin, alt


def _select(rows, cols, offset):
    r = lax.broadcasted_iota(jnp.int32, (rows, cols), 0)
    c = lax.broadcasted_iota(jnp.int32, (rows, cols), 1)
    return jnp.where(c == offset - r, 1.0, 0.0).astype(BF16)


def _fold_kernel(u_ref, t_ref, ae_ref, bo_ref, a0_ref, *, seq):
    half = seq // 2
    t = t_ref[...]
    r = lax.broadcasted_iota(jnp.int32, (FOLD_BLK, 2 * FOLD_BLK), 0)
    c = lax.broadcasted_iota(jnp.int32, (FOLD_BLK, 2 * FOLD_BLK), 1)
    shift_one = jnp.where(c == r + 1, 1.0, 0.0).astype(BF16)
    reverse = _select(FOLD_BLK, FOLD_BLK, FOLD_BLK - 1)
    for i in range(half // FOLD_BLK):
        lo = i * FOLD_BLK
        fwd = jnp.dot(shift_one, u_ref[lo:lo + 2 * FOLD_BLK, :], preferred_element_type=F32).astype(BF16)
        hi = seq - lo - FOLD_BLK
        bwd = jnp.dot(reverse, u_ref[hi:hi + FOLD_BLK, :], preferred_element_type=F32).astype(BF16)
        for gi in range(N_FGROUPS):
            cols = slice(gi * FGROUP_DIM, (gi + 1) * FGROUP_DIM)
            both = jnp.concatenate([fwd[:, cols], bwd[:, cols]], axis=1)
            res = jnp.dot(both, t, preferred_element_type=F32)
            ae_ref[lo:lo + FOLD_BLK, cols] = res[:, :FGROUP_DIM].astype(ae_ref.dtype)
            bo_ref[lo:lo + FOLD_BLK, cols] = res[:, FGROUP_DIM:].astype(bo_ref.dtype)
    for gi in range(N_FGROUPS):
        cols = slice(gi * FGROUP_DIM, (gi + 1) * FGROUP_DIM)
        first = jnp.dot(u_ref[0:BF16_SUBLANES, cols], t[:FGROUP_DIM, :FGROUP_DIM], preferred_element_type=F32)
        a0_ref[:, cols] = jnp.broadcast_to(first[0:1, :], (a0_ref.shape[0], FGROUP_DIM))


def _fold_call(u, chan_tab, batch, seq):
    half = seq // 2
    out = jax.ShapeDtypeStruct((batch * half, FOURIER_WIDTH), BF16)
    return pl.pallas_call(
        functools.partial(_fold_kernel, seq=seq),
        grid=(batch,),
        in_specs=[pl.BlockSpec((seq, FOURIER_WIDTH), lambda b: (b, 0)),
                  pl.BlockSpec((2 * FGROUP_DIM, 2 * FGROUP_DIM), lambda b: (0, 0))],
        out_specs=[pl.BlockSpec((half, FOURIER_WIDTH), lambda b: (b, 0)),
                   pl.BlockSpec((half, FOURIER_WIDTH), lambda b: (b, 0)),
                   pl.BlockSpec((None, 8, FOURIER_WIDTH), lambda b: (b, 0, 0))],
        out_shape=[out, out, jax.ShapeDtypeStruct((batch, 8, FOURIER_WIDTH), F32)],
        compiler_params=_params("parallel"),
        name="dft_fold",
    )(u, chan_tab)


def _pos_dft_kernel(c_ref, s_ref, alt_ref, ae_ref, bo_ref, a0_ref, o_ref, *, seq):
    half = seq // 2
    ae = ae_ref[...]
    dc = a0_ref[0:1, :] * seq ** -0.5
    even = jnp.dot(c_ref[...], ae, preferred_element_type=F32) + dc
    odd = jnp.dot(s_ref[...], bo_ref[...], preferred_element_type=F32)
    o_ref[0:half, :] = (even - odd).astype(o_ref.dtype)
    mirror = (even + odd).astype(BF16)
    middle = jnp.dot(alt_ref[...], ae, preferred_element_type=F32)[0:1, :] + dc
    middle = jnp.broadcast_to(middle, (FOLD_BLK, FOURIER_WIDTH)).astype(BF16)
    take = _select(FOLD_BLK, 2 * FOLD_BLK, FOLD_BLK)
    nblk = half // FOLD_BLK
    for bt in range(nblk):
        upper = middle if bt == 0 else mirror[(nblk - bt) * FOLD_BLK:(nblk - bt + 1) * FOLD_BLK, :]
        lower = mirror[(nblk - 1 - bt) * FOLD_BLK:(nblk - bt) * FOLD_BLK, :]
        blk = jnp.dot(take, jnp.concatenate([lower, upper], axis=0), preferred_element_type=F32)
        o_ref[half + bt * FOLD_BLK:half + (bt + 1) * FOLD_BLK, :] = blk.astype(o_ref.dtype)


def _pos_dft_call(pos_cos, pos_sin, alt, ae, bo, a0, batch, seq):
    half = seq // 2
    const = dict(pipeline_mode=pl.Buffered(1))
    return pl.pallas_call(
        functools.partial(_pos_dft_kernel, seq=seq),
        grid=(batch,),
        in_specs=[pl.BlockSpec((half, half), lambda b: (0, 0), **const),
                  pl.BlockSpec((half, half), lambda b: (0, 0), **const),
                  pl.BlockSpec((BF16_SUBLANES, half), lambda b: (0, 0)),
                  pl.BlockSpec((half, FOURIER_WIDTH), lambda b: (b, 0)),
                  pl.BlockSpec((half, FOURIER_WIDTH), lambda b: (b, 0)),
                  pl.BlockSpec((None, 8, FOURIER_WIDTH), lambda b: (b, 0, 0))],
        out_specs=pl.BlockSpec((seq, FOURIER_WIDTH), lambda b: (b, 0)),
        out_shape=jax.ShapeDtypeStruct((batch * seq, FOURIER_WIDTH), BF16),
        compiler_params=_params("parallel"),
        name="pos_dft",
    )(pos_cos, pos_sin, alt, ae, bo, a0)


FFN_CHUNKS = (1536, 1280)
TAIL_SUBTILES = 2


def _tail_kernel(h_ref, a_ref, f_ref, wf_ref, bf_ref, wo_ref, g1_ref, b1_ref, wgu_ref, wd_ref, g2_ref, b2_ref,
                 o_ref):
    sub = h_ref.shape[0] // TAIL_SUBTILES
    groups = [slice(s * sub, (s + 1) * sub) for s in range(TAIL_SUBTILES)]

    def mixed(rows):
        parts = []
        for gi in range(N_FGROUPS):
            cols = slice(gi * FGROUP_DIM, (gi + 1) * FGROUP_DIM)
            parts.append(jnp.dot(f_ref[rows, cols], wf_ref[gi], preferred_element_type=F32))
        y = jnp.concatenate(parts, axis=-1) + bf_ref[...]
        mix = jnp.dot(a_ref[rows, :], wo_ref[:ATT_WIDTH, :], preferred_element_type=F32)
        return mix + jnp.dot(y.astype(BF16), wo_ref[ATT_WIDTH:, :], preferred_element_type=F32)

    def swiglu(h1):
        hb = h1.astype(BF16)
        acc = None
        lo = 0
        for width in FFN_CHUNKS:
            gate = jnp.dot(hb, wgu_ref[:, lo:lo + width], preferred_element_type=F32)
            up = jnp.dot(hb, wgu_ref[:, D_FF + lo:D_FF + lo + width], preferred_element_type=F32)
            act = (gate * jax.nn.sigmoid(gate) * up).astype(BF16)
            part = jnp.dot(act, wd_ref[lo:lo + width, :], preferred_element_type=F32)
            acc = part if acc is None else acc + part
            lo += width
        return acc

    mixes = [mixed(rows) for rows in groups]
    h1s = [_layer_norm(ALPHA * h_ref[rows, :] + mix, g1_ref[...], b1_ref[...]) for rows, mix in zip(groups, mixes)]
    ffns = [swiglu(h1) for h1 in h1s]
    for rows, h1, ffn in zip(groups, h1s, ffns):
        o_ref[rows, :] = _layer_norm(ALPHA * h1 + ffn, g2_ref[...], b2_ref[...])


def _tail_call(h2, a, f, layer, wf_bf, b_f, wo_bf, g1, b1, wgu_bf, wd_bf, g2, b2, tm=512):
    m, d = h2.shape
    assert sum(FFN_CHUNKS) == D_FF
    const = dict(pipeline_mode=pl.Buffered(1))
    row = lambda i: (i, 0)
    fixed = lambda i: (0, 0)
    stacked = lambda i: (layer, 0, 0)
    return pl.pallas_call(
        _tail_kernel,
        grid=(m // tm,),
        in_specs=[pl.BlockSpec((tm, d), row),
                  pl.BlockSpec((tm, ATT_WIDTH), row),
                  pl.BlockSpec((tm, FOURIER_WIDTH), row),
                  pl.BlockSpec((None, N_FGROUPS, FGROUP_DIM, FGROUP_DIM), lambda i: (layer, 0, 0, 0), **const),
                  pl.BlockSpec((1, FOURIER_WIDTH), fixed),
                  pl.BlockSpec((None, d, d), stacked, **const),
                  pl.BlockSpec((1, d), fixed),
                  pl.BlockSpec((1, d), fixed),
                  pl.BlockSpec((None, d, 2 * D_FF), stacked, **const),
                  pl.BlockSpec((None, D_FF, d), stacked, **const),
                  pl.BlockSpec((1, d), fixed),
                  pl.BlockSpec((1, d), fixed)],
        out_specs=pl.BlockSpec((tm, d), row),
        out_shape=jax.ShapeDtypeStruct((m, d), F32),
        compiler_params=_params("parallel"),
        name="mix_ffn",
    )(h2, a, f, wf_bf, b_f.reshape(1, -1), wo_bf, g1.reshape(1, d), b1.reshape(1, d),
      wgu_bf, wd_bf, g2.reshape(1, d), b2.reshape(1, d))


def kernel(x, ln_in_g, ln_in_b, w_in, lam_params, subln_g, w_f, b_f, w_o, ln1_g, ln1_b, w_gu, w_down, ln2_g, ln2_b):
    batch, seq, d = x.shape
    assert d == D_MODEL and seq % ATT_TQ == 0
    chan_tab, pos_cos, pos_sin, pos_alt = _dft_tables(seq)
    w_in_bf, w_f_bf, w_o_bf = w_in.astype(BF16), w_f.astype(BF16), w_o.astype(BF16)
    w_gu_bf, w_down_bf = w_gu.astype(BF16), w_down.astype(BF16)
    h = x.reshape(batch * seq, d)
    for l in range(DEPTH):
        if l == 0:
            h, q, kp, vt, vtc, nrm, u = _proj_call(h, w_in_bf, l, batch, seq, ln=(ln_in_g, ln_in_b))
        else:
            q, kp, vt, vtc, nrm, u = _proj_call(h, w_in_bf, l, batch, seq)
        a = _attn_call(q, kp, vt, vtc, nrm, lam_params[l], subln_g[l], batch, seq, _lambda_init(l))
        fe, fo, f0 = _fold_call(u, chan_tab, batch, seq)
        f = _pos_dft_call(pos_cos, pos_sin, pos_alt, fe, fo, f0, batch, seq)
        h = _tail_call(h, a, f, l, w_f_bf, b_f[l], w_o_bf, ln1_g[l], ln1_b[l],
                       w_gu_bf, w_down_bf, ln2_g[l], ln2_b[l])
    return h.reshape(batch, seq, d)
```

```python
import functools
import math

import jax
import jax.numpy as jnp
from jax import lax
from jax.experimental import pallas as pl
from jax.experimental.pallas import tpu as pltpu

D_MODEL = 1024
DEPTH = 2
ATT_WIDTH = 512
FOURIER_WIDTH = 512
N_HEADS = 8
HEAD_DIM = 32
V_HEAD_DIM = 64
N_FGROUPS = 4
FGROUP_DIM = 128
IN_COLS = 2048
D_FF = 2816
LN_EPS = 1e-5
SUBLN_EPS = 1e-5
ALPHA = (2.0 * DEPTH) ** 0.25
LOG2E = math.log2(math.e)

LANES = 128
BF16_SUBLANES = 16
VMEM_LIMIT = 56 * 1024 * 1024

BF16 = jnp.bfloat16
F32 = jnp.float32


def _lambda_init(layer_idx):
    return 0.8 - 0.6 * math.exp(-0.3 * layer_idx)


def _layer_norm(x, g, b):
    mu = jnp.mean(x, axis=-1, keepdims=True)
    xc = x - mu
    var = jnp.mean(xc * xc, axis=-1, keepdims=True)
    return xc * lax.rsqrt(var + LN_EPS) * g + b


def _params(*sem):
    return pltpu.CompilerParams(dimension_semantics=sem, vmem_limit_bytes=VMEM_LIMIT)


def _split3(x):
    hi = x.astype(BF16).astype(F32)
    r = x - hi
    mid = r.astype(BF16).astype(F32)
    lo = (r - mid).astype(BF16).astype(F32)
    return hi, mid, lo


ATT_TQ = 256
HEADS_PER_STEP = LANES // V_HEAD_DIM
CHAINS_PER_STEP = 2 * HEADS_PER_STEP
VT_ROWS = 128
KP_WIDTH = 2 * LANES
MASK_LANE = 6
MASK_BIG = 2.0 ** 100
NORM_ROUND_UP = 1.05
MAX_UNSHIFTED_SCORE = 50.0


def _slice_norms(x_bf):
    x32 = x_bf.astype(F32)
    seg = lax.broadcasted_iota(jnp.int32, (ATT_WIDTH, LANES), 0) // HEAD_DIM
    col = lax.broadcasted_iota(jnp.int32, (ATT_WIDTH, LANES), 1)
    ind = jnp.where(seg == col, 1.0, 0.0).astype(BF16)
    return jnp.dot((x32 * x32).astype(BF16), ind, preferred_element_type=F32) * NORM_ROUND_UP


def _row_max(x, groups=8):
    r, c = x.shape
    return jnp.max(jnp.max(x.reshape(groups, r // groups, c), axis=0), axis=0, keepdims=True)


def _proj_kernel(*refs, seq, tm, pre_ln):
    if pre_ln:
        x_ref, g_ref, b_ref, w_ref, h_ref, q_ref, kp_ref, vt_ref, vtc_ref, nrm_ref, u_ref = refs
        h = _layer_norm(x_ref[...], g_ref[...], b_ref[...])
        h_ref[...] = h
    else:
        x_ref, w_ref, q_ref, kp_ref, vt_ref, vtc_ref, nrm_ref, u_ref = refs
        h = x_ref[...]
    i = pl.program_id(0)
    hb = h.astype(BF16)
    qf = jnp.dot(hb, w_ref[:, 0:ATT_WIDTH], preferred_element_type=F32)
    qb = (qf * (HEAD_DIM ** -0.5 * LOG2E)).astype(BF16)
    q_ref[...] = qb

    kf = jnp.dot(hb, w_ref[:, ATT_WIDTH:2 * ATT_WIDTH], preferred_element_type=F32).astype(BF16)
    pos = (i * tm) % seq + lax.broadcasted_iota(jnp.int32, (tm, 1), 0)
    hi, mid, lo = _split3((pos - seq // 2).astype(F32) * LOG2E)
    lane = lax.broadcasted_iota(jnp.int32, (tm, LANES), 1)
    aux = jnp.where(lane < 3, 1.0,
                    jnp.where(lane == 3, hi, jnp.where(lane == 4, mid, jnp.where(lane == 5, lo, 0.0))))
    aux = aux.astype(BF16)
    for p in range(N_HEADS // HEADS_PER_STEP):
        kp_ref[:, p * KP_WIDTH:p * KP_WIDTH + LANES] = kf[:, p * LANES:(p + 1) * LANES]
        kp_ref[:, p * KP_WIDTH + LANES:(p + 1) * KP_WIDTH] = aux

    nrm_ref[0:8, :] = jnp.broadcast_to(_row_max(_slice_norms(qb)), (8, LANES))
    nrm_ref[8:16, :] = jnp.broadcast_to(_row_max(_slice_norms(kf)), (8, LANES))

    vf = jnp.dot(hb, w_ref[:, 2 * ATT_WIDTH:3 * ATT_WIDTH], preferred_element_type=F32)
    vtr = vf.T.astype(BF16)
    ones = jnp.ones((VT_ROWS - V_HEAD_DIM, tm), BF16)
    for hd in range(N_HEADS):
        v_rows = slice(hd * VT_ROWS, hd * VT_ROWS + V_HEAD_DIM)
        one_rows = slice(hd * VT_ROWS + V_HEAD_DIM, (hd + 1) * VT_ROWS)
        v_head = vtr[hd * V_HEAD_DIM:(hd + 1) * V_HEAD_DIM, :]
        vt_ref[v_rows, :] = v_head
        vt_ref[one_rows, :] = ones
        for cc in range(tm // ATT_TQ):
            vtc_ref[cc, v_rows, :] = v_head[:, cc * ATT_TQ:(cc + 1) * ATT_TQ]
            vtc_ref[cc, one_rows, :] = ones[:, :ATT_TQ]

    u_ref[...] = jnp.dot(hb, w_ref[:, 3 * ATT_WIDTH:], preferred_element_type=F32).astype(BF16)


def _proj_call(x2, w_bf, layer, batch, seq, ln=None, tm=1024):
    m, d = x2.shape
    n = w_bf.shape[2]
    tiles_per_seq = seq // tm
    kern = functools.partial(_proj_kernel, seq=seq, tm=tm, pre_ln=ln is not None)
    row = lambda i: (i, 0)
    fixed = lambda i: (0, 0)
    in_specs = [pl.BlockSpec((tm, d), row), pl.BlockSpec((None, d, n), lambda i: (layer, 0, 0))]
    args = [x2, w_bf]
    out_specs = [pl.BlockSpec((tm, ATT_WIDTH), row),
                 pl.BlockSpec((tm, 4 * KP_WIDTH), row),
                 pl.BlockSpec((None, N_HEADS * VT_ROWS, tm), lambda i: (i // tiles_per_seq, 0, i % tiles_per_seq)),
                 pl.BlockSpec((None, tm // ATT_TQ, N_HEADS * VT_ROWS, ATT_TQ),
                              lambda i: (i // tiles_per_seq, i % tiles_per_seq, 0, 0)),
                 pl.BlockSpec((None, 16, LANES), lambda i: (i, 0, 0)),
                 pl.BlockSpec((tm, FOURIER_WIDTH), row)]
    out_shape = [jax.ShapeDtypeStruct((m, ATT_WIDTH), BF16),
                 jax.ShapeDtypeStruct((m, 4 * KP_WIDTH), BF16),
                 jax.ShapeDtypeStruct((batch, N_HEADS * VT_ROWS, seq), BF16),
                 jax.ShapeDtypeStruct((batch, seq // ATT_TQ, N_HEADS * VT_ROWS, ATT_TQ), BF16),
                 jax.ShapeDtypeStruct((m // tm, 16, LANES), F32),
                 jax.ShapeDtypeStruct((m, FOURIER_WIDTH), BF16)]
    if ln is not None:
        in_specs[1:1] = [pl.BlockSpec((1, d), fixed), pl.BlockSpec((1, d), fixed)]
        args[1:1] = [ln[0].reshape(1, d), ln[1].reshape(1, d)]
        out_specs.insert(0, pl.BlockSpec((tm, d), row))
        out_shape.insert(0, jax.ShapeDtypeStruct((m, d), F32))
    return pl.pallas_call(
        kern,
        grid=(m // tm,),
        in_specs=in_specs,
        out_specs=out_specs,
        out_shape=out_shape,
        compiler_params=_params("parallel"),
        name="proj",
    )(*args)


def _attn_kernel(lam_ref, g_ref, nrm_ref, q_ref, kp_ref, vt_ref, vtc_ref, o_ref, ks_ref, flag_ref, *, lam_init, seq):
    pair = pl.program_id(1)
    lane_k = lax.broadcasted_iota(jnp.int32, (ATT_TQ, LANES), 1)
    diag_cols = jnp.where(lane_k == MASK_LANE, -MASK_BIG, 0.0).astype(BF16)

    def tile_rows(tile):
        return pl.ds(pl.multiple_of(tile * ATT_TQ, ATT_TQ), ATT_TQ)

    def enter_tile(tile):
        prev = tile_rows(jnp.maximum(tile - 1, 0))
        ks_ref[prev, LANES:] = kp_ref[prev, LANES:]
        ks_ref[tile_rows(tile), LANES:] = diag_cols

    ks_ref[:, :LANES] = kp_ref[:, :LANES]
    ks_ref[:, LANES:] = -kp_ref[:, LANES:]
    nrm = jnp.max(nrm_ref[...], axis=0)
    lane = lax.broadcasted_iota(jnp.int32, (1, LANES), 1)
    prod = jnp.where(lane // CHAINS_PER_STEP == pair, nrm[0:1, :] * nrm[8:9, :], 0.0)
    flag_ref[0] = (jnp.max(prod) <= MAX_UNSHIFTED_SCORE ** 2).astype(jnp.int32)

    lp = lam_ref[...]
    lam = (jnp.exp(jnp.sum(lp[0:1] * lp[1:2], axis=-1, keepdims=True))
           - jnp.exp(jnp.sum(lp[2:3] * lp[3:4], axis=-1, keepdims=True)) + lam_init)

    rr = lax.broadcasted_iota(jnp.int32, (ATT_TQ, ATT_TQ), 0)
    cc = lax.broadcasted_iota(jnp.int32, (ATT_TQ, ATT_TQ), 1)
    diag_dist = jnp.abs(rr - cc).astype(F32) * LOG2E
    diag_dist = jnp.concatenate([diag_dist, diag_dist], axis=1)
    row_q = lax.broadcasted_iota(jnp.int32, (LANES, ATT_TQ), 0)
    zeros_t = jnp.zeros((LANES, 2 * ATT_TQ), BF16)
    g_col = g_ref[...]

    def tile_operands(tile):
        q_t = q_ref[tile_rows(tile), :].astype(F32).T
        pos = tile * ATT_TQ + lax.broadcasted_iota(jnp.int32, (1, ATT_TQ), 1)
        hi, mid, lo = _split3((pos - seq // 2).astype(F32) * LOG2E)
        base = jnp.where(row_q == 0, -hi,
                         jnp.where(row_q == 1, -mid,
                                   jnp.where(row_q == 2, -lo, jnp.where(row_q < 6, 1.0, 0.0))))
        return q_t, base

    def stage_operands(tile, hh, q_t, base):
        head = pair * HEADS_PER_STEP + hh
        slope = jnp.exp2(-jnp.full((1, 1), head + 1, jnp.int32).astype(F32))
        aux = jnp.where(row_q == MASK_LANE, 1.0, base * slope).astype(BF16)
        qm = jnp.concatenate([jnp.where(row_q // HEAD_DIM == hh * 2 + c, q_t, 0.0).astype(BF16) for c in range(2)],
                             axis=1)
        w_main = jnp.concatenate([qm, jnp.concatenate([aux, aux], axis=1)], axis=0)
        w_diag = jnp.concatenate([qm, zeros_t], axis=0)
        sd = jnp.dot(kp_ref[tile_rows(tile), :], w_diag, preferred_element_type=F32)
        sd = sd - slope * diag_dist
        return w_main, sd

    def attend(tile, hh, q_t, base, shifted):
        w_main, sd = stage_operands(tile, hh, q_t, base)
        st = jnp.dot(ks_ref[...], w_main, preferred_element_type=F32)
        if shifted:
            m = jnp.maximum(jnp.max(sd, axis=0, keepdims=True), jnp.max(st, axis=0, keepdims=True))
            st, sd = st - m, sd - m
        head_rows = slice(hh * VT_ROWS, (hh + 1) * VT_ROWS)
        return (jnp.dot(vt_ref[head_rows, :], jnp.exp2(st).astype(BF16), preferred_element_type=F32)
                + jnp.dot(vtc_ref[tile, head_rows, :], jnp.exp2(sd).astype(BF16), preferred_element_type=F32))

    def finish(tile, accs):
        outs = []
        for acc in accs:
            norm = []
            for c in range(2):
                part = acc[:, c * ATT_TQ:(c + 1) * ATT_TQ]
                denom = part[V_HEAD_DIM:V_HEAD_DIM + 1, :]
                norm.append(part[:V_HEAD_DIM, :] * (1.0 / denom))
            a_t = norm[0] - lam * norm[1]
            a_t = a_t * lax.rsqrt(jnp.mean(a_t * a_t, axis=0, keepdims=True) + SUBLN_EPS)
            outs.append(a_t * g_col * (1.0 - lam_init))
        o_ref[tile_rows(tile), :] = jnp.concatenate(outs, axis=0).T.astype(o_ref.dtype)

    ntiles = seq // ATT_TQ

    def run(shifted):
        def body(tile, accs):
            finish(jnp.maximum(tile - 1, 0), accs)
            enter_tile(tile)
            q_t, base = tile_operands(tile)
            return tuple(attend(tile, hh, q_t, base, shifted) for hh in range(HEADS_PER_STEP))
        dummy = tuple(jnp.ones((VT_ROWS, 2 * ATT_TQ), F32) for _ in range(HEADS_PER_STEP))
        finish(ntiles - 1, lax.fori_loop(0, ntiles, body, dummy, unroll=1 if shifted else 2))

    @pl.when(flag_ref[0] == 1)
    def _():
        run(shifted=False)

    @pl.when(flag_ref[0] != 1)
    def _():
        run(shifted=True)


def _attn_call(q, kp, vt, vtc, nrm, lam_params, subln_g, batch, seq, lam_init):
    n_pairs = N_HEADS // HEADS_PER_STEP
    tiles_per_seq = nrm.shape[0] // batch
    kern = functools.partial(_attn_kernel, lam_init=lam_init, seq=seq)
    pair_rows = HEADS_PER_STEP * VT_ROWS
    return pl.pallas_call(
        kern,
        grid=(batch, n_pairs),
        in_specs=[pl.BlockSpec((4, HEAD_DIM), lambda b, p: (0, 0)),
                  pl.BlockSpec((V_HEAD_DIM, 1), lambda b, p: (0, 0)),
                  pl.BlockSpec((tiles_per_seq, 16, LANES), lambda b, p: (b, 0, 0)),
                  pl.BlockSpec((seq, LANES), lambda b, p: (b, p)),
                  pl.BlockSpec((seq, KP_WIDTH), lambda b, p: (b, p)),
                  pl.BlockSpec((None, pair_rows, seq), lambda b, p: (b, p, 0)),
                  pl.BlockSpec((None, seq // ATT_TQ, pair_rows, ATT_TQ), lambda b, p: (b, 0, p, 0))],
        out_specs=pl.BlockSpec((seq, LANES), lambda b, p: (b, p)),
        out_shape=jax.ShapeDtypeStruct((batch * seq, ATT_WIDTH), BF16),
        scratch_shapes=[pltpu.VMEM((seq, KP_WIDTH), BF16), pltpu.SMEM((1,), jnp.int32)],
        compiler_params=_params("parallel", "parallel"),
        name="diff_attn",
    )(lam_params, subln_g.reshape(V_HEAD_DIM, 1), nrm, q, kp, vt, vtc)


FOLD_BLK = 256
TABLE_SPLIT = 64


def _dft_tables(seq):
    c = FGROUP_DIM
    ci = lax.broadcasted_iota(jnp.int32, (c, c), 0) * lax.broadcasted_iota(jnp.int32, (c, c), 1) % c
    ang_c = ci.astype(F32) * (2.0 * math.pi / c)
    cc, sc = jnp.cos(ang_c) * c ** -0.5, jnp.sin(ang_c) * c ** -0.5
    chan = jnp.concatenate([jnp.concatenate([cc, sc], axis=1), jnp.concatenate([cc, -sc], axis=1)], axis=0)
    half = seq // 2
    n = lax.broadcasted_iota(jnp.int32, (1, half), 1) + 1
    weight = jnp.where(n == half, 0.5, 1.0) * seq ** -0.5

    def thin(count, stride):
        kk = lax.broadcasted_iota(jnp.int32, (count, half), 0) * stride
        ang = (kk * n % seq).astype(F32) * (2.0 * math.pi / seq)
        return jnp.cos(ang), jnp.sin(ang)

    ca, sa = thin(half // TABLE_SPLIT, TABLE_SPLIT)
    cb, sb = thin(TABLE_SPLIT, 1)
    cb, sb = cb * weight, sb * weight
    pos_cos = (ca[:, None, :] * cb[None] - sa[:, None, :] * sb[None]).reshape(half, half).astype(BF16)
    pos_sin = (sa[:, None, :] * cb[None] + ca[:, None, :] * sb[None]).reshape(half, half).astype(BF16)
    n1 = lax.broadcasted_iota(jnp.int32, (BF16_SUBLANES, half), 1) + 1
    row = lax.broadcasted_iota(jnp.int32, (BF16_SUBLANES, half), 0)
    alt = jnp.where(row == 0, jnp.where(n1 % 2 == 0, 1.0, -1.0), 0.0)
    alt = (alt * jnp.where(n1 == half, 0.5, 1.0) * seq ** -0.5).astype(BF16)
    return chan.astype(BF16), pos_cos, pos_sin, alt


def _select(rows, cols, offset):
    r = lax.broadcasted_iota(jnp.int32, (rows, cols), 0)
    c = lax.broadcasted_iota(jnp.int32, (rows, cols), 1)
    return jnp.where(c == offset - r, 1.0, 0.0).astype(BF16)


def _fold_kernel(u_ref, t_ref, ae_ref, bo_ref, a0_ref, *, seq):
    half = seq // 2
    t = t_ref[...]
    r = lax.broadcasted_iota(jnp.int32, (FOLD_BLK, 2 * FOLD_BLK), 0)
    c = lax.broadcasted_iota(jnp.int32, (FOLD_BLK, 2 * FOLD_BLK), 1)
    shift_one = jnp.where(c == r + 1, 1.0, 0.0).astype(BF16)
    reverse = _select(FOLD_BLK, FOLD_BLK, FOLD_BLK - 1)
    for i in range(half // FOLD_BLK):
        lo = i * FOLD_BLK
        fwd = jnp.dot(shift_one, u_ref[lo:lo + 2 * FOLD_BLK, :], preferred_element_type=F32).astype(BF16)
        hi = seq - lo - FOLD_BLK
        bwd = jnp.dot(reverse, u_ref[hi:hi + FOLD_BLK, :], preferred_element_type=F32).astype(BF16)
        for gi in range(N_FGROUPS):
            cols = slice(gi * FGROUP_DIM, (gi + 1) * FGROUP_DIM)
            both = jnp.concatenate([fwd[:, cols], bwd[:, cols]], axis=1)
            res = jnp.dot(both, t, preferred_element_type=F32)
            ae_ref[lo:lo + FOLD_BLK, cols] = res[:, :FGROUP_DIM].astype(ae_ref.dtype)
            bo_ref[lo:lo + FOLD_BLK, cols] = res[:, FGROUP_DIM:].astype(bo_ref.dtype)
    for gi in range(N_FGROUPS):
        cols = slice(gi * FGROUP_DIM, (gi + 1) * FGROUP_DIM)
        first = jnp.dot(u_ref[0:BF16_SUBLANES, cols], t[:FGROUP_DIM, :FGROUP_DIM], preferred_element_type=F32)
        a0_ref[:, cols] = jnp.broadcast_to(first[0:1, :], (a0_ref.shape[0], FGROUP_DIM))


def _fold_call(u, chan_tab, batch, seq):
    half = seq // 2
    out = jax.ShapeDtypeStruct((batch * half, FOURIER_WIDTH), BF16)
    return pl.pallas_call(
        functools.partial(_fold_kernel, seq=seq),
        grid=(batch,),
        in_specs=[pl.BlockSpec((seq, FOURIER_WIDTH), lambda b: (b, 0)),
                  pl.BlockSpec((2 * FGROUP_DIM, 2 * FGROUP_DIM), lambda b: (0, 0))],
        out_specs=[pl.BlockSpec((half, FOURIER_WIDTH), lambda b: (b, 0)),
                   pl.BlockSpec((half, FOURIER_WIDTH), lambda b: (b, 0)),
                   pl.BlockSpec((None, 8, FOURIER_WIDTH), lambda b: (b, 0, 0))],
        out_shape=[out, out, jax.ShapeDtypeStruct((batch, 8, FOURIER_WIDTH), F32)],
        compiler_params=_params("parallel"),
        name="dft_fold",
    )(u, chan_tab)


def _pos_dft_kernel(c_ref, s_ref, alt_ref, ae_ref, bo_ref, a0_ref, o_ref, *, seq):
    half = seq // 2
    ae = ae_ref[...]
    dc = a0_ref[0:1, :] * seq ** -0.5
    even = jnp.dot(c_ref[...], ae, preferred_element_type=F32) + dc
    odd = jnp.dot(s_ref[...], bo_ref[...], preferred_element_type=F32)
    o_ref[0:half, :] = (even - odd).astype(o_ref.dtype)
    mirror = (even + odd).astype(BF16)
    middle = jnp.dot(alt_ref[...], ae, preferred_element_type=F32)[0:1, :] + dc
    middle = jnp.broadcast_to(middle, (FOLD_BLK, FOURIER_WIDTH)).astype(BF16)
    take = _select(FOLD_BLK, 2 * FOLD_BLK, FOLD_BLK)
    nblk = half // FOLD_BLK
    for bt in range(nblk):
        upper = middle if bt == 0 else mirror[(nblk - bt) * FOLD_BLK:(nblk - bt + 1) * FOLD_BLK, :]
        lower = mirror[(nblk - 1 - bt) * FOLD_BLK:(nblk - bt) * FOLD_BLK, :]
        blk = jnp.dot(take, jnp.concatenate([lower, upper], axis=0), preferred_element_type=F32)
        o_ref[half + bt * FOLD_BLK:half + (bt + 1) * FOLD_BLK, :] = blk.astype(o_ref.dtype)


def _pos_dft_call(pos_cos, pos_sin, alt, ae, bo, a0, batch, seq):
    half = seq // 2
    const = dict(pipeline_mode=pl.Buffered(1))
    return pl.pallas_call(
        functools.partial(_pos_dft_kernel, seq=seq),
        grid=(batch,),
        in_specs=[pl.BlockSpec((half, half), lambda b: (0, 0), **const),
                  pl.BlockSpec((half, half), lambda b: (0, 0), **const),
                  pl.BlockSpec((BF16_SUBLANES, half), lambda b: (0, 0)),
                  pl.BlockSpec((half, FOURIER_WIDTH), lambda b: (b, 0)),
                  pl.BlockSpec((half, FOURIER_WIDTH), lambda b: (b, 0)),
                  pl.BlockSpec((None, 8, FOURIER_WIDTH), lambda b: (b, 0, 0))],
        out_specs=pl.BlockSpec((seq, FOURIER_WIDTH), lambda b: (b, 0)),
        out_shape=jax.ShapeDtypeStruct((batch * seq, FOURIER_WIDTH), BF16),
        compiler_params=_params("parallel"),
        name="pos_dft",
    )(pos_cos, pos_sin, alt, ae, bo, a0)


FFN_CHUNKS = (1536, 1280)
TAIL_SUBTILES = 2


def _tail_kernel(h_ref, a_ref, f_ref, wf_ref, bf_ref, wo_ref, g1_ref, b1_ref, wgu_ref, wd_ref, g2_ref, b2_ref,
                 o_ref):
    sub = h_ref.shape[0] // TAIL_SUBTILES
    groups = [slice(s * sub, (s + 1) * sub) for s in range(TAIL_SUBTILES)]

    def mixed(rows):
        parts = []
        for gi in range(N_FGROUPS):
            cols = slice(gi * FGROUP_DIM, (gi + 1) * FGROUP_DIM)
            parts.append(jnp.dot(f_ref[rows, cols], wf_ref[gi], preferred_element_type=F32))
        y = jnp.concatenate(parts, axis=-1) + bf_ref[...]
        mix = jnp.dot(a_ref[rows, :], wo_ref[:ATT_WIDTH, :], preferred_element_type=F32)
        return mix + jnp.dot(y.astype(BF16), wo_ref[ATT_WIDTH:, :], preferred_element_type=F32)

    def swiglu(h1):
        hb = h1.astype(BF16)
        acc = None
        lo = 0
        for width in FFN_CHUNKS:
            gate = jnp.dot(hb, wgu_ref[:, lo:lo + width], preferred_element_type=F32)
            up = jnp.dot(hb, wgu_ref[:, D_FF + lo:D_FF + lo + width], preferred_element_type=F32)
            act = (gate * jax.nn.sigmoid(gate) * up).astype(BF16)
            part = jnp.dot(act, wd_ref[lo:lo + width, :], preferred_element_type=F32)
            acc = part if acc is None else acc + part
            lo += width
        return acc

    mixes = [mixed(rows) for rows in groups]
    h1s = [_layer_norm(ALPHA * h_ref[rows, :] + mix, g1_ref[...], b1_ref[...]) for rows, mix in zip(groups, mixes)]
    ffns = [swiglu(h1) for h1 in h1s]
    for rows, h1, ffn in zip(groups, h1s, ffns):
        o_ref[rows, :] = _layer_norm(ALPHA * h1 + ffn, g2_ref[...], b2_ref[...])


def _tail_call(h2, a, f, layer, wf_bf, b_f, wo_bf, g1, b1, wgu_bf, wd_bf, g2, b2, tm=512):
    m, d = h2.shape
    assert sum(FFN_CHUNKS) == D_FF
    const = dict(pipeline_mode=pl.Buffered(1))
    row = lambda i: (i, 0)
    fixed = lambda i: (0, 0)
    stacked = lambda i: (layer, 0, 0)
    return pl.pallas_call(
        _tail_kernel,
        grid=(m // tm,),
        in_specs=[pl.BlockSpec((tm, d), row),
                  pl.BlockSpec((tm, ATT_WIDTH), row),
                  pl.BlockSpec((tm, FOURIER_WIDTH), row),
                  pl.BlockSpec((None, N_FGROUPS, FGROUP_DIM, FGROUP_DIM), lambda i: (layer, 0, 0, 0), **const),
                  pl.BlockSpec((1, FOURIER_WIDTH), fixed),
                  pl.BlockSpec((None, d, d), stacked, **const),
                  pl.BlockSpec((1, d), fixed),
                  pl.BlockSpec((1, d), fixed),
                  pl.BlockSpec((None, d, 2 * D_FF), stacked, **const),
                  pl.BlockSpec((None, D_FF, d), stacked, **const),
                  pl.BlockSpec((1, d), fixed),
                  pl.BlockSpec((1, d), fixed)],
        out_specs=pl.BlockSpec((tm, d), row),
        out_shape=jax.ShapeDtypeStruct((m, d), F32),
        compiler_params=_params("parallel"),
        name="mix_ffn",
    )(h2, a, f, wf_bf, b_f.reshape(1, -1), wo_bf, g1.reshape(1, d), b1.reshape(1, d),
      wgu_bf, wd_bf, g2.reshape(1, d), b2.reshape(1, d))


def kernel(x, ln_in_g, ln_in_b, w_in, lam_params, subln_g, w_f, b_f, w_o, ln1_g, ln1_b, w_gu, w_down, ln2_g, ln2_b):
    batch, seq, d = x.shape
    assert d == D_MODEL and seq % ATT_TQ == 0
    chan_tab, pos_cos, pos_sin, pos_alt = _dft_tables(seq)
    w_in_bf, w_f_bf, w_o_bf = w_in.astype(BF16), w_f.astype(BF16), w_o.astype(BF16)
    w_gu_bf, w_down_bf = w_gu.astype(BF16), w_down.astype(BF16)
    h = x.reshape(batch * seq, d)
    for l in range(DEPTH):
        if l == 0:
            h, q, kp, vt, vtc, nrm, u = _proj_call(h, w_in_bf, l, batch, seq, ln=(ln_in_g, ln_in_b))
        else:
            q, kp, vt, vtc, nrm, u = _proj_call(h, w_in_bf, l, batch, seq)
        a = _attn_call(q, kp, vt, vtc, nrm, lam_params[l], subln_g[l], batch, seq, _lambda_init(l))
        fe, fo, f0 = _fold_call(u, chan_tab, batch, seq)
        f = _pos_dft_call(pos_cos, pos_sin, pos_alt, fe, fo, f0, batch, seq)
        h = _tail_call(h, a, f, l, w_f_bf, b_f[l], w_o_bf, ln1_g[l], ln1_b[l],
                       w_gu_bf, w_down_bf, ln2_g[l], ln2_b[l])
    return h.reshape(batch, seq, d)
```

```python
import functools
import math

import jax
import jax.numpy as jnp
from jax import lax
from jax.experimental import pallas as pl
from jax.experimental.pallas import tpu as pltpu

D_MODEL = 1024
DEPTH = 2
ATT_WIDTH = 512
FOURIER_WIDTH = 512
N_HEADS = 8
HEAD_DIM = 32
V_HEAD_DIM = 64
N_FGROUPS = 4
FGROUP_DIM = 128
IN_COLS = 2048
D_FF = 2816
LN_EPS = 1e-5
SUBLN_EPS = 1e-5
ALPHA = (2.0 * DEPTH) ** 0.25
LOG2E = math.log2(math.e)

LANES = 128
BF16_SUBLANES = 16
VMEM_LIMIT = 56 * 1024 * 1024

BF16 = jnp.bfloat16
F32 = jnp.float32


def _lambda_init(layer_idx):
    return 0.8 - 0.6 * math.exp(-0.3 * layer_idx)


def _layer_norm(x, g, b):
    mu = jnp.mean(x, axis=-1, keepdims=True)
    xc = x - mu
    var = jnp.mean(xc * xc, axis=-1, keepdims=True)
    return xc * lax.rsqrt(var + LN_EPS) * g + b


def _params(*sem):
    return pltpu.CompilerParams(dimension_semantics=sem, vmem_limit_bytes=VMEM_LIMIT)


def _split3(x):
    hi = x.astype(BF16).astype(F32)
    r = x - hi
    mid = r.astype(BF16).astype(F32)
    lo = (r - mid).astype(BF16).astype(F32)
    return hi, mid, lo


ATT_TQ = 256
HEADS_PER_STEP = LANES // V_HEAD_DIM
CHAINS_PER_STEP = 2 * HEADS_PER_STEP
VT_ROWS = 128
KP_WIDTH = 2 * LANES
NORM_ROUND_UP = 1.05
MAX_UNSHIFTED_SCORE = 50.0


def _slice_norms(x_bf):
    x32 = x_bf.astype(F32)
    seg = lax.broadcasted_iota(jnp.int32, (ATT_WIDTH, LANES), 0) // HEAD_DIM
    col = lax.broadcasted_iota(jnp.int32, (ATT_WIDTH, LANES), 1)
    ind = jnp.where(seg == col, 1.0, 0.0).astype(BF16)
    return jnp.dot((x32 * x32).astype(BF16), ind, preferred_element_type=F32) * NORM_ROUND_UP


def _row_max(x, groups=8):
    r, c = x.shape
    return jnp.max(jnp.max(x.reshape(groups, r // groups, c), axis=0), axis=0, keepdims=True)


def _proj_kernel(*refs, seq, tm, pre_ln):
    if pre_ln:
        x_ref, g_ref, b_ref, w_ref, h_ref, q_ref, kp_ref, vtc_ref, nrm_ref, u_ref = refs
        h = _layer_norm(x_ref[...], g_ref[...], b_ref[...])
        h_ref[...] = h
    else:
        x_ref, w_ref, q_ref, kp_ref, vtc_ref, nrm_ref, u_ref = refs
        h = x_ref[...]
    i = pl.program_id(0)
    hb = h.astype(BF16)
    qf = jnp.dot(hb, w_ref[:, 0:ATT_WIDTH], preferred_element_type=F32)
    qb = (qf * (HEAD_DIM ** -0.5 * LOG2E)).astype(BF16)
    q_ref[...] = qb

    kf = jnp.dot(hb, w_ref[:, ATT_WIDTH:2 * ATT_WIDTH], preferred_element_type=F32).astype(BF16)
    pos = (i * tm) % seq + lax.broadcasted_iota(jnp.int32, (tm, 1), 0)
    hi, mid, lo = _split3((pos - seq // 2).astype(F32) * LOG2E)
    lane = lax.broadcasted_iota(jnp.int32, (tm, LANES), 1)
    aux = jnp.where(lane < 3, 1.0,
                    jnp.where(lane == 3, hi, jnp.where(lane == 4, mid, jnp.where(lane == 5, lo, 0.0))))
    aux = aux.astype(BF16)
    for p in range(N_HEADS // HEADS_PER_STEP):
        kp_ref[:, p * KP_WIDTH:p * KP_WIDTH + LANES] = kf[:, p * LANES:(p + 1) * LANES]
        kp_ref[:, p * KP_WIDTH + LANES:(p + 1) * KP_WIDTH] = aux

    nrm_ref[0:8, :] = jnp.broadcast_to(_row_max(_slice_norms(qb)), (8, LANES))
    nrm_ref[8:16, :] = jnp.broadcast_to(_row_max(_slice_norms(kf)), (8, LANES))

    vf = jnp.dot(hb, w_ref[:, 2 * ATT_WIDTH:3 * ATT_WIDTH], preferred_element_type=F32)
    vtr = vf.T.astype(BF16)
    ones = jnp.ones((VT_ROWS - V_HEAD_DIM, ATT_TQ), BF16)
    for hd in range(N_HEADS):
        v_rows = slice(hd * VT_ROWS, hd * VT_ROWS + V_HEAD_DIM)
        one_rows = slice(hd * VT_ROWS + V_HEAD_DIM, (hd + 1) * VT_ROWS)
        for cc in range(tm // ATT_TQ):
            vtc_ref[cc, v_rows, :] = vtr[hd * V_HEAD_DIM:(hd + 1) * V_HEAD_DIM, cc * ATT_TQ:(cc + 1) * ATT_TQ]
            vtc_ref[cc, one_rows, :] = ones

    u_ref[...] = jnp.dot(hb, w_ref[:, 3 * ATT_WIDTH:], preferred_element_type=F32).astype(BF16)


def _proj_call(x2, w_bf, layer, batch, seq, ln=None, tm=1024):
    m, d = x2.shape
    n = w_bf.shape[2]
    tiles_per_seq = seq // tm
    kern = functools.partial(_proj_kernel, seq=seq, tm=tm, pre_ln=ln is not None)
    row = lambda i: (i, 0)
    fixed = lambda i: (0, 0)
    in_specs = [pl.BlockSpec((tm, d), row), pl.BlockSpec((None, d, n), lambda i: (layer, 0, 0))]
    args = [x2, w_bf]
    out_specs = [pl.BlockSpec((tm, ATT_WIDTH), row),
                 pl.BlockSpec((tm, 4 * KP_WIDTH), row),
                 pl.BlockSpec((None, tm // ATT_TQ, N_HEADS * VT_ROWS, ATT_TQ),
                              lambda i: (i // tiles_per_seq, i % tiles_per_seq, 0, 0)),
                 pl.BlockSpec((None, 16, LANES), lambda i: (i, 0, 0)),
                 pl.BlockSpec((tm, FOURIER_WIDTH), row)]
    out_shape = [jax.ShapeDtypeStruct((m, ATT_WIDTH), BF16),
                 jax.ShapeDtypeStruct((m, 4 * KP_WIDTH), BF16),
                 jax.ShapeDtypeStruct((batch, seq // ATT_TQ, N_HEADS * VT_ROWS, ATT_TQ), BF16),
                 jax.ShapeDtypeStruct((m // tm, 16, LANES), F32),
                 jax.ShapeDtypeStruct((m, FOURIER_WIDTH), BF16)]
    if ln is not None:
        in_specs[1:1] = [pl.BlockSpec((1, d), fixed), pl.BlockSpec((1, d), fixed)]
        args[1:1] = [ln[0].reshape(1, d), ln[1].reshape(1, d)]
        out_specs.insert(0, pl.BlockSpec((tm, d), row))
        out_shape.insert(0, jax.ShapeDtypeStruct((m, d), F32))
    return pl.pallas_call(
        kern,
        grid=(m // tm,),
        in_specs=in_specs,
        out_specs=out_specs,
        out_shape=out_shape,
        compiler_params=_params("parallel"),
        name="proj",
    )(*args)


def _attn_kernel(lam_ref, g_ref, nrm_ref, q_ref, kp_ref, vtc_ref, o_ref, ks_ref, flag_ref, *, lam_init, seq):
    pair = pl.program_id(1)
    ntiles = seq // ATT_TQ
    last = ntiles - 1
    last_rows = slice(last * ATT_TQ, seq)

    def tile_rows(tile):
        return pl.ds(pl.multiple_of(tile * ATT_TQ, ATT_TQ), ATT_TQ)

    def enter_tile(tile):
        prev = tile_rows(jnp.maximum(tile - 1, 0))
        ks_ref[prev, :] = kp_ref[prev, :]
        ks_ref[tile_rows(tile), :LANES] = kp_ref[last_rows, :LANES]
        ks_ref[tile_rows(tile), LANES:] = -kp_ref[last_rows, LANES:]

    ks_ref[:, :LANES] = kp_ref[:, :LANES]
    ks_ref[:, LANES:] = -kp_ref[:, LANES:]
    nrm = jnp.max(nrm_ref[...], axis=0)
    lane = lax.broadcasted_iota(jnp.int32, (1, LANES), 1)
    prod = jnp.where(lane // CHAINS_PER_STEP == pair, nrm[0:1, :] * nrm[8:9, :], 0.0)
    flag_ref[0] = (jnp.max(prod) <= MAX_UNSHIFTED_SCORE ** 2).astype(jnp.int32)

    lp = lam_ref[...]
    lam = (jnp.exp(jnp.sum(lp[0:1] * lp[1:2], axis=-1, keepdims=True))
           - jnp.exp(jnp.sum(lp[2:3] * lp[3:4], axis=-1, keepdims=True)) + lam_init)

    rr = lax.broadcasted_iota(jnp.int32, (ATT_TQ, ATT_TQ), 0)
    cc = lax.broadcasted_iota(jnp.int32, (ATT_TQ, ATT_TQ), 1)
    diag_dist = jnp.abs(rr - cc).astype(F32) * LOG2E
    diag_dist = jnp.concatenate([diag_dist, diag_dist], axis=1)
    row_q = lax.broadcasted_iota(jnp.int32, (LANES, ATT_TQ), 0)
    zeros_t = jnp.zeros((LANES, 2 * ATT_TQ), BF16)
    g_col = g_ref[...]

    def tile_operands(tile):
        q_t = q_ref[tile_rows(tile), :].astype(F32).T
        pos = tile * ATT_TQ + lax.broadcasted_iota(jnp.int32, (1, ATT_TQ), 1)
        hi, mid, lo = _split3((pos - seq // 2).astype(F32) * LOG2E)
        base = jnp.where(row_q == 0, -hi,
                         jnp.where(row_q == 1, -mid,
                                   jnp.where(row_q == 2, -lo, jnp.where(row_q < 6, 1.0, 0.0))))
        return q_t, base

    def stage_operands(tile, hh, q_t, base):
        head = pair * HEADS_PER_STEP + hh
        slope = jnp.exp2(-jnp.full((1, 1), head + 1, jnp.int32).astype(F32))
        aux = (base * slope).astype(BF16)
        qm = jnp.concatenate([jnp.where(row_q // HEAD_DIM == hh * 2 + c, q_t, 0.0).astype(BF16) for c in range(2)],
                             axis=1)
        w_main = jnp.concatenate([qm, jnp.concatenate([aux, aux], axis=1)], axis=0)
        w_diag = jnp.concatenate([qm, zeros_t], axis=0)
        sd = jnp.dot(kp_ref[tile_rows(tile), :], w_diag, preferred_element_type=F32)
        sd = sd - slope * diag_dist
        return w_main, sd

    def attend(tile, hh, q_t, base, shifted):
        w_main, sd = stage_operands(tile, hh, q_t, base)
        st = jnp.dot(ks_ref[0:last * ATT_TQ, :], w_main, preferred_element_type=F32)
        if shifted:
            m = jnp.maximum(jnp.max(sd, axis=0, keepdims=True), jnp.max(st, axis=0, keepdims=True))
            st, sd = st - m, sd - m
        head_rows = slice(hh * VT_ROWS, (hh + 1) * VT_ROWS)
        p = jnp.exp2(st).astype(BF16)
        acc = jnp.dot(vtc_ref[tile, head_rows, :], jnp.exp2(sd).astype(BF16), preferred_element_type=F32)
        for j in range(last):
            chunk = jnp.where(tile == j, last, j)
            acc = acc + jnp.dot(vtc_ref[chunk, head_rows, :], p[j * ATT_TQ:(j + 1) * ATT_TQ, :],
                                preferred_element_type=F32)
        return acc

    def finish(tile, accs):
        outs = []
        for acc in accs:
            norm = []
            for c in range(2):
                part = acc[:, c * ATT_TQ:(c + 1) * ATT_TQ]
                denom = part[V_HEAD_DIM:V_HEAD_DIM + 1, :]
                norm.append(part[:V_HEAD_DIM, :] * (1.0 / denom))
            a_t = norm[0] - lam * norm[1]
            a_t = a_t * lax.rsqrt(jnp.mean(a_t * a_t, axis=0, keepdims=True) + SUBLN_EPS)
            outs.append(a_t * g_col * (1.0 - lam_init))
        o_ref[tile_rows(tile), :] = jnp.concatenate(outs, axis=0).T.astype(o_ref.dtype)

    def run(shifted):
        def body(tile, accs):
            finish(jnp.maximum(tile - 1, 0), accs)
            enter_tile(tile)
            q_t, base = tile_operands(tile)
            return tuple(attend(tile, hh, q_t, base, shifted) for hh in range(HEADS_PER_STEP))
        dummy = tuple(jnp.ones((VT_ROWS, 2 * ATT_TQ), F32) for _ in range(HEADS_PER_STEP))
        finish(last, lax.fori_loop(0, ntiles, body, dummy, unroll=1 if shifted else 2))

    @pl.when(flag_ref[0] == 1)
    def _():
        run(shifted=False)

    @pl.when(flag_ref[0] != 1)
    def _():
        run(shifted=True)


def _attn_call(q, kp, vtc, nrm, lam_params, subln_g, batch, seq, lam_init):
    n_pairs = N_HEADS // HEADS_PER_STEP
    tiles_per_seq = nrm.shape[0] // batch
    kern = functools.partial(_attn_kernel, lam_init=lam_init, seq=seq)
    pair_rows = HEADS_PER_STEP * VT_ROWS
    return pl.pallas_call(
        kern,
        grid=(batch, n_pairs),
        in_specs=[pl.BlockSpec((4, HEAD_DIM), lambda b, p: (0, 0)),
                  pl.BlockSpec((V_HEAD_DIM, 1), lambda b, p: (0, 0)),
                  pl.BlockSpec((tiles_per_seq, 16, LANES), lambda b, p: (b, 0, 0)),
                  pl.BlockSpec((seq, LANES), lambda b, p: (b, p)),
                  pl.BlockSpec((seq, KP_WIDTH), lambda b, p: (b, p)),
                  pl.BlockSpec((None, seq // ATT_TQ, pair_rows, ATT_TQ), lambda b, p: (b, 0, p, 0))],
        out_specs=pl.BlockSpec((seq, LANES), lambda b, p: (b, p)),
        out_shape=jax.ShapeDtypeStruct((batch * seq, ATT_WIDTH), BF16),
        scratch_shapes=[pltpu.VMEM((seq, KP_WIDTH), BF16), pltpu.SMEM((1,), jnp.int32)],
        compiler_params=_params("parallel", "parallel"),
        name="diff_attn",
    )(lam_params, subln_g.reshape(V_HEAD_DIM, 1), nrm, q, kp, vtc)


FOLD_BLK = 256
TABLE_SPLIT = 64


def _dft_tables(seq):
    c = FGROUP_DIM
    ci = lax.broadcasted_iota(jnp.int32, (c, c), 0) * lax.broadcasted_iota(jnp.int32, (c, c), 1) % c
    ang_c = ci.astype(F32) * (2.0 * math.pi / c)
    cc, sc = jnp.cos(ang_c) * c ** -0.5, jnp.sin(ang_c) * c ** -0.5
    chan = jnp.concatenate([jnp.concatenate([cc, sc], axis=1), jnp.concatenate([cc, -sc], axis=1)], axis=0)
    half = seq // 2
    n = lax.broadcasted_iota(jnp.int32, (1, half), 1) + 1
    weight = jnp.where(n == half, 0.5, 1.0) * seq ** -0.5

    def thin(count, stride):
        kk = lax.broadcasted_iota(jnp.int32, (count, half), 0) * stride
        ang = (kk * n % seq).astype(F32) * (2.0 * math.pi / seq)
        return jnp.cos(ang), jnp.sin(ang)

    ca, sa = thin(half // TABLE_SPLIT, TABLE_SPLIT)
    cb, sb = thin(TABLE_SPLIT, 1)
    cb, sb = cb * weight, sb * weight
    pos_cos = (ca[:, None, :] * cb[None] - sa[:, None, :] * sb[None]).reshape(half, half).astype(BF16)
    pos_sin = (sa[:, None, :] * cb[None] + ca[:, None, :] * sb[None]).reshape(half, half).astype(BF16)
    n1 = lax.broadcasted_iota(jnp.int32, (BF16_SUBLANES, half), 1) + 1
    row = lax.broadcasted_iota(jnp.int32, (BF16_SUBLANES, half), 0)
    alt = jnp.where(row == 0, jnp.where(n1 % 2 == 0, 1.0, -1.0), 0.0)
    alt = (alt * jnp.where(n1 == half, 0.5, 1.0) * seq ** -0.5).astype(BF16)
    return chan.astype(BF16), pos_cos, pos_sin, alt


def _select(rows, cols, offset):
    r = lax.broadcasted_iota(jnp.int32, (rows, cols), 0)
    c = lax.broadcasted_iota(jnp.int32, (rows, cols), 1)
    return jnp.where(c == offset - r, 1.0, 0.0).astype(BF16)


def _fold_kernel(u_ref, t_ref, ae_ref, bo_ref, a0_ref, *, seq):
    half = seq // 2
    t = t_ref[...]
    r = lax.broadcasted_iota(jnp.int32, (FOLD_BLK, 2 * FOLD_BLK), 0)
    c = lax.broadcasted_iota(jnp.int32, (FOLD_BLK, 2 * FOLD_BLK), 1)
    shift_one = jnp.where(c == r + 1, 1.0, 0.0).astype(BF16)
    reverse = _select(FOLD_BLK, FOLD_BLK, FOLD_BLK - 1)
    for i in range(half // FOLD_BLK):
        lo = i * FOLD_BLK
        fwd = jnp.dot(shift_one, u_ref[lo:lo + 2 * FOLD_BLK, :], preferred_element_type=F32).astype(BF16)
        hi = seq - lo - FOLD_BLK
        bwd = jnp.dot(reverse, u_ref[hi:hi + FOLD_BLK, :], preferred_element_type=F32).astype(BF16)
        for gi in range(N_FGROUPS):
            cols = slice(gi * FGROUP_DIM, (gi + 1) * FGROUP_DIM)
            both = jnp.concatenate([fwd[:, cols], bwd[:, cols]], axis=1)
            res = jnp.dot(both, t, preferred_element_type=F32)
            ae_ref[lo:lo + FOLD_BLK, cols] = res[:, :FGROUP_DIM].astype(ae_ref.dtype)
            bo_ref[lo:lo + FOLD_BLK, cols] = res[:, FGROUP_DIM:].astype(bo_ref.dtype)
    for gi in range(N_FGROUPS):
        cols = slice(gi * FGROUP_DIM, (gi + 1) * FGROUP_DIM)
        first = jnp.dot(u_ref[0:BF16_SUBLANES, cols], t[:FGROUP_DIM, :FGROUP_DIM], preferred_element_type=F32)
        a0_ref[:, cols] = jnp.broadcast_to(first[0:1, :], (a0_ref.shape[0], FGROUP_DIM))


def _fold_call(u, chan_tab, batch, seq):
    half = seq // 2
    out = jax.ShapeDtypeStruct((batch * half, FOURIER_WIDTH), BF16)
    return pl.pallas_call(
        functools.partial(_fold_kernel, seq=seq),
        grid=(batch,),
        in_specs=[pl.BlockSpec((seq, FOURIER_WIDTH), lambda b: (b, 0)),
                  pl.BlockSpec((2 * FGROUP_DIM, 2 * FGROUP_DIM), lambda b: (0, 0))],
        out_specs=[pl.BlockSpec((half, FOURIER_WIDTH), lambda b: (b, 0)),
                   pl.BlockSpec((half, FOURIER_WIDTH), lambda b: (b, 0)),
                   pl.BlockSpec((None, 8, FOURIER_WIDTH), lambda b: (b, 0, 0))],
        out_shape=[out, out, jax.ShapeDtypeStruct((batch, 8, FOURIER_WIDTH), F32)],
        compiler_params=_params("parallel"),
        name="dft_fold",
    )(u, chan_tab)


def _pos_dft_kernel(c_ref, s_ref, alt_ref, ae_ref, bo_ref, a0_ref, o_ref, *, seq):
    half = seq // 2
    ae = ae_ref[...]
    dc = a0_ref[0:1, :] * seq ** -0.5
    even = jnp.dot(c_ref[...], ae, preferred_element_type=F32) + dc
    odd = jnp.dot(s_ref[...], bo_ref[...], preferred_element_type=F32)
    o_ref[0:half, :] = (even - odd).astype(o_ref.dtype)
    mirror = (even + odd).astype(BF16)
    middle = jnp.dot(alt_ref[...], ae, preferred_element_type=F32)[0:1, :] + dc
    middle = jnp.broadcast_to(middle, (FOLD_BLK, FOURIER_WIDTH)).astype(BF16)
    take = _select(FOLD_BLK, 2 * FOLD_BLK, FOLD_BLK)
    nblk = half // FOLD_BLK
    for bt in range(nblk):
        upper = middle if bt == 0 else mirror[(nblk - bt) * FOLD_BLK:(nblk - bt + 1) * FOLD_BLK, :]
        lower = mirror[(nblk - 1 - bt) * FOLD_BLK:(nblk - bt) * FOLD_BLK, :]
        blk = jnp.dot(take, jnp.concatenate([lower, upper], axis=0), preferred_element_type=F32)
        o_ref[half + bt * FOLD_BLK:half + (bt + 1) * FOLD_BLK, :] = blk.astype(o_ref.dtype)


def _pos_dft_call(pos_cos, pos_sin, alt, ae, bo, a0, batch, seq):
    half = seq // 2
    const = dict(pipeline_mode=pl.Buffered(1))
    return pl.pallas_call(
        functools.partial(_pos_dft_kernel, seq=seq),
        grid=(batch,),
        in_specs=[pl.BlockSpec((half, half), lambda b: (0, 0), **const),
                  pl.BlockSpec((half, half), lambda b: (0, 0), **const),
                  pl.BlockSpec((BF16_SUBLANES, half), lambda b: (0, 0)),
                  pl.BlockSpec((half, FOURIER_WIDTH), lambda b: (b, 0)),
                  pl.BlockSpec((half, FOURIER_WIDTH), lambda b: (b, 0)),
                  pl.BlockSpec((None, 8, FOURIER_WIDTH), lambda b: (b, 0, 0))],
        out_specs=pl.BlockSpec((seq, FOURIER_WIDTH), lambda b: (b, 0)),
        out_shape=jax.ShapeDtypeStruct((batch * seq, FOURIER_WIDTH), BF16),
        compiler_params=_params("parallel"),
        name="pos_dft",
    )(pos_cos, pos_sin, alt, ae, bo, a0)


FFN_CHUNKS = (1536, 1280)
TAIL_SUBTILES = 2


def _tail_kernel(h_ref, a_ref, f_ref, wf_ref, bf_ref, wo_ref, g1_ref, b1_ref, wgu_ref, wd_ref, g2_ref, b2_ref,
                 o_ref):
    sub = h_ref.shape[0] // TAIL_SUBTILES
    groups = [slice(s * sub, (s + 1) * sub) for s in range(TAIL_SUBTILES)]

    def mixed(rows):
        parts = []
        for gi in range(N_FGROUPS):
            cols = slice(gi * FGROUP_DIM, (gi + 1) * FGROUP_DIM)
            parts.append(jnp.dot(f_ref[rows, cols], wf_ref[gi], preferred_element_type=F32))
        y = jnp.concatenate(parts, axis=-1) + bf_ref[...]
        mix = jnp.dot(a_ref[rows, :], wo_ref[:ATT_WIDTH, :], preferred_element_type=F32)
        return mix + jnp.dot(y.astype(BF16), wo_ref[ATT_WIDTH:, :], preferred_element_type=F32)

    def swiglu(h1):
        hb = h1.astype(BF16)
        acc = None
        lo = 0
        for width in FFN_CHUNKS:
            gate = jnp.dot(hb, wgu_ref[:, lo:lo + width], preferred_element_type=F32)
            up = jnp.dot(hb, wgu_ref[:, D_FF + lo:D_FF + lo + width], preferred_element_type=F32)
            act = (gate * jax.nn.sigmoid(gate) * up).astype(BF16)
            part = jnp.dot(act, wd_ref[lo:lo + width, :], preferred_element_type=F32)
            acc = part if acc is None else acc + part
            lo += width
        return acc

    mixes = [mixed(rows) for rows in groups]
    h1s = [_layer_norm(ALPHA * h_ref[rows, :] + mix, g1_ref[...], b1_ref[...]) for rows, mix in zip(groups, mixes)]
    ffns = [swiglu(h1) for h1 in h1s]
    for rows, h1, ffn in zip(groups, h1s, ffns):
        o_ref[rows, :] = _layer_norm(ALPHA * h1 + ffn, g2_ref[...], b2_ref[...])


def _tail_call(h2, a, f, layer, wf_bf, b_f, wo_bf, g1, b1, wgu_bf, wd_bf, g2, b2, tm=512):
    m, d = h2.shape
    assert sum(FFN_CHUNKS) == D_FF
    const = dict(pipeline_mode=pl.Buffered(1))
    row = lambda i: (i, 0)
    fixed = lambda i: (0, 0)
    stacked = lambda i: (layer, 0, 0)
    return pl.pallas_call(
        _tail_kernel,
        grid=(m // tm,),
        in_specs=[pl.BlockSpec((tm, d), row),
                  pl.BlockSpec((tm, ATT_WIDTH), row),
                  pl.BlockSpec((tm, FOURIER_WIDTH), row),
                  pl.BlockSpec((None, N_FGROUPS, FGROUP_DIM, FGROUP_DIM), lambda i: (layer, 0, 0, 0), **const),
                  pl.BlockSpec((1, FOURIER_WIDTH), fixed),
                  pl.BlockSpec((None, d, d), stacked, **const),
                  pl.BlockSpec((1, d), fixed),
                  pl.BlockSpec((1, d), fixed),
                  pl.BlockSpec((None, d, 2 * D_FF), stacked, **const),
                  pl.BlockSpec((None, D_FF, d), stacked, **const),
                  pl.BlockSpec((1, d), fixed),
                  pl.BlockSpec((1, d), fixed)],
        out_specs=pl.BlockSpec((tm, d), row),
        out_shape=jax.ShapeDtypeStruct((m, d), F32),
        compiler_params=_params("parallel"),
        name="mix_ffn",
    )(h2, a, f, wf_bf, b_f.reshape(1, -1), wo_bf, g1.reshape(1, d), b1.reshape(1, d),
      wgu_bf, wd_bf, g2.reshape(1, d), b2.reshape(1, d))


def kernel(x, ln_in_g, ln_in_b, w_in, lam_params, subln_g, w_f, b_f, w_o, ln1_g, ln1_b, w_gu, w_down, ln2_g, ln2_b):
    batch, seq, d = x.shape
    assert d == D_MODEL and seq % ATT_TQ == 0
    chan_tab, pos_cos, pos_sin, pos_alt = _dft_tables(seq)
    w_in_bf, w_f_bf, w_o_bf = w_in.astype(BF16), w_f.astype(BF16), w_o.astype(BF16)
    w_gu_bf, w_down_bf = w_gu.astype(BF16), w_down.astype(BF16)
    h = x.reshape(batch * seq, d)
    for l in range(DEPTH):
        if l == 0:
            h, q, kp, vtc, nrm, u = _proj_call(h, w_in_bf, l, batch, seq, ln=(ln_in_g, ln_in_b))
        else:
            q, kp, vtc, nrm, u = _proj_call(h, w_in_bf, l, batch, seq)
        a = _attn_call(q, kp, vtc, nrm, lam_params[l], subln_g[l], batch, seq, _lambda_init(l))
        fe, fo, f0 = _fold_call(u, chan_tab, batch, seq)
        f = _pos_dft_call(pos_cos, pos_sin, pos_alt, fe, fo, f0, batch, seq)
        h = _tail_call(h, a, f, l, w_f_bf, b_f[l], w_o_bf, ln1_g[l], ln1_b[l],
                       w_gu_bf, w_down_bf, ln2_g[l], ln2_b[l])
    return h.reshape(batch, seq, d)
```

```python
import functools
import math

import jax
import jax.numpy as jnp
from jax import lax
from jax.experimental import pallas as pl
from jax.experimental.pallas import tpu as pltpu

D_MODEL = 1024
DEPTH = 2
ATT_WIDTH = 512
FOURIER_WIDTH = 512
N_HEADS = 8
HEAD_DIM = 32
V_HEAD_DIM = 64
N_FGROUPS = 4
FGROUP_DIM = 128
D_FF = 2816
LN_EPS = 1e-5
SUBLN_EPS = 1e-5
ALPHA = (2.0 * DEPTH) ** 0.25
LOG2E = math.log2(math.e)

LANES = 128
BF16_SUBLANES = 16
VMEM_LIMIT = 56 * 1024 * 1024

BF16 = jnp.bfloat16
F32 = jnp.float32


def _lambda_init(layer_idx):
    return 0.8 - 0.6 * math.exp(-0.3 * layer_idx)


def _layer_norm(x, g, b):
    mu = jnp.mean(x, axis=-1, keepdims=True)
    xc = x - mu
    var = jnp.mean(xc * xc, axis=-1, keepdims=True)
    return xc * lax.rsqrt(var + LN_EPS) * g + b


def _params(*sem):
    return pltpu.CompilerParams(dimension_semantics=sem, vmem_limit_bytes=VMEM_LIMIT)


def _split3(x):
    hi = x.astype(BF16).astype(F32)
    r = x - hi
    mid = r.astype(BF16).astype(F32)
    lo = (r - mid).astype(BF16).astype(F32)
    return hi, mid, lo


ATT_TQ = 256
ATT_RANGE = 8
HEADS_PER_STEP = LANES // V_HEAD_DIM
CHAINS_PER_STEP = 2 * HEADS_PER_STEP
VT_ROWS = 128
KP_WIDTH = 2 * LANES
NORM_ROUND_UP = 1.05
NORM_ROWS = 8
MAX_UNSHIFTED_SCORE = 50.0


def _slice_norms(x_bf):
    x32 = x_bf.astype(F32)
    seg = lax.broadcasted_iota(jnp.int32, (ATT_WIDTH, LANES), 0) // HEAD_DIM
    col = lax.broadcasted_iota(jnp.int32, (ATT_WIDTH, LANES), 1)
    ind = jnp.where(seg == col, 1.0, 0.0).astype(BF16)
    return jnp.dot((x32 * x32).astype(BF16), ind, preferred_element_type=F32) * NORM_ROUND_UP


def _row_max(x, groups=8):
    r, c = x.shape
    return jnp.max(jnp.max(x.reshape(groups, r // groups, c), axis=0), axis=0, keepdims=True)


def _proj_kernel(*refs, seq, tm, pre_ln):
    if pre_ln:
        x_ref, g_ref, b_ref, w_ref, h_ref, q_ref, kp_ref, vtc_ref, nrm_ref, u_ref = refs
        h = _layer_norm(x_ref[...], g_ref[...], b_ref[...])
        h_ref[...] = h
    else:
        x_ref, w_ref, q_ref, kp_ref, vtc_ref, nrm_ref, u_ref = refs
        h = x_ref[...]
    i = pl.program_id(0)
    hb = h.astype(BF16)
    qf = jnp.dot(hb, w_ref[:, 0:ATT_WIDTH], preferred_element_type=F32)
    qb = (qf * (HEAD_DIM ** -0.5 * LOG2E)).astype(BF16)
    q_ref[...] = qb

    kf = jnp.dot(hb, w_ref[:, ATT_WIDTH:2 * ATT_WIDTH], preferred_element_type=F32).astype(BF16)
    pos = (i * tm) % seq + lax.broadcasted_iota(jnp.int32, (tm, 1), 0)
    hi, mid, lo = _split3((pos - seq // 2).astype(F32) * LOG2E)
    lane = lax.broadcasted_iota(jnp.int32, (tm, LANES), 1)
    aux = jnp.where(lane < 3, 1.0,
                    jnp.where(lane == 3, hi, jnp.where(lane == 4, mid, jnp.where(lane == 5, lo, 0.0))))
    aux = aux.astype(BF16)
    for p in range(N_HEADS // HEADS_PER_STEP):
        kp_ref[:, p * KP_WIDTH:p * KP_WIDTH + LANES] = kf[:, p * LANES:(p + 1) * LANES]
        kp_ref[:, p * KP_WIDTH + LANES:(p + 1) * KP_WIDTH] = aux

    nrm_ref[0:NORM_ROWS, :] = jnp.broadcast_to(_row_max(_slice_norms(qb)), (NORM_ROWS, LANES))
    nrm_ref[NORM_ROWS:, :] = jnp.broadcast_to(_row_max(_slice_norms(kf)), (NORM_ROWS, LANES))

    vf = jnp.dot(hb, w_ref[:, 2 * ATT_WIDTH:3 * ATT_WIDTH], preferred_element_type=F32)
    vtr = vf.T.astype(BF16)
    ones = jnp.ones((VT_ROWS - V_HEAD_DIM, ATT_TQ), BF16)
    for hd in range(N_HEADS):
        v_rows = slice(hd * VT_ROWS, hd * VT_ROWS + V_HEAD_DIM)
        one_rows = slice(hd * VT_ROWS + V_HEAD_DIM, (hd + 1) * VT_ROWS)
        for cc in range(tm // ATT_TQ):
            vtc_ref[cc, v_rows, :] = vtr[hd * V_HEAD_DIM:(hd + 1) * V_HEAD_DIM, cc * ATT_TQ:(cc + 1) * ATT_TQ]
            vtc_ref[cc, one_rows, :] = ones

    u_ref[...] = jnp.dot(hb, w_ref[:, 3 * ATT_WIDTH:], preferred_element_type=F32).astype(BF16)


def _proj_call(x2, w_bf, layer, batch, seq, ln=None, tm=1024):
    m, d = x2.shape
    n = w_bf.shape[2]
    tiles_per_seq = seq // tm
    kern = functools.partial(_proj_kernel, seq=seq, tm=tm, pre_ln=ln is not None)
    row = lambda i: (i, 0)
    fixed = lambda i: (0, 0)
    in_specs = [pl.BlockSpec((tm, d), row), pl.BlockSpec((None, d, n), lambda i: (layer, 0, 0))]
    args = [x2, w_bf]
    out_specs = [pl.BlockSpec((tm, ATT_WIDTH), row),
                 pl.BlockSpec((tm, 4 * KP_WIDTH), row),
                 pl.BlockSpec((None, tm // ATT_TQ, N_HEADS * VT_ROWS, ATT_TQ),
                              lambda i: (i // tiles_per_seq, i % tiles_per_seq, 0, 0)),
                 pl.BlockSpec((None, 2 * NORM_ROWS, LANES), lambda i: (i, 0, 0)),
                 pl.BlockSpec((tm, FOURIER_WIDTH), row)]
    out_shape = [jax.ShapeDtypeStruct((m, ATT_WIDTH), BF16),
                 jax.ShapeDtypeStruct((m, 4 * KP_WIDTH), BF16),
                 jax.ShapeDtypeStruct((batch, seq // ATT_TQ, N_HEADS * VT_ROWS, ATT_TQ), BF16),
                 jax.ShapeDtypeStruct((m // tm, 2 * NORM_ROWS, LANES), F32),
                 jax.ShapeDtypeStruct((m, FOURIER_WIDTH), BF16)]
    if ln is not None:
        in_specs[1:1] = [pl.BlockSpec((1, d), fixed), pl.BlockSpec((1, d), fixed)]
        args[1:1] = [ln[0].reshape(1, d), ln[1].reshape(1, d)]
        out_specs.insert(0, pl.BlockSpec((tm, d), row))
        out_shape.insert(0, jax.ShapeDtypeStruct((m, d), F32))
    return pl.pallas_call(
        kern,
        grid=(m // tm,),
        in_specs=in_specs,
        out_specs=out_specs,
        out_shape=out_shape,
        compiler_params=_params("parallel"),
        name="proj",
    )(*args)


def _attn_kernel(lam_ref, g_ref, nrm_ref, q_ref, kp_ref, vtc_ref, o_ref, ks_ref, flag_ref, *, lam_init, seq):
    pair = pl.program_id(1)
    ntiles = seq // ATT_TQ
    last = ntiles - 1
    last_rows = slice(last * ATT_TQ, seq)

    def tile_rows(tile):
        return pl.ds(pl.multiple_of(tile * ATT_TQ, ATT_TQ), ATT_TQ)

    def enter_tile(tile):
        prev = tile_rows(jnp.maximum(tile - 1, 0))
        ks_ref[prev, :] = kp_ref[prev, :]
        ks_ref[tile_rows(tile), :LANES] = kp_ref[last_rows, :LANES]
        ks_ref[tile_rows(tile), LANES:] = -kp_ref[last_rows, LANES:]

    ks_ref[:, :LANES] = kp_ref[:, :LANES]
    ks_ref[:, LANES:] = -kp_ref[:, LANES:]
    nrm = jnp.max(nrm_ref[...], axis=0)
    lane = lax.broadcasted_iota(jnp.int32, (1, LANES), 1)
    prod = jnp.where(lane // CHAINS_PER_STEP == pair, nrm[0:1, :] * nrm[NORM_ROWS:NORM_ROWS + 1, :], 0.0)
    flag_ref[0] = (jnp.max(prod) <= MAX_UNSHIFTED_SCORE ** 2).astype(jnp.int32)

    lp = lam_ref[...]
    lam = (jnp.exp(jnp.sum(lp[0:1] * lp[1:2], axis=-1, keepdims=True))
           - jnp.exp(jnp.sum(lp[2:3] * lp[3:4], axis=-1, keepdims=True)) + lam_init)

    rr = lax.broadcasted_iota(jnp.int32, (ATT_TQ, ATT_TQ), 0)
    cc = lax.broadcasted_iota(jnp.int32, (ATT_TQ, ATT_TQ), 1)
    diag_dist = jnp.abs(rr - cc).astype(F32) * LOG2E
    diag_dist = jnp.concatenate([diag_dist, diag_dist], axis=1)
    row_q = lax.broadcasted_iota(jnp.int32, (LANES, ATT_TQ), 0)
    zeros_t = jnp.zeros((LANES, 2 * ATT_TQ), BF16)
    g_col = g_ref[...]

    def tile_operands(tile):
        q_t = q_ref[tile_rows(tile), :].astype(F32).T
        pos = tile * ATT_TQ + lax.broadcasted_iota(jnp.int32, (1, ATT_TQ), 1)
        hi, mid, lo = _split3((pos - seq // 2).astype(F32) * LOG2E)
        base = jnp.where(row_q == 0, -hi,
                         jnp.where(row_q == 1, -mid,
                                   jnp.where(row_q == 2, -lo, jnp.where(row_q < 6, 1.0, 0.0))))
        return q_t, base

    def stage_operands(tile, hh, q_t, base):
        head = pair * HEADS_PER_STEP + hh
        slope = jnp.exp2(-jnp.full((1, 1), head + 1, jnp.int32).astype(F32))
        aux = (base * slope).astype(BF16)
        qm = jnp.concatenate([jnp.where(row_q // HEAD_DIM == hh * 2 + c, q_t, 0.0).astype(BF16) for c in range(2)],
                             axis=1)
        w_main = jnp.concatenate([qm, jnp.concatenate([aux, aux], axis=1)], axis=0)
        w_diag = jnp.concatenate([qm, zeros_t], axis=0)
        sd = jnp.dot(kp_ref[tile_rows(tile), :], w_diag, preferred_element_type=F32)
        sd = sd - slope * diag_dist
        return w_main, sd

    def attend(tile, hh, q_t, base, shifted):
        w_main, sd = stage_operands(tile, hh, q_t, base)
        head_rows = slice(hh * VT_ROWS, (hh + 1) * VT_ROWS)

        def scores(lo, hi):
            return jnp.dot(ks_ref[lo * ATT_TQ:hi * ATT_TQ, :], w_main, preferred_element_type=F32)

        def weighted_values(acc, lo, st):
            p = jnp.exp2(st).astype(BF16)
            for j in range(st.shape[0] // ATT_TQ):
                blk = lo + j
                chunk = jnp.where(tile == blk, last, blk)
                acc = acc + jnp.dot(vtc_ref[chunk, head_rows, :], p[j * ATT_TQ:(j + 1) * ATT_TQ, :],
                                    preferred_element_type=F32)
            return acc

        ranges = [(lo, min(lo + ATT_RANGE, last)) for lo in range(0, last, ATT_RANGE)]
        if shifted:
            sts = [scores(lo, hi) for lo, hi in ranges]
            m = jnp.max(sd, axis=0, keepdims=True)
            for st in sts:
                m = jnp.maximum(m, jnp.max(st, axis=0, keepdims=True))
            acc = jnp.dot(vtc_ref[tile, head_rows, :], jnp.exp2(sd - m).astype(BF16), preferred_element_type=F32)
            for (lo, _), st in zip(ranges, sts):
                acc = weighted_values(acc, lo, st - m)
            return acc
        acc = jnp.dot(vtc_ref[tile, head_rows, :], jnp.exp2(sd).astype(BF16), preferred_element_type=F32)
        pending = scores(*ranges[0])
        for r, (lo, _) in enumerate(ranges):
            following = scores(*ranges[r + 1]) if r + 1 < len(ranges) else None
            acc = weighted_values(acc, lo, pending)
            pending = following
        return acc

    def finish(tile, accs):
        outs = []
        for acc in accs:
            norm = []
            for c in range(2):
                part = acc[:, c * ATT_TQ:(c + 1) * ATT_TQ]
                denom = part[V_HEAD_DIM:V_HEAD_DIM + 1, :]
                norm.append(part[:V_HEAD_DIM, :] * (1.0 / denom))
            a_t = norm[0] - lam * norm[1]
            a_t = a_t * lax.rsqrt(jnp.mean(a_t * a_t, axis=0, keepdims=True) + SUBLN_EPS)
            outs.append(a_t * g_col * (1.0 - lam_init))
        o_ref[tile_rows(tile), :] = jnp.concatenate(outs, axis=0).T.astype(o_ref.dtype)

    def run(shifted):
        def body(tile, accs):
            finish(jnp.maximum(tile - 1, 0), accs)
            enter_tile(tile)
            q_t, base = tile_operands(tile)
            return tuple(attend(tile, hh, q_t, base, shifted) for hh in range(HEADS_PER_STEP))
        dummy = tuple(jnp.ones((VT_ROWS, 2 * ATT_TQ), F32) for _ in range(HEADS_PER_STEP))
        finish(last, lax.fori_loop(0, ntiles, body, dummy, unroll=1 if shifted else 2))

    @pl.when(flag_ref[0] == 1)
    def _():
        run(shifted=False)

    @pl.when(flag_ref[0] != 1)
    def _():
        run(shifted=True)


def _attn_call(q, kp, vtc, nrm, lam_params, subln_g, batch, seq, lam_init):
    n_pairs = N_HEADS // HEADS_PER_STEP
    tiles_per_seq = nrm.shape[0] // batch
    kern = functools.partial(_attn_kernel, lam_init=lam_init, seq=seq)
    pair_rows = HEADS_PER_STEP * VT_ROWS
    return pl.pallas_call(
        kern,
        grid=(batch, n_pairs),
        in_specs=[pl.BlockSpec((4, HEAD_DIM), lambda b, p: (0, 0)),
                  pl.BlockSpec((V_HEAD_DIM, 1), lambda b, p: (0, 0)),
                  pl.BlockSpec((tiles_per_seq, 2 * NORM_ROWS, LANES), lambda b, p: (b, 0, 0)),
                  pl.BlockSpec((seq, LANES), lambda b, p: (b, p)),
                  pl.BlockSpec((seq, KP_WIDTH), lambda b, p: (b, p)),
                  pl.BlockSpec((None, seq // ATT_TQ, pair_rows, ATT_TQ), lambda b, p: (b, 0, p, 0))],
        out_specs=pl.BlockSpec((seq, LANES), lambda b, p: (b, p)),
        out_shape=jax.ShapeDtypeStruct((batch * seq, ATT_WIDTH), BF16),
        scratch_shapes=[pltpu.VMEM((seq, KP_WIDTH), BF16), pltpu.SMEM((1,), jnp.int32)],
        compiler_params=_params("parallel", "parallel"),
        name="diff_attn",
    )(lam_params, subln_g.reshape(V_HEAD_DIM, 1), nrm, q, kp, vtc)


FOLD_BLK = 256
TABLE_SPLIT = 64


def _dft_tables(seq):
    c = FGROUP_DIM
    ci = lax.broadcasted_iota(jnp.int32, (c, c), 0) * lax.broadcasted_iota(jnp.int32, (c, c), 1) % c
    ang_c = ci.astype(F32) * (2.0 * math.pi / c)
    cc, sc = jnp.cos(ang_c) * c ** -0.5, jnp.sin(ang_c) * c ** -0.5
    chan = jnp.concatenate([jnp.concatenate([cc, sc], axis=1), jnp.concatenate([cc, -sc], axis=1)], axis=0)
    half = seq // 2
    n = lax.broadcasted_iota(jnp.int32, (1, half), 1) + 1
    weight = jnp.where(n == half, 0.5, 1.0) * seq ** -0.5

    def thin(count, stride):
        kk = lax.broadcasted_iota(jnp.int32, (count, half), 0) * stride
        ang = (kk * n % seq).astype(F32) * (2.0 * math.pi / seq)
        return jnp.cos(ang), jnp.sin(ang)

    ca, sa = thin(half // TABLE_SPLIT, TABLE_SPLIT)
    cb, sb = thin(TABLE_SPLIT, 1)
    cb, sb = cb * weight, sb * weight
    pos_cos = (ca[:, None, :] * cb[None] - sa[:, None, :] * sb[None]).reshape(half, half).astype(BF16)
    pos_sin = (sa[:, None, :] * cb[None] + ca[:, None, :] * sb[None]).reshape(half, half).astype(BF16)
    n1 = lax.broadcasted_iota(jnp.int32, (BF16_SUBLANES, half), 1) + 1
    row = lax.broadcasted_iota(jnp.int32, (BF16_SUBLANES, half), 0)
    alt = jnp.where(row == 0, jnp.where(n1 % 2 == 0, 1.0, -1.0), 0.0)
    alt = (alt * jnp.where(n1 == half, 0.5, 1.0) * seq ** -0.5).astype(BF16)
    return chan.astype(BF16), pos_cos, pos_sin, alt


def _select(rows, cols, offset):
    r = lax.broadcasted_iota(jnp.int32, (rows, cols), 0)
    c = lax.broadcasted_iota(jnp.int32, (rows, cols), 1)
    return jnp.where(c == offset - r, 1.0, 0.0).astype(BF16)


def _fold_kernel(u_ref, t_ref, ae_ref, bo_ref, a0_ref, *, seq):
    half = seq // 2
    t = t_ref[...]
    r = lax.broadcasted_iota(jnp.int32, (FOLD_BLK, 2 * FOLD_BLK), 0)
    c = lax.broadcasted_iota(jnp.int32, (FOLD_BLK, 2 * FOLD_BLK), 1)
    shift_one = jnp.where(c == r + 1, 1.0, 0.0).astype(BF16)
    reverse = _select(FOLD_BLK, FOLD_BLK, FOLD_BLK - 1)
    for i in range(half // FOLD_BLK):
        lo = i * FOLD_BLK
        fwd = jnp.dot(shift_one, u_ref[lo:lo + 2 * FOLD_BLK, :], preferred_element_type=F32).astype(BF16)
        hi = seq - lo - FOLD_BLK
        bwd = jnp.dot(reverse, u_ref[hi:hi + FOLD_BLK, :], preferred_element_type=F32).astype(BF16)
        for gi in range(N_FGROUPS):
            cols = slice(gi * FGROUP_DIM, (gi + 1) * FGROUP_DIM)
            both = jnp.concatenate([fwd[:, cols], bwd[:, cols]], axis=1)
            res = jnp.dot(both, t, preferred_element_type=F32)
            ae_ref[lo:lo + FOLD_BLK, cols] = res[:, :FGROUP_DIM].astype(ae_ref.dtype)
            bo_ref[lo:lo + FOLD_BLK, cols] = res[:, FGROUP_DIM:].astype(bo_ref.dtype)
    for gi in range(N_FGROUPS):
        cols = slice(gi * FGROUP_DIM, (gi + 1) * FGROUP_DIM)
        first = jnp.dot(u_ref[0:BF16_SUBLANES, cols], t[:FGROUP_DIM, :FGROUP_DIM], preferred_element_type=F32)
        a0_ref[:, cols] = jnp.broadcast_to(first[0:1, :], (a0_ref.shape[0], FGROUP_DIM))


def _fold_call(u, chan_tab, batch, seq):
    half = seq // 2
    out = jax.ShapeDtypeStruct((batch * half, FOURIER_WIDTH), BF16)
    return pl.pallas_call(
        functools.partial(_fold_kernel, seq=seq),
        grid=(batch,),
        in_specs=[pl.BlockSpec((seq, FOURIER_WIDTH), lambda b: (b, 0)),
                  pl.BlockSpec((2 * FGROUP_DIM, 2 * FGROUP_DIM), lambda b: (0, 0))],
        out_specs=[pl.BlockSpec((half, FOURIER_WIDTH), lambda b: (b, 0)),
                   pl.BlockSpec((half, FOURIER_WIDTH), lambda b: (b, 0)),
                   pl.BlockSpec((None, 8, FOURIER_WIDTH), lambda b: (b, 0, 0))],
        out_shape=[out, out, jax.ShapeDtypeStruct((batch, 8, FOURIER_WIDTH), F32)],
        compiler_params=_params("parallel"),
        name="dft_fold",
    )(u, chan_tab)


def _pos_dft_kernel(c_ref, s_ref, alt_ref, ae_ref, bo_ref, a0_ref, o_ref, *, seq):
    half = seq // 2
    ae = ae_ref[...]
    dc = a0_ref[0:1, :] * seq ** -0.5
    even = jnp.dot(c_ref[...], ae, preferred_element_type=F32) + dc
    odd = jnp.dot(s_ref[...], bo_ref[...], preferred_element_type=F32)
    o_ref[0:half, :] = (even - odd).astype(o_ref.dtype)
    mirror = (even + odd).astype(BF16)
    middle = jnp.dot(alt_ref[...], ae, preferred_element_type=F32)[0:1, :] + dc
    middle = jnp.broadcast_to(middle, (FOLD_BLK, FOURIER_WIDTH)).astype(BF16)
    take = _select(FOLD_BLK, 2 * FOLD_BLK, FOLD_BLK)
    nblk = half // FOLD_BLK
    for bt in range(nblk):
        upper = middle if bt == 0 else mirror[(nblk - bt) * FOLD_BLK:(nblk - bt + 1) * FOLD_BLK, :]
        lower = mirror[(nblk - 1 - bt) * FOLD_BLK:(nblk - bt) * FOLD_BLK, :]
        blk = jnp.dot(take, jnp.concatenate([lower, upper], axis=0), preferred_element_type=F32)
        o_ref[half + bt * FOLD_BLK:half + (bt + 1) * FOLD_BLK, :] = blk.astype(o_ref.dtype)


def _pos_dft_call(pos_cos, pos_sin, alt, ae, bo, a0, batch, seq):
    half = seq // 2
    const = dict(pipeline_mode=pl.Buffered(1))
    return pl.pallas_call(
        functools.partial(_pos_dft_kernel, seq=seq),
        grid=(batch,),
        in_specs=[pl.BlockSpec((half, half), lambda b: (0, 0), **const),
                  pl.BlockSpec((half, half), lambda b: (0, 0), **const),
                  pl.BlockSpec((BF16_SUBLANES, half), lambda b: (0, 0)),
                  pl.BlockSpec((half, FOURIER_WIDTH), lambda b: (b, 0)),
                  pl.BlockSpec((half, FOURIER_WIDTH), lambda b: (b, 0)),
                  pl.BlockSpec((None, 8, FOURIER_WIDTH), lambda b: (b, 0, 0))],
        out_specs=pl.BlockSpec((seq, FOURIER_WIDTH), lambda b: (b, 0)),
        out_shape=jax.ShapeDtypeStruct((batch * seq, FOURIER_WIDTH), BF16),
        compiler_params=_params("parallel"),
        name="pos_dft",
    )(pos_cos, pos_sin, alt, ae, bo, a0)


FFN_CHUNKS = (1536, 1280)
TAIL_SUBTILES = 2


def _tail_kernel(h_ref, a_ref, f_ref, wf_ref, bf_ref, wo_ref, g1_ref, b1_ref, wgu_ref, wd_ref, g2_ref, b2_ref,
                 o_ref):
    sub = h_ref.shape[0] // TAIL_SUBTILES
    groups = [slice(s * sub, (s + 1) * sub) for s in range(TAIL_SUBTILES)]

    def mixed(rows):
        parts = []
        for gi in range(N_FGROUPS):
            cols = slice(gi * FGROUP_DIM, (gi + 1) * FGROUP_DIM)
            parts.append(jnp.dot(f_ref[rows, cols], wf_ref[gi], preferred_element_type=F32))
        y = jnp.concatenate(parts, axis=-1) + bf_ref[...]
        mix = jnp.dot(a_ref[rows, :], wo_ref[:ATT_WIDTH, :], preferred_element_type=F32)
        return mix + jnp.dot(y.astype(BF16), wo_ref[ATT_WIDTH:, :], preferred_element_type=F32)

    def swiglu(h1):
        hb = h1.astype(BF16)
        acc = None
        lo = 0
        for width in FFN_CHUNKS:
            gate = jnp.dot(hb, wgu_ref[:, lo:lo + width], preferred_element_type=F32)
            up = jnp.dot(hb, wgu_ref[:, D_FF + lo:D_FF + lo + width], preferred_element_type=F32)
            act = (gate * jax.nn.sigmoid(gate) * up).astype(BF16)
            part = jnp.dot(act, wd_ref[lo:lo + width, :], preferred_element_type=F32)
            acc = part if acc is None else acc + part
            lo += width
        return acc

    mixes = [mixed(rows) for rows in groups]
    h1s = [_layer_norm(ALPHA * h_ref[rows, :] + mix, g1_ref[...], b1_ref[...]) for rows, mix in zip(groups, mixes)]
    ffns = [swiglu(h1) for h1 in h1s]
    for rows, h1, ffn in zip(groups, h1s, ffns):
        o_ref[rows, :] = _layer_norm(ALPHA * h1 + ffn, g2_ref[...], b2_ref[...])


def _tail_call(h2, a, f, layer, wf_bf, b_f, wo_bf, g1, b1, wgu_bf, wd_bf, g2, b2, tm=512):
    m, d = h2.shape
    assert sum(FFN_CHUNKS) == D_FF
    const = dict(pipeline_mode=pl.Buffered(1))
    row = lambda i: (i, 0)
    fixed = lambda i: (0, 0)
    stacked = lambda i: (layer, 0, 0)
    return pl.pallas_call(
        _tail_kernel,
        grid=(m // tm,),
        in_specs=[pl.BlockSpec((tm, d), row),
                  pl.BlockSpec((tm, ATT_WIDTH), row),
                  pl.BlockSpec((tm, FOURIER_WIDTH), row),
                  pl.BlockSpec((None, N_FGROUPS, FGROUP_DIM, FGROUP_DIM), lambda i: (layer, 0, 0, 0), **const),
                  pl.BlockSpec((1, FOURIER_WIDTH), fixed),
                  pl.BlockSpec((None, d, d), stacked, **const),
                  pl.BlockSpec((1, d), fixed),
                  pl.BlockSpec((1, d), fixed),
                  pl.BlockSpec((None, d, 2 * D_FF), stacked, **const),
                  pl.BlockSpec((None, D_FF, d), stacked, **const),
                  pl.BlockSpec((1, d), fixed),
                  pl.BlockSpec((1, d), fixed)],
        out_specs=pl.BlockSpec((tm, d), row),
        out_shape=jax.ShapeDtypeStruct((m, d), F32),
        compiler_params=_params("parallel"),
        name="mix_ffn",
    )(h2, a, f, wf_bf, b_f.reshape(1, -1), wo_bf, g1.reshape(1, d), b1.reshape(1, d),
      wgu_bf, wd_bf, g2.reshape(1, d), b2.reshape(1, d))


def kernel(x, ln_in_g, ln_in_b, w_in, lam_params, subln_g, w_f, b_f, w_o, ln1_g, ln1_b, w_gu, w_down, ln2_g, ln2_b):
    batch, seq, d = x.shape
    assert d == D_MODEL and seq % ATT_TQ == 0
    chan_tab, pos_cos, pos_sin, pos_alt = _dft_tables(seq)
    w_in_bf, w_f_bf, w_o_bf = w_in.astype(BF16), w_f.astype(BF16), w_o.astype(BF16)
    w_gu_bf, w_down_bf = w_gu.astype(BF16), w_down.astype(BF16)
    h = x.reshape(batch * seq, d)
    for l in range(DEPTH):
        if l == 0:
            h, q, kp, vtc, nrm, u = _proj_call(h, w_in_bf, l, batch, seq, ln=(ln_in_g, ln_in_b))
        else:
            q, kp, vtc, nrm, u = _proj_call(h, w_in_bf, l, batch, seq)
        a = _attn_call(q, kp, vtc, nrm, lam_params[l], subln_g[l], batch, seq, _lambda_init(l))
        fe, fo, f0 = _fold_call(u, chan_tab, batch, seq)
        f = _pos_dft_call(pos_cos, pos_sin, pos_alt, fe, fo, f0, batch, seq)
        h = _tail_call(h, a, f, l, w_f_bf, b_f[l], w_o_bf, ln1_g[l], ln1_b[l],
                       w_gu_bf, w_down_bf, ln2_g[l], ln2_b[l])
    return h.reshape(batch, seq, d)
```

```python
import functools
import math

import jax
import jax.numpy as jnp
from jax import lax
from jax.experimental import pallas as pl
from jax.experimental.pallas import tpu as pltpu

D_MODEL = 1024
DEPTH = 2
ATT_WIDTH = 512
FOURIER_WIDTH = 512
N_HEADS = 8
HEAD_DIM = 32
V_HEAD_DIM = 64
N_FGROUPS = 4
FGROUP_DIM = 128
D_FF = 2816
LN_EPS = 1e-5
SUBLN_EPS = 1e-5
ALPHA = (2.0 * DEPTH) ** 0.25
LOG2E = math.log2(math.e)

LANES = 128
BF16_SUBLANES = 16
VMEM_LIMIT = 56 * 1024 * 1024

BF16 = jnp.bfloat16
F32 = jnp.float32


def _lambda_init(layer_idx):
    return 0.8 - 0.6 * math.exp(-0.3 * layer_idx)


def _layer_norm(x, g, b):
    mu = jnp.mean(x, axis=-1, keepdims=True)
    xc = x - mu
    var = jnp.mean(xc * xc, axis=-1, keepdims=True)
    return xc * lax.rsqrt(var + LN_EPS) * g + b


def _params(*sem):
    return pltpu.CompilerParams(dimension_semantics=sem, vmem_limit_bytes=VMEM_LIMIT)


def _split3(x):
    hi = x.astype(BF16).astype(F32)
    r = x - hi
    mid = r.astype(BF16).astype(F32)
    lo = (r - mid).astype(BF16).astype(F32)
    return hi, mid, lo


ATT_TQ = 256
ATT_RANGE = 8
HEADS_PER_STEP = LANES // V_HEAD_DIM
CHAINS_PER_STEP = 2 * HEADS_PER_STEP
VT_ROWS = 128
KP_WIDTH = 2 * LANES
NORM_ROUND_UP = 1.05
NORM_ROWS = 8
MAX_UNSHIFTED_SCORE = 50.0


def _slice_norms(x_bf):
    x32 = x_bf.astype(F32)
    seg = lax.broadcasted_iota(jnp.int32, (ATT_WIDTH, LANES), 0) // HEAD_DIM
    col = lax.broadcasted_iota(jnp.int32, (ATT_WIDTH, LANES), 1)
    ind = jnp.where(seg == col, 1.0, 0.0).astype(BF16)
    return jnp.dot((x32 * x32).astype(BF16), ind, preferred_element_type=F32) * NORM_ROUND_UP


def _row_max(x, groups=8):
    r, c = x.shape
    return jnp.max(jnp.max(x.reshape(groups, r // groups, c), axis=0), axis=0, keepdims=True)


def _proj_kernel(*refs, seq, tm, pre_ln):
    if pre_ln:
        x_ref, g_ref, b_ref, w_ref, h_ref, q_ref, kp_ref, vtc_ref, nrm_ref, u_ref = refs
        h = _layer_norm(x_ref[...], g_ref[...], b_ref[...])
        h_ref[...] = h
    else:
        x_ref, w_ref, q_ref, kp_ref, vtc_ref, nrm_ref, u_ref = refs
        h = x_ref[...]
    i = pl.program_id(0)
    hb = h.astype(BF16)
    qf = jnp.dot(hb, w_ref[:, 0:ATT_WIDTH], preferred_element_type=F32)
    qb = (qf * (HEAD_DIM ** -0.5 * LOG2E)).astype(BF16)
    q_ref[...] = qb

    kf = jnp.dot(hb, w_ref[:, ATT_WIDTH:2 * ATT_WIDTH], preferred_element_type=F32).astype(BF16)
    pos = (i * tm) % seq + lax.broadcasted_iota(jnp.int32, (tm, 1), 0)
    hi, mid, lo = _split3((pos - seq // 2).astype(F32) * LOG2E)
    lane = lax.broadcasted_iota(jnp.int32, (tm, LANES), 1)
    aux = jnp.where(lane < 3, 1.0,
                    jnp.where(lane == 3, hi, jnp.where(lane == 4, mid, jnp.where(lane == 5, lo, 0.0))))
    aux = aux.astype(BF16)
    for p in range(N_HEADS // HEADS_PER_STEP):
        kp_ref[:, p * KP_WIDTH:p * KP_WIDTH + LANES] = kf[:, p * LANES:(p + 1) * LANES]
        kp_ref[:, p * KP_WIDTH + LANES:(p + 1) * KP_WIDTH] = aux

    nrm_ref[0:NORM_ROWS, :] = jnp.broadcast_to(_row_max(_slice_norms(qb)), (NORM_ROWS, LANES))
    nrm_ref[NORM_ROWS:, :] = jnp.broadcast_to(_row_max(_slice_norms(kf)), (NORM_ROWS, LANES))

    vf = jnp.dot(hb, w_ref[:, 2 * ATT_WIDTH:3 * ATT_WIDTH], preferred_element_type=F32)
    vtr = vf.T.astype(BF16)
    ones = jnp.ones((VT_ROWS - V_HEAD_DIM, ATT_TQ), BF16)
    for hd in range(N_HEADS):
        v_rows = slice(hd * VT_ROWS, hd * VT_ROWS + V_HEAD_DIM)
        one_rows = slice(hd * VT_ROWS + V_HEAD_DIM, (hd + 1) * VT_ROWS)
        for cc in range(tm // ATT_TQ):
            vtc_ref[cc, v_rows, :] = vtr[hd * V_HEAD_DIM:(hd + 1) * V_HEAD_DIM, cc * ATT_TQ:(cc + 1) * ATT_TQ]
            vtc_ref[cc, one_rows, :] = ones

    u_ref[...] = jnp.dot(hb, w_ref[:, 3 * ATT_WIDTH:], preferred_element_type=F32).astype(BF16)


def _proj_call(x2, w_bf, layer, batch, seq, ln=None, tm=1024):
    m, d = x2.shape
    n = w_bf.shape[2]
    tiles_per_seq = seq // tm
    kern = functools.partial(_proj_kernel, seq=seq, tm=tm, pre_ln=ln is not None)
    row = lambda i: (i, 0)
    fixed = lambda i: (0, 0)
    in_specs = [pl.BlockSpec((tm, d), row), pl.BlockSpec((None, d, n), lambda i: (layer, 0, 0))]
    args = [x2, w_bf]
    out_specs = [pl.BlockSpec((tm, ATT_WIDTH), row),
                 pl.BlockSpec((tm, 4 * KP_WIDTH), row),
                 pl.BlockSpec((None, tm // ATT_TQ, N_HEADS * VT_ROWS, ATT_TQ),
                              lambda i: (i // tiles_per_seq, i % tiles_per_seq, 0, 0)),
                 pl.BlockSpec((None, 2 * NORM_ROWS, LANES), lambda i: (i, 0, 0)),
                 pl.BlockSpec((tm, FOURIER_WIDTH), row)]
    out_shape = [jax.ShapeDtypeStruct((m, ATT_WIDTH), BF16),
                 jax.ShapeDtypeStruct((m, 4 * KP_WIDTH), BF16),
                 jax.ShapeDtypeStruct((batch, seq // ATT_TQ, N_HEADS * VT_ROWS, ATT_TQ), BF16),
                 jax.ShapeDtypeStruct((m // tm, 2 * NORM_ROWS, LANES), F32),
                 jax.ShapeDtypeStruct((m, FOURIER_WIDTH), BF16)]
    if ln is not None:
        in_specs[1:1] = [pl.BlockSpec((1, d), fixed), pl.BlockSpec((1, d), fixed)]
        args[1:1] = [ln[0].reshape(1, d), ln[1].reshape(1, d)]
        out_specs.insert(0, pl.BlockSpec((tm, d), row))
        out_shape.insert(0, jax.ShapeDtypeStruct((m, d), F32))
    return pl.pallas_call(
        kern,
        grid=(m // tm,),
        in_specs=in_specs,
        out_specs=out_specs,
        out_shape=out_shape,
        compiler_params=_params("parallel"),
        name="proj",
    )(*args)


def _attn_kernel(lam_ref, g_ref, nrm_ref, q_ref, kp_ref, vtc_ref, o_ref, ks_ref, flag_ref, *, lam_init, seq):
    pair = pl.program_id(1)
    ntiles = seq // ATT_TQ
    last = ntiles - 1
    last_rows = slice(last * ATT_TQ, seq)

    def tile_rows(tile):
        return pl.ds(pl.multiple_of(tile * ATT_TQ, ATT_TQ), ATT_TQ)

    def enter_tile(tile):
        prev = tile_rows(jnp.maximum(tile - 1, 0))
        ks_ref[prev, :] = kp_ref[prev, :]
        ks_ref[tile_rows(tile), :LANES] = kp_ref[last_rows, :LANES]
        ks_ref[tile_rows(tile), LANES:] = -kp_ref[last_rows, LANES:]

    ks_ref[:, :LANES] = kp_ref[:, :LANES]
    ks_ref[:, LANES:] = -kp_ref[:, LANES:]
    nrm = jnp.max(nrm_ref[...], axis=0)
    lane = lax.broadcasted_iota(jnp.int32, (1, LANES), 1)
    prod = jnp.where(lane // CHAINS_PER_STEP == pair, nrm[0:1, :] * nrm[NORM_ROWS:NORM_ROWS + 1, :], 0.0)
    flag_ref[0] = (jnp.max(prod) <= MAX_UNSHIFTED_SCORE ** 2).astype(jnp.int32)

    lp = lam_ref[...]
    lam = (jnp.exp(jnp.sum(lp[0:1] * lp[1:2], axis=-1, keepdims=True))
           - jnp.exp(jnp.sum(lp[2:3] * lp[3:4], axis=-1, keepdims=True)) + lam_init)

    rr = lax.broadcasted_iota(jnp.int32, (ATT_TQ, ATT_TQ), 0)
    cc = lax.broadcasted_iota(jnp.int32, (ATT_TQ, ATT_TQ), 1)
    diag_dist = jnp.abs(rr - cc).astype(F32) * LOG2E
    diag_dist = jnp.concatenate([diag_dist, diag_dist], axis=1)
    row_q = lax.broadcasted_iota(jnp.int32, (LANES, ATT_TQ), 0)
    zeros_t = jnp.zeros((LANES, 2 * ATT_TQ), BF16)
    g_col = g_ref[...]

    def tile_operands(tile):
        q_t = q_ref[tile_rows(tile), :].astype(F32).T
        pos = tile * ATT_TQ + lax.broadcasted_iota(jnp.int32, (1, ATT_TQ), 1)
        hi, mid, lo = _split3((pos - seq // 2).astype(F32) * LOG2E)
        base = jnp.where(row_q == 0, -hi,
                         jnp.where(row_q == 1, -mid,
                                   jnp.where(row_q == 2, -lo, jnp.where(row_q < 6, 1.0, 0.0))))
        return q_t, base

    def stage_operands(tile, hh, q_t, base):
        head = pair * HEADS_PER_STEP + hh
        slope = jnp.exp2(-jnp.full((1, 1), head + 1, jnp.int32).astype(F32))
        aux = (base * slope).astype(BF16)
        qm = jnp.concatenate([jnp.where(row_q // HEAD_DIM == hh * 2 + c, q_t, 0.0).astype(BF16) for c in range(2)],
                             axis=1)
        w_main = jnp.concatenate([qm, jnp.concatenate([aux, aux], axis=1)], axis=0)
        w_diag = jnp.concatenate([qm, zeros_t], axis=0)
        sd = jnp.dot(kp_ref[tile_rows(tile), :], w_diag, preferred_element_type=F32)
        sd = sd - slope * diag_dist
        return w_main, sd

    def attend(tile, hh, q_t, base, shifted):
        w_main, sd = stage_operands(tile, hh, q_t, base)
        head_rows = slice(hh * VT_ROWS, (hh + 1) * VT_ROWS)

        def scores(lo, hi):
            return jnp.dot(ks_ref[lo * ATT_TQ:hi * ATT_TQ, :], w_main, preferred_element_type=F32)

        def weighted_values(acc, lo, st):
            p = jnp.exp2(st).astype(BF16)
            for j in range(st.shape[0] // ATT_TQ):
                blk = lo + j
                chunk = jnp.where(tile == blk, last, blk)
                acc = acc + jnp.dot(vtc_ref[chunk, head_rows, :], p[j * ATT_TQ:(j + 1) * ATT_TQ, :],
                                    preferred_element_type=F32)
            return acc

        ranges = [(lo, min(lo + ATT_RANGE, last)) for lo in range(0, last, ATT_RANGE)]
        if shifted:
            sts = [scores(lo, hi) for lo, hi in ranges]
            m = jnp.max(sd, axis=0, keepdims=True)
            for st in sts:
                m = jnp.maximum(m, jnp.max(st, axis=0, keepdims=True))
            acc = jnp.dot(vtc_ref[tile, head_rows, :], jnp.exp2(sd - m).astype(BF16), preferred_element_type=F32)
            for (lo, _), st in zip(ranges, sts):
                acc = weighted_values(acc, lo, st - m)
            return acc
        acc = jnp.dot(vtc_ref[tile, head_rows, :], jnp.exp2(sd).astype(BF16), preferred_element_type=F32)
        pending = scores(*ranges[0])
        for r, (lo, _) in enumerate(ranges):
            following = scores(*ranges[r + 1]) if r + 1 < len(ranges) else None
            acc = weighted_values(acc, lo, pending)
            pending = following
        return acc

    def finish(tile, accs):
        outs = []
        for acc in accs:
            norm = []
            for c in range(2):
                part = acc[:, c * ATT_TQ:(c + 1) * ATT_TQ]
                denom = part[V_HEAD_DIM:V_HEAD_DIM + 1, :]
                norm.append(part[:V_HEAD_DIM, :] * (1.0 / denom))
            a_t = norm[0] - lam * norm[1]
            a_t = a_t * lax.rsqrt(jnp.mean(a_t * a_t, axis=0, keepdims=True) + SUBLN_EPS)
            outs.append(a_t * g_col * (1.0 - lam_init))
        o_ref[tile_rows(tile), :] = jnp.concatenate(outs, axis=0).T.astype(o_ref.dtype)

    def run(shifted):
        def body(tile, accs):
            finish(jnp.maximum(tile - 1, 0), accs)
            enter_tile(tile)
            q_t, base = tile_operands(tile)
            return tuple(attend(tile, hh, q_t, base, shifted) for hh in range(HEADS_PER_STEP))
        dummy = tuple(jnp.ones((VT_ROWS, 2 * ATT_TQ), F32) for _ in range(HEADS_PER_STEP))
        finish(last, lax.fori_loop(0, ntiles, body, dummy, unroll=1 if shifted else 2))

    @pl.when(flag_ref[0] == 1)
    def _():
        run(shifted=False)

    @pl.when(flag_ref[0] != 1)
    def _():
        run(shifted=True)


def _attn_call(q, kp, vtc, nrm, lam_params, subln_g, batch, seq, lam_init):
    n_pairs = N_HEADS // HEADS_PER_STEP
    tiles_per_seq = nrm.shape[0] // batch
    kern = functools.partial(_attn_kernel, lam_init=lam_init, seq=seq)
    pair_rows = HEADS_PER_STEP * VT_ROWS
    return pl.pallas_call(
        kern,
        grid=(batch, n_pairs),
        in_specs=[pl.BlockSpec((4, HEAD_DIM), lambda b, p: (0, 0)),
                  pl.BlockSpec((V_HEAD_DIM, 1), lambda b, p: (0, 0)),
                  pl.BlockSpec((tiles_per_seq, 2 * NORM_ROWS, LANES), lambda b, p: (b, 0, 0)),
                  pl.BlockSpec((seq, LANES), lambda b, p: (b, p)),
                  pl.BlockSpec((seq, KP_WIDTH), lambda b, p: (b, p)),
                  pl.BlockSpec((None, seq // ATT_TQ, pair_rows, ATT_TQ), lambda b, p: (b, 0, p, 0))],
        out_specs=pl.BlockSpec((seq, LANES), lambda b, p: (b, p)),
        out_shape=jax.ShapeDtypeStruct((batch * seq, ATT_WIDTH), BF16),
        scratch_shapes=[pltpu.VMEM((seq, KP_WIDTH), BF16), pltpu.SMEM((1,), jnp.int32)],
        compiler_params=_params("parallel", "parallel"),
        name="diff_attn",
    )(lam_params, subln_g.reshape(V_HEAD_DIM, 1), nrm, q, kp, vtc)


FOLD_BLK = 256
TABLE_SPLIT = 64


def _dft_tables(seq):
    c = FGROUP_DIM
    ci = lax.broadcasted_iota(jnp.int32, (c, c), 0) * lax.broadcasted_iota(jnp.int32, (c, c), 1) % c
    ang_c = ci.astype(F32) * (2.0 * math.pi / c)
    cc, sc = jnp.cos(ang_c) * c ** -0.5, jnp.sin(ang_c) * c ** -0.5
    chan = jnp.concatenate([jnp.concatenate([cc, sc], axis=1), jnp.concatenate([cc, -sc], axis=1)], axis=0)
    half = seq // 2
    n = lax.broadcasted_iota(jnp.int32, (1, half), 1) + 1
    weight = jnp.where(n == half, 0.5, 1.0) * seq ** -0.5

    def thin(count, stride):
        kk = lax.broadcasted_iota(jnp.int32, (count, half), 0) * stride
        ang = (kk * n % seq).astype(F32) * (2.0 * math.pi / seq)
        return jnp.cos(ang), jnp.sin(ang)

    ca, sa = thin(half // TABLE_SPLIT, TABLE_SPLIT)
    cb, sb = thin(TABLE_SPLIT, 1)
    cb, sb = cb * weight, sb * weight
    pos_cos = (ca[:, None, :] * cb[None] - sa[:, None, :] * sb[None]).reshape(half, half).astype(BF16)
    pos_sin = (sa[:, None, :] * cb[None] + ca[:, None, :] * sb[None]).reshape(half, half).astype(BF16)
    n1 = lax.broadcasted_iota(jnp.int32, (BF16_SUBLANES, half), 1) + 1
    row = lax.broadcasted_iota(jnp.int32, (BF16_SUBLANES, half), 0)
    alt = jnp.where(row == 0, jnp.where(n1 % 2 == 0, 1.0, -1.0), 0.0)
    alt = (alt * jnp.where(n1 == half, 0.5, 1.0) * seq ** -0.5).astype(BF16)
    return chan.astype(BF16), pos_cos, pos_sin, alt


def _select(rows, cols, offset):
    r = lax.broadcasted_iota(jnp.int32, (rows, cols), 0)
    c = lax.broadcasted_iota(jnp.int32, (rows, cols), 1)
    return jnp.where(c == offset - r, 1.0, 0.0).astype(BF16)


def _fold_kernel(u_ref, t_ref, ae_ref, bo_ref, a0_ref, *, seq):
    half = seq // 2
    t = t_ref[...]
    r = lax.broadcasted_iota(jnp.int32, (FOLD_BLK, 2 * FOLD_BLK), 0)
    c = lax.broadcasted_iota(jnp.int32, (FOLD_BLK, 2 * FOLD_BLK), 1)
    shift_one = jnp.where(c == r + 1, 1.0, 0.0).astype(BF16)
    reverse = _select(FOLD_BLK, FOLD_BLK, FOLD_BLK - 1)
    for i in range(half // FOLD_BLK):
        lo = i * FOLD_BLK
        fwd = jnp.dot(shift_one, u_ref[lo:lo + 2 * FOLD_BLK, :], preferred_element_type=F32).astype(BF16)
        hi = seq - lo - FOLD_BLK
        bwd = jnp.dot(reverse, u_ref[hi:hi + FOLD_BLK, :], preferred_element_type=F32).astype(BF16)
        for gi in range(N_FGROUPS):
            cols = slice(gi * FGROUP_DIM, (gi + 1) * FGROUP_DIM)
            both = jnp.concatenate([fwd[:, cols], bwd[:, cols]], axis=1)
            res = jnp.dot(both, t, preferred_element_type=F32)
            ae_ref[lo:lo + FOLD_BLK, cols] = res[:, :FGROUP_DIM].astype(ae_ref.dtype)
            bo_ref[lo:lo + FOLD_BLK, cols] = res[:, FGROUP_DIM:].astype(bo_ref.dtype)
    for gi in range(N_FGROUPS):
        cols = slice(gi * FGROUP_DIM, (gi + 1) * FGROUP_DIM)
        first = jnp.dot(u_ref[0:BF16_SUBLANES, cols], t[:FGROUP_DIM, :FGROUP_DIM], preferred_element_type=F32)
        a0_ref[:, cols] = jnp.broadcast_to(first[0:1, :], (a0_ref.shape[0], FGROUP_DIM))


def _fold_call(u, chan_tab, batch, seq):
    half = seq // 2
    out = jax.ShapeDtypeStruct((batch * half, FOURIER_WIDTH), BF16)
    return pl.pallas_call(
        functools.partial(_fold_kernel, seq=seq),
        grid=(batch,),
        in_specs=[pl.BlockSpec((seq, FOURIER_WIDTH), lambda b: (b, 0)),
                  pl.BlockSpec((2 * FGROUP_DIM, 2 * FGROUP_DIM), lambda b: (0, 0))],
        out_specs=[pl.BlockSpec((half, FOURIER_WIDTH), lambda b: (b, 0)),
                   pl.BlockSpec((half, FOURIER_WIDTH), lambda b: (b, 0)),
                   pl.BlockSpec((None, 8, FOURIER_WIDTH), lambda b: (b, 0, 0))],
        out_shape=[out, out, jax.ShapeDtypeStruct((batch, 8, FOURIER_WIDTH), F32)],
        compiler_params=_params("parallel"),
        name="dft_fold",
    )(u, chan_tab)


def _pos_dft_kernel(c_ref, s_ref, alt_ref, ae_ref, bo_ref, a0_ref, o_ref, *, seq):
    half = seq // 2
    ae = ae_ref[...]
    dc = a0_ref[0:1, :] * seq ** -0.5
    even = jnp.dot(c_ref[...], ae, preferred_element_type=F32) + dc
    odd = jnp.dot(s_ref[...], bo_ref[...], preferred_element_type=F32)
    o_ref[0:half, :] = (even - odd).astype(o_ref.dtype)
    mirror = (even + odd).astype(BF16)
    middle = jnp.dot(alt_ref[...], ae, preferred_element_type=F32)[0:1, :] + dc
    middle = jnp.broadcast_to(middle, (FOLD_BLK, FOURIER_WIDTH)).astype(BF16)
    take = _select(FOLD_BLK, 2 * FOLD_BLK, FOLD_BLK)
    nblk = half // FOLD_BLK
    for bt in range(nblk):
        upper = middle if bt == 0 else mirror[(nblk - bt) * FOLD_BLK:(nblk - bt + 1) * FOLD_BLK, :]
        lower = mirror[(nblk - 1 - bt) * FOLD_BLK:(nblk - bt) * FOLD_BLK, :]
        blk = jnp.dot(take, jnp.concatenate([lower, upper], axis=0), preferred_element_type=F32)
        o_ref[half + bt * FOLD_BLK:half + (bt + 1) * FOLD_BLK, :] = blk.astype(o_ref.dtype)


def _pos_dft_call(pos_cos, pos_sin, alt, ae, bo, a0, batch, seq):
    half = seq // 2
    const = dict(pipeline_mode=pl.Buffered(1))
    return pl.pallas_call(
        functools.partial(_pos_dft_kernel, seq=seq),
        grid=(batch,),
        in_specs=[pl.BlockSpec((half, half), lambda b: (0, 0), **const),
                  pl.BlockSpec((half, half), lambda b: (0, 0), **const),
                  pl.BlockSpec((BF16_SUBLANES, half), lambda b: (0, 0)),
                  pl.BlockSpec((half, FOURIER_WIDTH), lambda b: (b, 0)),
                  pl.BlockSpec((half, FOURIER_WIDTH), lambda b: (b, 0)),
                  pl.BlockSpec((None, 8, FOURIER_WIDTH), lambda b: (b, 0, 0))],
        out_specs=pl.BlockSpec((seq, FOURIER_WIDTH), lambda b: (b, 0)),
        out_shape=jax.ShapeDtypeStruct((batch * seq, FOURIER_WIDTH), BF16),
        compiler_params=_params("parallel"),
        name="pos_dft",
    )(pos_cos, pos_sin, alt, ae, bo, a0)


FFN_CHUNKS = (1536, 1280)
TAIL_SUBTILES = 2


def _tail_kernel(h_ref, a_ref, f_ref, wf_ref, bf_ref, wo_ref, g1_ref, b1_ref, wgu_ref, wd_ref, g2_ref, b2_ref,
                 o_ref):
    sub = h_ref.shape[0] // TAIL_SUBTILES
    groups = [slice(s * sub, (s + 1) * sub) for s in range(TAIL_SUBTILES)]

    def mixed(rows):
        parts = []
        for gi in range(N_FGROUPS):
            cols = slice(gi * FGROUP_DIM, (gi + 1) * FGROUP_DIM)
            parts.append(jnp.dot(f_ref[rows, cols], wf_ref[gi], preferred_element_type=F32))
        y = jnp.concatenate(parts, axis=-1) + bf_ref[...]
        mix = jnp.dot(a_ref[rows, :], wo_ref[:ATT_WIDTH, :], preferred_element_type=F32)
        return mix + jnp.dot(y.astype(BF16), wo_ref[ATT_WIDTH:, :], preferred_element_type=F32)

    def swiglu(h1):
        hb = h1.astype(BF16)
        acc = None
        lo = 0
        for width in FFN_CHUNKS:
            gate = jnp.dot(hb, wgu_ref[:, lo:lo + width], preferred_element_type=F32)
            up = jnp.dot(hb, wgu_ref[:, D_FF + lo:D_FF + lo + width], preferred_element_type=F32)
            act = (gate * jax.nn.sigmoid(gate) * up).astype(BF16)
            part = jnp.dot(act, wd_ref[lo:lo + width, :], preferred_element_type=F32)
            acc = part if acc is None else acc + part
            lo += width
        return acc

    mixes = [mixed(rows) for rows in groups]
    h1s = [_layer_norm(ALPHA * h_ref[rows, :] + mix, g1_ref[...], b1_ref[...]) for rows, mix in zip(groups, mixes)]
    ffns = [swiglu(h1) for h1 in h1s]
    for rows, h1, ffn in zip(groups, h1s, ffns):
        o_ref[rows, :] = _layer_norm(ALPHA * h1 + ffn, g2_ref[...], b2_ref[...])


def _tail_call(h2, a, f, layer, wf_bf, b_f, wo_bf, g1, b1, wgu_bf, wd_bf, g2, b2, tm=1024):
    m, d = h2.shape
    assert sum(FFN_CHUNKS) == D_FF
    const = dict(pipeline_mode=pl.Buffered(1))
    row = lambda i: (i, 0)
    fixed = lambda i: (0, 0)
    stacked = lambda i: (layer, 0, 0)
    return pl.pallas_call(
        _tail_kernel,
        grid=(m // tm,),
        in_specs=[pl.BlockSpec((tm, d), row),
                  pl.BlockSpec((tm, ATT_WIDTH), row),
                  pl.BlockSpec((tm, FOURIER_WIDTH), row),
                  pl.BlockSpec((None, N_FGROUPS, FGROUP_DIM, FGROUP_DIM), lambda i: (layer, 0, 0, 0), **const),
                  pl.BlockSpec((1, FOURIER_WIDTH), fixed),
                  pl.BlockSpec((None, d, d), stacked, **const),
                  pl.BlockSpec((1, d), fixed),
                  pl.BlockSpec((1, d), fixed),
                  pl.BlockSpec((None, d, 2 * D_FF), stacked, **const),
                  pl.BlockSpec((None, D_FF, d), stacked, **const),
                  pl.BlockSpec((1, d), fixed),
                  pl.BlockSpec((1, d), fixed)],
        out_specs=pl.BlockSpec((tm, d), row),
        out_shape=jax.ShapeDtypeStruct((m, d), F32),
        compiler_params=_params("parallel"),
        name="mix_ffn",
    )(h2, a, f, wf_bf, b_f.reshape(1, -1), wo_bf, g1.reshape(1, d), b1.reshape(1, d),
      wgu_bf, wd_bf, g2.reshape(1, d), b2.reshape(1, d))


def kernel(x, ln_in_g, ln_in_b, w_in, lam_params, subln_g, w_f, b_f, w_o, ln1_g, ln1_b, w_gu, w_down, ln2_g, ln2_b):
    batch, seq, d = x.shape
    assert d == D_MODEL and seq % ATT_TQ == 0
    chan_tab, pos_cos, pos_sin, pos_alt = _dft_tables(seq)
    w_in_bf, w_f_bf, w_o_bf = w_in.astype(BF16), w_f.astype(BF16), w_o.astype(BF16)
    w_gu_bf, w_down_bf = w_gu.astype(BF16), w_down.astype(BF16)
    h = x.reshape(batch * seq, d)
    for l in range(DEPTH):
        if l == 0:
            h, q, kp, vtc, nrm, u = _proj_call(h, w_in_bf, l, batch, seq, ln=(ln_in_g, ln_in_b))
        else:
            q, kp, vtc, nrm, u = _proj_call(h, w_in_bf, l, batch, seq)
        a = _attn_call(q, kp, vtc, nrm, lam_params[l], subln_g[l], batch, seq, _lambda_init(l))
        fe, fo, f0 = _fold_call(u, chan_tab, batch, seq)
        f = _pos_dft_call(pos_cos, pos_sin, pos_alt, fe, fo, f0, batch, seq)
        h = _tail_call(h, a, f, l, w_f_bf, b_f[l], w_o_bf, ln1_g[l], ln1_b[l],
                       w_gu_bf, w_down_bf, ln2_g[l], ln2_b[l])
    return h.reshape(batch, seq, d)
```

```python
import functools
import math

import jax
import jax.numpy as jnp
from jax import lax
from jax.experimental import pallas as pl
from jax.experimental.pallas import tpu as pltpu

D_MODEL = 1024
DEPTH = 2
ATT_WIDTH = 512
FOURIER_WIDTH = 512
N_HEADS = 8
HEAD_DIM = 32
V_HEAD_DIM = 64
N_FGROUPS = 4
FGROUP_DIM = 128
D_FF = 2816
LN_EPS = 1e-5
SUBLN_EPS = 1e-5
ALPHA = (2.0 * DEPTH) ** 0.25
LOG2E = math.log2(math.e)

LANES = 128
BF16_SUBLANES = 16
VMEM_LIMIT = 56 * 1024 * 1024

BF16 = jnp.bfloat16
F32 = jnp.float32


def _lambda_init(layer_idx):
    return 0.8 - 0.6 * math.exp(-0.3 * layer_idx)


def _layer_norm(x, g, b):
    mu = jnp.mean(x, axis=-1, keepdims=True)
    xc = x - mu
    var = jnp.mean(xc * xc, axis=-1, keepdims=True)
    return xc * lax.rsqrt(var + LN_EPS) * g + b


def _params(*sem):
    return pltpu.CompilerParams(dimension_semantics=sem, vmem_limit_bytes=VMEM_LIMIT)


def _split3(x):
    hi = x.astype(BF16).astype(F32)
    r = x - hi
    mid = r.astype(BF16).astype(F32)
    lo = (r - mid).astype(BF16).astype(F32)
    return hi, mid, lo


ATT_TQ = 256
ATT_RANGE = 8
HEADS_PER_STEP = LANES // V_HEAD_DIM
CHAINS_PER_STEP = 2 * HEADS_PER_STEP
VT_ROWS = 128
KP_WIDTH = 2 * LANES
NORM_ROUND_UP = 1.05
NORM_ROWS = 8
MAX_UNSHIFTED_SCORE = 50.0


def _slice_norms(x_bf):
    x32 = x_bf.astype(F32)
    seg = lax.broadcasted_iota(jnp.int32, (ATT_WIDTH, LANES), 0) // HEAD_DIM
    col = lax.broadcasted_iota(jnp.int32, (ATT_WIDTH, LANES), 1)
    ind = jnp.where(seg == col, 1.0, 0.0).astype(BF16)
    return jnp.dot((x32 * x32).astype(BF16), ind, preferred_element_type=F32) * NORM_ROUND_UP


def _row_max(x, groups=8):
    r, c = x.shape
    return jnp.max(jnp.max(x.reshape(groups, r // groups, c), axis=0), axis=0, keepdims=True)


def _proj_kernel(*refs, seq, tm, pre_ln):
    if pre_ln:
        x_ref, g_ref, b_ref, w_ref, h_ref, q_ref, kp_ref, vtc_ref, nrm_ref, u_ref = refs
        h = _layer_norm(x_ref[...], g_ref[...], b_ref[...])
        h_ref[...] = h
    else:
        x_ref, w_ref, q_ref, kp_ref, vtc_ref, nrm_ref, u_ref = refs
        h = x_ref[...]
    i = pl.program_id(0)
    hb = h.astype(BF16)
    qf = jnp.dot(hb, w_ref[:, 0:ATT_WIDTH], preferred_element_type=F32)
    qb = (qf * (HEAD_DIM ** -0.5 * LOG2E)).astype(BF16)
    q_ref[...] = qb

    kf = jnp.dot(hb, w_ref[:, ATT_WIDTH:2 * ATT_WIDTH], preferred_element_type=F32).astype(BF16)
    pos = (i * tm) % seq + lax.broadcasted_iota(jnp.int32, (tm, 1), 0)
    hi, mid, lo = _split3((pos - seq // 2).astype(F32) * LOG2E)
    lane = lax.broadcasted_iota(jnp.int32, (tm, LANES), 1)
    aux = jnp.where(lane < 3, 1.0,
                    jnp.where(lane == 3, hi, jnp.where(lane == 4, mid, jnp.where(lane == 5, lo, 0.0))))
    aux = aux.astype(BF16)
    for p in range(N_HEADS // HEADS_PER_STEP):
        kp_ref[:, p * KP_WIDTH:p * KP_WIDTH + LANES] = kf[:, p * LANES:(p + 1) * LANES]
        kp_ref[:, p * KP_WIDTH + LANES:(p + 1) * KP_WIDTH] = aux

    nrm_ref[0:NORM_ROWS, :] = jnp.broadcast_to(_row_max(_slice_norms(qb)), (NORM_ROWS, LANES))
    nrm_ref[NORM_ROWS:, :] = jnp.broadcast_to(_row_max(_slice_norms(kf)), (NORM_ROWS, LANES))

    vf = jnp.dot(hb, w_ref[:, 2 * ATT_WIDTH:3 * ATT_WIDTH], preferred_element_type=F32)
    vtr = vf.T.astype(BF16)
    ones = jnp.ones((VT_ROWS - V_HEAD_DIM, ATT_TQ), BF16)
    for hd in range(N_HEADS):
        v_rows = slice(hd * VT_ROWS, hd * VT_ROWS + V_HEAD_DIM)
        one_rows = slice(hd * VT_ROWS + V_HEAD_DIM, (hd + 1) * VT_ROWS)
        for cc in range(tm // ATT_TQ):
            vtc_ref[cc, v_rows, :] = vtr[hd * V_HEAD_DIM:(hd + 1) * V_HEAD_DIM, cc * ATT_TQ:(cc + 1) * ATT_TQ]
            vtc_ref[cc, one_rows, :] = ones

    u_ref[...] = jnp.dot(hb, w_ref[:, 3 * ATT_WIDTH:], preferred_element_type=F32).astype(BF16)


def _proj_call(x2, w_bf, layer, batch, seq, ln=None, tm=1024):
    m, d = x2.shape
    n = w_bf.shape[2]
    tiles_per_seq = seq // tm
    kern = functools.partial(_proj_kernel, seq=seq, tm=tm, pre_ln=ln is not None)
    row = lambda i: (i, 0)
    fixed = lambda i: (0, 0)
    in_specs = [pl.BlockSpec((tm, d), row), pl.BlockSpec((None, d, n), lambda i: (layer, 0, 0))]
    args = [x2, w_bf]
    out_specs = [pl.BlockSpec((tm, ATT_WIDTH), row),
                 pl.BlockSpec((tm, 4 * KP_WIDTH), row),
                 pl.BlockSpec((None, tm // ATT_TQ, N_HEADS * VT_ROWS, ATT_TQ),
                              lambda i: (i // tiles_per_seq, i % tiles_per_seq, 0, 0)),
                 pl.BlockSpec((None, 2 * NORM_ROWS, LANES), lambda i: (i, 0, 0)),
                 pl.BlockSpec((tm, FOURIER_WIDTH), row)]
    out_shape = [jax.ShapeDtypeStruct((m, ATT_WIDTH), BF16),
                 jax.ShapeDtypeStruct((m, 4 * KP_WIDTH), BF16),
                 jax.ShapeDtypeStruct((batch, seq // ATT_TQ, N_HEADS * VT_ROWS, ATT_TQ), BF16),
                 jax.ShapeDtypeStruct((m // tm, 2 * NORM_ROWS, LANES), F32),
                 jax.ShapeDtypeStruct((m, FOURIER_WIDTH), BF16)]
    if ln is not None:
        in_specs[1:1] = [pl.BlockSpec((1, d), fixed), pl.BlockSpec((1, d), fixed)]
        args[1:1] = [ln[0].reshape(1, d), ln[1].reshape(1, d)]
        out_specs.insert(0, pl.BlockSpec((tm, d), row))
        out_shape.insert(0, jax.ShapeDtypeStruct((m, d), F32))
    return pl.pallas_call(
        kern,
        grid=(m // tm,),
        in_specs=in_specs,
        out_specs=out_specs,
        out_shape=out_shape,
        compiler_params=_params("parallel"),
        name="proj",
    )(*args)


def _attn_kernel(lam_ref, g_ref, nrm_ref, q_ref, kp_ref, vtc_ref, o_ref, ks_ref, flag_ref, *, lam_init, seq):
    pair = pl.program_id(1)
    ntiles = seq // ATT_TQ
    last = ntiles - 1
    last_rows = slice(last * ATT_TQ, seq)

    def tile_rows(tile):
        return pl.ds(pl.multiple_of(tile * ATT_TQ, ATT_TQ), ATT_TQ)

    def enter_tile(tile):
        prev = tile_rows(jnp.maximum(tile - 1, 0))
        ks_ref[prev, :] = kp_ref[prev, :]
        ks_ref[tile_rows(tile), :LANES] = kp_ref[last_rows, :LANES]
        ks_ref[tile_rows(tile), LANES:] = -kp_ref[last_rows, LANES:]

    ks_ref[:, :LANES] = kp_ref[:, :LANES]
    ks_ref[:, LANES:] = -kp_ref[:, LANES:]
    nrm = jnp.max(nrm_ref[...], axis=0)
    lane = lax.broadcasted_iota(jnp.int32, (1, LANES), 1)
    prod = jnp.where(lane // CHAINS_PER_STEP == pair, nrm[0:1, :] * nrm[NORM_ROWS:NORM_ROWS + 1, :], 0.0)
    flag_ref[0] = (jnp.max(prod) <= MAX_UNSHIFTED_SCORE ** 2).astype(jnp.int32)

    lp = lam_ref[...]
    lam = (jnp.exp(jnp.sum(lp[0:1] * lp[1:2], axis=-1, keepdims=True))
           - jnp.exp(jnp.sum(lp[2:3] * lp[3:4], axis=-1, keepdims=True)) + lam_init)

    rr = lax.broadcasted_iota(jnp.int32, (ATT_TQ, ATT_TQ), 0)
    cc = lax.broadcasted_iota(jnp.int32, (ATT_TQ, ATT_TQ), 1)
    diag_dist = jnp.abs(rr - cc).astype(F32) * LOG2E
    diag_dist = jnp.concatenate([diag_dist, diag_dist], axis=1)
    row_q = lax.broadcasted_iota(jnp.int32, (LANES, ATT_TQ), 0)
    zeros_t = jnp.zeros((LANES, 2 * ATT_TQ), BF16)
    g_col = g_ref[...]

    def tile_operands(tile):
        q_t = q_ref[tile_rows(tile), :].astype(F32).T
        pos = tile * ATT_TQ + lax.broadcasted_iota(jnp.int32, (1, ATT_TQ), 1)
        hi, mid, lo = _split3((pos - seq // 2).astype(F32) * LOG2E)
        base = jnp.where(row_q == 0, -hi,
                         jnp.where(row_q == 1, -mid,
                                   jnp.where(row_q == 2, -lo, jnp.where(row_q < 6, 1.0, 0.0))))
        return q_t, base

    def stage_operands(tile, hh, q_t, base):
        head = pair * HEADS_PER_STEP + hh
        slope = jnp.exp2(-jnp.full((1, 1), head + 1, jnp.int32).astype(F32))
        aux = (base * slope).astype(BF16)
        qm = jnp.concatenate([jnp.where(row_q // HEAD_DIM == hh * 2 + c, q_t, 0.0).astype(BF16) for c in range(2)],
                             axis=1)
        w_main = jnp.concatenate([qm, jnp.concatenate([aux, aux], axis=1)], axis=0)
        w_diag = jnp.concatenate([qm, zeros_t], axis=0)
        sd = jnp.dot(kp_ref[tile_rows(tile), :], w_diag, preferred_element_type=F32)
        sd = sd - slope * diag_dist
        return w_main, sd

    key_ranges = [(lo, min(lo + ATT_RANGE, last)) for lo in range(0, last, ATT_RANGE)]

    def attend(tile, q_t, base, shifted):
        ops = [stage_operands(tile, hh, q_t, base) for hh in range(HEADS_PER_STEP)]

        def scores(hh, lo, hi):
            return jnp.dot(ks_ref[lo * ATT_TQ:hi * ATT_TQ, :], ops[hh][0], preferred_element_type=F32)

        def weighted_values(hh, sd, sts):
            head_rows = slice(hh * VT_ROWS, (hh + 1) * VT_ROWS)
            acc = jnp.dot(vtc_ref[tile, head_rows, :], jnp.exp2(sd).astype(BF16), preferred_element_type=F32)
            for (lo, _), st in zip(key_ranges, sts):
                p = jnp.exp2(st).astype(BF16)
                for j in range(st.shape[0] // ATT_TQ):
                    blk = lo + j
                    chunk = jnp.where(tile == blk, last, blk)
                    acc = acc + jnp.dot(vtc_ref[chunk, head_rows, :], p[j * ATT_TQ:(j + 1) * ATT_TQ, :],
                                        preferred_element_type=F32)
            return acc

        if shifted:
            accs = []
            for hh in range(HEADS_PER_STEP):
                sd = ops[hh][1]
                sts = [scores(hh, lo, hi) for lo, hi in key_ranges]
                m = jnp.max(sd, axis=0, keepdims=True)
                for st in sts:
                    m = jnp.maximum(m, jnp.max(st, axis=0, keepdims=True))
                accs.append(weighted_values(hh, sd - m, [st - m for st in sts]))
            return tuple(accs)
        all_sts = [[scores(hh, lo, hi) for lo, hi in key_ranges] for hh in range(HEADS_PER_STEP)]
        return tuple(weighted_values(hh, ops[hh][1], all_sts[hh]) for hh in range(HEADS_PER_STEP))

    def finish(tile, accs):
        outs = []
        for acc in accs:
            norm = []
            for c in range(2):
                part = acc[:, c * ATT_TQ:(c + 1) * ATT_TQ]
                denom = part[V_HEAD_DIM:V_HEAD_DIM + 1, :]
                norm.append(part[:V_HEAD_DIM, :] * (1.0 / denom))
            a_t = norm[0] - lam * norm[1]
            a_t = a_t * lax.rsqrt(jnp.mean(a_t * a_t, axis=0, keepdims=True) + SUBLN_EPS)
            outs.append(a_t * g_col * (1.0 - lam_init))
        o_ref[tile_rows(tile), :] = jnp.concatenate(outs, axis=0).T.astype(o_ref.dtype)

    def run(shifted):
        def body(tile, accs):
            finish(jnp.maximum(tile - 1, 0), accs)
            enter_tile(tile)
            q_t, base = tile_operands(tile)
            return attend(tile, q_t, base, shifted)
        dummy = tuple(jnp.ones((VT_ROWS, 2 * ATT_TQ), F32) for _ in range(HEADS_PER_STEP))
        finish(last, lax.fori_loop(0, ntiles, body, dummy, unroll=1 if shifted else 2))

    @pl.when(flag_ref[0] == 1)
    def _():
        run(shifted=False)

    @pl.when(flag_ref[0] != 1)
    def _():
        run(shifted=True)


def _attn_call(q, kp, vtc, nrm, lam_params, subln_g, batch, seq, lam_init):
    n_pairs = N_HEADS // HEADS_PER_STEP
    tiles_per_seq = nrm.shape[0] // batch
    kern = functools.partial(_attn_kernel, lam_init=lam_init, seq=seq)
    pair_rows = HEADS_PER_STEP * VT_ROWS
    return pl.pallas_call(
        kern,
        grid=(batch, n_pairs),
        in_specs=[pl.BlockSpec((4, HEAD_DIM), lambda b, p: (0, 0)),
                  pl.BlockSpec((V_HEAD_DIM, 1), lambda b, p: (0, 0)),
                  pl.BlockSpec((tiles_per_seq, 2 * NORM_ROWS, LANES), lambda b, p: (b, 0, 0)),
                  pl.BlockSpec((seq, LANES), lambda b, p: (b, p)),
                  pl.BlockSpec((seq, KP_WIDTH), lambda b, p: (b, p)),
                  pl.BlockSpec((None, seq // ATT_TQ, pair_rows, ATT_TQ), lambda b, p: (b, 0, p, 0))],
        out_specs=pl.BlockSpec((seq, LANES), lambda b, p: (b, p)),
        out_shape=jax.ShapeDtypeStruct((batch * seq, ATT_WIDTH), BF16),
        scratch_shapes=[pltpu.VMEM((seq, KP_WIDTH), BF16), pltpu.SMEM((1,), jnp.int32)],
        compiler_params=_params("parallel", "parallel"),
        name="diff_attn",
    )(lam_params, subln_g.reshape(V_HEAD_DIM, 1), nrm, q, kp, vtc)


FOLD_BLK = 256
TABLE_SPLIT = 64


def _dft_tables(seq):
    c = FGROUP_DIM
    ci = lax.broadcasted_iota(jnp.int32, (c, c), 0) * lax.broadcasted_iota(jnp.int32, (c, c), 1) % c
    ang_c = ci.astype(F32) * (2.0 * math.pi / c)
    cc, sc = jnp.cos(ang_c) * c ** -0.5, jnp.sin(ang_c) * c ** -0.5
    chan = jnp.concatenate([jnp.concatenate([cc, sc], axis=1), jnp.concatenate([cc, -sc], axis=1)], axis=0)
    half = seq // 2
    n = lax.broadcasted_iota(jnp.int32, (1, half), 1) + 1
    weight = jnp.where(n == half, 0.5, 1.0) * seq ** -0.5

    def thin(count, stride):
        kk = lax.broadcasted_iota(jnp.int32, (count, half), 0) * stride
        ang = (kk * n % seq).astype(F32) * (2.0 * math.pi / seq)
        return jnp.cos(ang), jnp.sin(ang)

    ca, sa = thin(half // TABLE_SPLIT, TABLE_SPLIT)
    cb, sb = thin(TABLE_SPLIT, 1)
    cb, sb = cb * weight, sb * weight
    pos_cos = (ca[:, None, :] * cb[None] - sa[:, None, :] * sb[None]).reshape(half, half).astype(BF16)
    pos_sin = (sa[:, None, :] * cb[None] + ca[:, None, :] * sb[None]).reshape(half, half).astype(BF16)
    n1 = lax.broadcasted_iota(jnp.int32, (BF16_SUBLANES, half), 1) + 1
    row = lax.broadcasted_iota(jnp.int32, (BF16_SUBLANES, half), 0)
    alt = jnp.where(row == 0, jnp.where(n1 % 2 == 0, 1.0, -1.0), 0.0)
    alt = (alt * jnp.where(n1 == half, 0.5, 1.0) * seq ** -0.5).astype(BF16)
    return chan.astype(BF16), pos_cos, pos_sin, alt


def _select(rows, cols, offset):
    r = lax.broadcasted_iota(jnp.int32, (rows, cols), 0)
    c = lax.broadcasted_iota(jnp.int32, (rows, cols), 1)
    return jnp.where(c == offset - r, 1.0, 0.0).astype(BF16)


def _fold_kernel(u_ref, t_ref, ae_ref, bo_ref, a0_ref, *, seq):
    half = seq // 2
    t = t_ref[...]
    r = lax.broadcasted_iota(jnp.int32, (FOLD_BLK, 2 * FOLD_BLK), 0)
    c = lax.broadcasted_iota(jnp.int32, (FOLD_BLK, 2 * FOLD_BLK), 1)
    shift_one = jnp.where(c == r + 1, 1.0, 0.0).astype(BF16)
    reverse = _select(FOLD_BLK, FOLD_BLK, FOLD_BLK - 1)
    for i in range(half // FOLD_BLK):
        lo = i * FOLD_BLK
        fwd = jnp.dot(shift_one, u_ref[lo:lo + 2 * FOLD_BLK, :], preferred_element_type=F32).astype(BF16)
        hi = seq - lo - FOLD_BLK
        bwd = jnp.dot(reverse, u_ref[hi:hi + FOLD_BLK, :], preferred_element_type=F32).astype(BF16)
        for gi in range(N_FGROUPS):
            cols = slice(gi * FGROUP_DIM, (gi + 1) * FGROUP_DIM)
            both = jnp.concatenate([fwd[:, cols], bwd[:, cols]], axis=1)
            res = jnp.dot(both, t, preferred_element_type=F32)
            ae_ref[lo:lo + FOLD_BLK, cols] = res[:, :FGROUP_DIM].astype(ae_ref.dtype)
            bo_ref[lo:lo + FOLD_BLK, cols] = res[:, FGROUP_DIM:].astype(bo_ref.dtype)
    for gi in range(N_FGROUPS):
        cols = slice(gi * FGROUP_DIM, (gi + 1) * FGROUP_DIM)
        first = jnp.dot(u_ref[0:BF16_SUBLANES, cols], t[:FGROUP_DIM, :FGROUP_DIM], preferred_element_type=F32)
        a0_ref[:, cols] = jnp.broadcast_to(first[0:1, :], (a0_ref.shape[0], FGROUP_DIM))


def _fold_call(u, chan_tab, batch, seq):
    half = seq // 2
    out = jax.ShapeDtypeStruct((batch * half, FOURIER_WIDTH), BF16)
    return pl.pallas_call(
        functools.partial(_fold_kernel, seq=seq),
        grid=(batch,),
        in_specs=[pl.BlockSpec((seq, FOURIER_WIDTH), lambda b: (b, 0)),
                  pl.BlockSpec((2 * FGROUP_DIM, 2 * FGROUP_DIM), lambda b: (0, 0))],
        out_specs=[pl.BlockSpec((half, FOURIER_WIDTH), lambda b: (b, 0)),
                   pl.BlockSpec((half, FOURIER_WIDTH), lambda b: (b, 0)),
                   pl.BlockSpec((None, 8, FOURIER_WIDTH), lambda b: (b, 0, 0))],
        out_shape=[out, out, jax.ShapeDtypeStruct((batch, 8, FOURIER_WIDTH), F32)],
        compiler_params=_params("parallel"),
        name="dft_fold",
    )(u, chan_tab)


def _pos_dft_kernel(c_ref, s_ref, alt_ref, ae_ref, bo_ref, a0_ref, o_ref, *, seq):
    half = seq // 2
    ae = ae_ref[...]
    dc = a0_ref[0:1, :] * seq ** -0.5
    even = jnp.dot(c_ref[...], ae, preferred_element_type=F32) + dc
    odd = jnp.dot(s_ref[...], bo_ref[...], preferred_element_type=F32)
    o_ref[0:half, :] = (even - odd).astype(o_ref.dtype)
    mirror = (even + odd).astype(BF16)
    middle = jnp.dot(alt_ref[...], ae, preferred_element_type=F32)[0:1, :] + dc
    middle = jnp.broadcast_to(middle, (FOLD_BLK, FOURIER_WIDTH)).astype(BF16)
    take = _select(FOLD_BLK, 2 * FOLD_BLK, FOLD_BLK)
    nblk = half // FOLD_BLK
    for bt in range(nblk):
        upper = middle if bt == 0 else mirror[(nblk - bt) * FOLD_BLK:(nblk - bt + 1) * FOLD_BLK, :]
        lower = mirror[(nblk - 1 - bt) * FOLD_BLK:(nblk - bt) * FOLD_BLK, :]
        blk = jnp.dot(take, jnp.concatenate([lower, upper], axis=0), preferred_element_type=F32)
        o_ref[half + bt * FOLD_BLK:half + (bt + 1) * FOLD_BLK, :] = blk.astype(o_ref.dtype)


def _pos_dft_call(pos_cos, pos_sin, alt, ae, bo, a0, batch, seq):
    half = seq // 2
    const = dict(pipeline_mode=pl.Buffered(1))
    return pl.pallas_call(
        functools.partial(_pos_dft_kernel, seq=seq),
        grid=(batch,),
        in_specs=[pl.BlockSpec((half, half), lambda b: (0, 0), **const),
                  pl.BlockSpec((half, half), lambda b: (0, 0), **const),
                  pl.BlockSpec((BF16_SUBLANES, half), lambda b: (0, 0)),
                  pl.BlockSpec((half, FOURIER_WIDTH), lambda b: (b, 0)),
                  pl.BlockSpec((half, FOURIER_WIDTH), lambda b: (b, 0)),
                  pl.BlockSpec((None, 8, FOURIER_WIDTH), lambda b: (b, 0, 0))],
        out_specs=pl.BlockSpec((seq, FOURIER_WIDTH), lambda b: (b, 0)),
        out_shape=jax.ShapeDtypeStruct((batch * seq, FOURIER_WIDTH), BF16),
        compiler_params=_params("parallel"),
        name="pos_dft",
    )(pos_cos, pos_sin, alt, ae, bo, a0)


FFN_CHUNKS = (1536, 1280)
TAIL_SUBTILES = 2


def _tail_kernel(h_ref, a_ref, f_ref, wf_ref, bf_ref, wo_ref, g1_ref, b1_ref, wgu_ref, wd_ref, g2_ref, b2_ref,
                 o_ref):
    sub = h_ref.shape[0] // TAIL_SUBTILES
    groups = [slice(s * sub, (s + 1) * sub) for s in range(TAIL_SUBTILES)]

    def mixed(rows):
        parts = []
        for gi in range(N_FGROUPS):
            cols = slice(gi * FGROUP_DIM, (gi + 1) * FGROUP_DIM)
            parts.append(jnp.dot(f_ref[rows, cols], wf_ref[gi], preferred_element_type=F32))
        y = jnp.concatenate(parts, axis=-1) + bf_ref[...]
        mix = jnp.dot(a_ref[rows, :], wo_ref[:ATT_WIDTH, :], preferred_element_type=F32)
        return mix + jnp.dot(y.astype(BF16), wo_ref[ATT_WIDTH:, :], preferred_element_type=F32)

    def swiglu(h1):
        hb = h1.astype(BF16)
        acc = None
        lo = 0
        for width in FFN_CHUNKS:
            gate = jnp.dot(hb, wgu_ref[:, lo:lo + width], preferred_element_type=F32)
            up = jnp.dot(hb, wgu_ref[:, D_FF + lo:D_FF + lo + width], preferred_element_type=F32)
            act = (gate * jax.nn.sigmoid(gate) * up).astype(BF16)
            part = jnp.dot(act, wd_ref[lo:lo + width, :], preferred_element_type=F32)
            acc = part if acc is None else acc + part
            lo += width
        return acc

    mixes = [mixed(rows) for rows in groups]
    h1s = [_layer_norm(ALPHA * h_ref[rows, :] + mix, g1_ref[...], b1_ref[...]) for rows, mix in zip(groups, mixes)]
    ffns = [swiglu(h1) for h1 in h1s]
    for rows, h1, ffn in zip(groups, h1s, ffns):
        o_ref[rows, :] = _layer_norm(ALPHA * h1 + ffn, g2_ref[...], b2_ref[...])


def _tail_call(h2, a, f, layer, wf_bf, b_f, wo_bf, g1, b1, wgu_bf, wd_bf, g2, b2, tm=512):
    m, d = h2.shape
    assert sum(FFN_CHUNKS) == D_FF
    const = dict(pipeline_mode=pl.Buffered(1))
    row = lambda i: (i, 0)
    fixed = lambda i: (0, 0)
    stacked = lambda i: (layer, 0, 0)
    return pl.pallas_call(
        _tail_kernel,
        grid=(m // tm,),
        in_specs=[pl.BlockSpec((tm, d), row),
                  pl.BlockSpec((tm, ATT_WIDTH), row),
                  pl.BlockSpec((tm, FOURIER_WIDTH), row),
                  pl.BlockSpec((None, N_FGROUPS, FGROUP_DIM, FGROUP_DIM), lambda i: (layer, 0, 0, 0), **const),
                  pl.BlockSpec((1, FOURIER_WIDTH), fixed),
                  pl.BlockSpec((None, d, d), stacked, **const),
                  pl.BlockSpec((1, d), fixed),
                  pl.BlockSpec((1, d), fixed),
                  pl.BlockSpec((None, d, 2 * D_FF), stacked, **const),
                  pl.BlockSpec((None, D_FF, d), stacked, **const),
                  pl.BlockSpec((1, d), fixed),
                  pl.BlockSpec((1, d), fixed)],
        out_specs=pl.BlockSpec((tm, d), row),
        out_shape=jax.ShapeDtypeStruct((m, d), F32),
        compiler_params=_params("parallel"),
        name="mix_ffn",
    )(h2, a, f, wf_bf, b_f.reshape(1, -1), wo_bf, g1.reshape(1, d), b1.reshape(1, d),
      wgu_bf, wd_bf, g2.reshape(1, d), b2.reshape(1, d))


def kernel(x, ln_in_g, ln_in_b, w_in, lam_params, subln_g, w_f, b_f, w_o, ln1_g, ln1_b, w_gu, w_down, ln2_g, ln2_b):
    batch, seq, d = x.shape
    assert d == D_MODEL and seq % ATT_TQ == 0
    chan_tab, pos_cos, pos_sin, pos_alt = _dft_tables(seq)
    w_in_bf, w_f_bf, w_o_bf = w_in.astype(BF16), w_f.astype(BF16), w_o.astype(BF16)
    w_gu_bf, w_down_bf = w_gu.astype(BF16), w_down.astype(BF16)
    h = x.reshape(batch * seq, d)
    for l in range(DEPTH):
        if l == 0:
            h, q, kp, vtc, nrm, u = _proj_call(h, w_in_bf, l, batch, seq, ln=(ln_in_g, ln_in_b))
        else:
            q, kp, vtc, nrm, u = _proj_call(h, w_in_bf, l, batch, seq)
        a = _attn_call(q, kp, vtc, nrm, lam_params[l], subln_g[l], batch, seq, _lambda_init(l))
        fe, fo, f0 = _fold_call(u, chan_tab, batch, seq)
        f = _pos_dft_call(pos_cos, pos_sin, pos_alt, fe, fo, f0, batch, seq)
        h = _tail_call(h, a, f, l, w_f_bf, b_f[l], w_o_bf, ln1_g[l], ln1_b[l],
                       w_gu_bf, w_down_bf, ln2_g[l], ln2_b[l])
    return h.reshape(batch, seq, d)
```

```python
import functools
import math

import jax
import jax.numpy as jnp
from jax import lax
from jax.experimental import pallas as pl
from jax.experimental.pallas import tpu as pltpu

D_MODEL = 1024
DEPTH = 2
ATT_WIDTH = 512
FOURIER_WIDTH = 512
N_HEADS = 8
HEAD_DIM = 32
V_HEAD_DIM = 64
N_FGROUPS = 4
FGROUP_DIM = 128
D_FF = 2816
LN_EPS = 1e-5
SUBLN_EPS = 1e-5
ALPHA = (2.0 * DEPTH) ** 0.25
LOG2E = math.log2(math.e)

LANES = 128
F32_SUBLANES = 8
BF16_SUBLANES = 16
VMEM_LIMIT = 56 * 1024 * 1024

BF16 = jnp.bfloat16
F32 = jnp.float32


def _lambda_init(layer_idx):
    return 0.8 - 0.6 * math.exp(-0.3 * layer_idx)


def _layer_norm(x, g, b):
    mu = jnp.mean(x, axis=-1, keepdims=True)
    xc = x - mu
    var = jnp.mean(xc * xc, axis=-1, keepdims=True)
    return xc * lax.rsqrt(var + LN_EPS) * g + b


def _params(*sem):
    return pltpu.CompilerParams(dimension_semantics=sem, vmem_limit_bytes=VMEM_LIMIT)


def _split3(x):
    hi = x.astype(BF16).astype(F32)
    r = x - hi
    mid = r.astype(BF16).astype(F32)
    lo = (r - mid).astype(BF16).astype(F32)
    return hi, mid, lo


ATT_TQ = 256
ATT_RANGE = 8
HEADS_PER_STEP = LANES // V_HEAD_DIM
CHAINS_PER_STEP = 2 * HEADS_PER_STEP
VT_ROWS = 128
KP_WIDTH = 2 * LANES
NORM_ROUND_UP = 1.05
NORM_ROWS = F32_SUBLANES
MAX_UNSHIFTED_SCORE = 50.0


def _slice_norms(x_bf):
    x32 = x_bf.astype(F32)
    seg = lax.broadcasted_iota(jnp.int32, (ATT_WIDTH, LANES), 0) // HEAD_DIM
    col = lax.broadcasted_iota(jnp.int32, (ATT_WIDTH, LANES), 1)
    ind = jnp.where(seg == col, 1.0, 0.0).astype(BF16)
    return jnp.dot((x32 * x32).astype(BF16), ind, preferred_element_type=F32) * NORM_ROUND_UP


def _row_max(x, groups=F32_SUBLANES):
    r, c = x.shape
    return jnp.max(jnp.max(x.reshape(groups, r // groups, c), axis=0), axis=0, keepdims=True)


def _proj_kernel(*refs, seq, tm, pre_ln):
    if pre_ln:
        x_ref, g_ref, b_ref, w_ref, h_ref, q_ref, kp_ref, vtc_ref, nrm_ref, u_ref = refs
        h = _layer_norm(x_ref[...], g_ref[...], b_ref[...])
        h_ref[...] = h
    else:
        x_ref, w_ref, q_ref, kp_ref, vtc_ref, nrm_ref, u_ref = refs
        h = x_ref[...]
    i = pl.program_id(0)
    hb = h.astype(BF16)
    qf = jnp.dot(hb, w_ref[:, 0:ATT_WIDTH], preferred_element_type=F32)
    qb = (qf * (HEAD_DIM ** -0.5 * LOG2E)).astype(BF16)
    q_ref[...] = qb

    kf = jnp.dot(hb, w_ref[:, ATT_WIDTH:2 * ATT_WIDTH], preferred_element_type=F32).astype(BF16)
    pos = (i * tm) % seq + lax.broadcasted_iota(jnp.int32, (tm, 1), 0)
    hi, mid, lo = _split3((pos - seq // 2).astype(F32) * LOG2E)
    lane = lax.broadcasted_iota(jnp.int32, (tm, LANES), 1)
    aux = jnp.where(lane < 3, 1.0,
                    jnp.where(lane == 3, hi, jnp.where(lane == 4, mid, jnp.where(lane == 5, lo, 0.0))))
    aux = aux.astype(BF16)
    for p in range(N_HEADS // HEADS_PER_STEP):
        kp_ref[:, p * KP_WIDTH:p * KP_WIDTH + LANES] = kf[:, p * LANES:(p + 1) * LANES]
        kp_ref[:, p * KP_WIDTH + LANES:(p + 1) * KP_WIDTH] = aux

    nrm_ref[0:NORM_ROWS, :] = jnp.broadcast_to(_row_max(_slice_norms(qb)), (NORM_ROWS, LANES))
    nrm_ref[NORM_ROWS:, :] = jnp.broadcast_to(_row_max(_slice_norms(kf)), (NORM_ROWS, LANES))

    vf = jnp.dot(hb, w_ref[:, 2 * ATT_WIDTH:3 * ATT_WIDTH], preferred_element_type=F32)
    vtr = vf.T.astype(BF16)
    ones = jnp.ones((VT_ROWS - V_HEAD_DIM, ATT_TQ), BF16)
    for hd in range(N_HEADS):
        v_rows = slice(hd * VT_ROWS, hd * VT_ROWS + V_HEAD_DIM)
        one_rows = slice(hd * VT_ROWS + V_HEAD_DIM, (hd + 1) * VT_ROWS)
        for cc in range(tm // ATT_TQ):
            vtc_ref[cc, v_rows, :] = vtr[hd * V_HEAD_DIM:(hd + 1) * V_HEAD_DIM, cc * ATT_TQ:(cc + 1) * ATT_TQ]
            vtc_ref[cc, one_rows, :] = ones

    u_ref[...] = jnp.dot(hb, w_ref[:, 3 * ATT_WIDTH:], preferred_element_type=F32).astype(BF16)


def _proj_call(x2, w_bf, layer, batch, seq, ln=None, tm=1024):
    m, d = x2.shape
    n = w_bf.shape[2]
    tiles_per_seq = seq // tm
    kern = functools.partial(_proj_kernel, seq=seq, tm=tm, pre_ln=ln is not None)
    row = lambda i: (i, 0)
    fixed = lambda i: (0, 0)
    in_specs = [pl.BlockSpec((tm, d), row), pl.BlockSpec((None, d, n), lambda i: (layer, 0, 0))]
    args = [x2, w_bf]
    out_specs = [pl.BlockSpec((tm, ATT_WIDTH), row),
                 pl.BlockSpec((tm, 4 * KP_WIDTH), row),
                 pl.BlockSpec((None, tm // ATT_TQ, N_HEADS * VT_ROWS, ATT_TQ),
                              lambda i: (i // tiles_per_seq, i % tiles_per_seq, 0, 0)),
                 pl.BlockSpec((None, 2 * NORM_ROWS, LANES), lambda i: (i, 0, 0)),
                 pl.BlockSpec((tm, FOURIER_WIDTH), row)]
    out_shape = [jax.ShapeDtypeStruct((m, ATT_WIDTH), BF16),
                 jax.ShapeDtypeStruct((m, 4 * KP_WIDTH), BF16),
                 jax.ShapeDtypeStruct((batch, seq // ATT_TQ, N_HEADS * VT_ROWS, ATT_TQ), BF16),
                 jax.ShapeDtypeStruct((m // tm, 2 * NORM_ROWS, LANES), F32),
                 jax.ShapeDtypeStruct((m, FOURIER_WIDTH), BF16)]
    if ln is not None:
        in_specs[1:1] = [pl.BlockSpec((1, d), fixed), pl.BlockSpec((1, d), fixed)]
        args[1:1] = [ln[0].reshape(1, d), ln[1].reshape(1, d)]
        out_specs.insert(0, pl.BlockSpec((tm, d), row))
        out_shape.insert(0, jax.ShapeDtypeStruct((m, d), F32))
    return pl.pallas_call(
        kern,
        grid=(m // tm,),
        in_specs=in_specs,
        out_specs=out_specs,
        out_shape=out_shape,
        compiler_params=_params("parallel"),
        name="proj",
    )(*args)


def _attn_kernel(lam_ref, g_ref, nrm_ref, q_ref, kp_ref, vtc_ref, o_ref, ks_ref, p_even_ref, p_odd_ref, flag_ref, *,
                 lam_init, seq):
    pair = pl.program_id(1)
    ntiles = seq // ATT_TQ
    last = ntiles - 1
    last_rows = slice(last * ATT_TQ, seq)

    def tile_rows(tile):
        return pl.ds(pl.multiple_of(tile * ATT_TQ, ATT_TQ), ATT_TQ)

    def enter_tile(tile):
        prev = tile_rows(jnp.maximum(tile - 1, 0))
        ks_ref[prev, :] = kp_ref[prev, :]
        ks_ref[tile_rows(tile), :LANES] = kp_ref[last_rows, :LANES]
        ks_ref[tile_rows(tile), LANES:] = -kp_ref[last_rows, LANES:]

    ks_ref[:, :LANES] = kp_ref[:, :LANES]
    ks_ref[:, LANES:] = -kp_ref[:, LANES:]
    nrm = jnp.max(nrm_ref[...], axis=0)
    lane = lax.broadcasted_iota(jnp.int32, (1, LANES), 1)
    prod = jnp.where(lane // CHAINS_PER_STEP == pair, nrm[0:1, :] * nrm[NORM_ROWS:NORM_ROWS + 1, :], 0.0)
    flag_ref[0] = (jnp.max(prod) <= MAX_UNSHIFTED_SCORE ** 2).astype(jnp.int32)

    lp = lam_ref[...]
    lam = (jnp.exp(jnp.sum(lp[0:1] * lp[1:2], axis=-1, keepdims=True))
           - jnp.exp(jnp.sum(lp[2:3] * lp[3:4], axis=-1, keepdims=True)) + lam_init)

    rr = lax.broadcasted_iota(jnp.int32, (ATT_TQ, ATT_TQ), 0)
    cc = lax.broadcasted_iota(jnp.int32, (ATT_TQ, ATT_TQ), 1)
    diag_dist = jnp.abs(rr - cc).astype(F32) * LOG2E
    diag_dist = jnp.concatenate([diag_dist, diag_dist], axis=1)
    row_q = lax.broadcasted_iota(jnp.int32, (LANES, ATT_TQ), 0)
    zeros_t = jnp.zeros((LANES, 2 * ATT_TQ), BF16)
    g_col = g_ref[...]

    def tile_operands(tile):
        q_t = q_ref[tile_rows(tile), :].astype(F32).T
        pos = tile * ATT_TQ + lax.broadcasted_iota(jnp.int32, (1, ATT_TQ), 1)
        hi, mid, lo = _split3((pos - seq // 2).astype(F32) * LOG2E)
        base = jnp.where(row_q == 0, -hi,
                         jnp.where(row_q == 1, -mid,
                                   jnp.where(row_q == 2, -lo, jnp.where(row_q < 6, 1.0, 0.0))))
        return q_t, base

    def stage_operands(tile, hh, q_t, base):
        head = pair * HEADS_PER_STEP + hh
        slope = jnp.exp2(-jnp.full((1, 1), head + 1, jnp.int32).astype(F32))
        aux = (base * slope).astype(BF16)
        qm = jnp.concatenate([jnp.where(row_q // HEAD_DIM == hh * 2 + c, q_t, 0.0).astype(BF16) for c in range(2)],
                             axis=1)
        w_main = jnp.concatenate([qm, jnp.concatenate([aux, aux], axis=1)], axis=0)
        w_diag = jnp.concatenate([qm, zeros_t], axis=0)
        sd = jnp.dot(kp_ref[tile_rows(tile), :], w_diag, preferred_element_type=F32)
        sd = sd - slope * diag_dist
        return w_main, sd

    key_ranges = [(lo, min(lo + ATT_RANGE, last)) for lo in range(0, last, ATT_RANGE)]

    def attend(tile, q_t, base):
        accs = []
        for hh in range(HEADS_PER_STEP):
            w_main, sd = stage_operands(tile, hh, q_t, base)
            head_rows = slice(hh * VT_ROWS, (hh + 1) * VT_ROWS)
            sts = [jnp.dot(ks_ref[lo * ATT_TQ:hi * ATT_TQ, :], w_main, preferred_element_type=F32)
                   for lo, hi in key_ranges]
            m = jnp.max(sd, axis=0, keepdims=True)
            for st in sts:
                m = jnp.maximum(m, jnp.max(st, axis=0, keepdims=True))
            acc = jnp.dot(vtc_ref[tile, head_rows, :], jnp.exp2(sd - m).astype(BF16), preferred_element_type=F32)
            for (lo, _), st in zip(key_ranges, sts):
                p = jnp.exp2(st - m).astype(BF16)
                for j in range(st.shape[0] // ATT_TQ):
                    blk = lo + j
                    chunk = jnp.where(tile == blk, last, blk)
                    acc = acc + jnp.dot(vtc_ref[chunk, head_rows, :], p[j * ATT_TQ:(j + 1) * ATT_TQ, :],
                                        preferred_element_type=F32)
            accs.append(acc)
        return tuple(accs)

    def tile_scores(tile, p_ref):
        q_t, base = tile_operands(tile)
        for hh in range(HEADS_PER_STEP):
            w_main, sd = stage_operands(tile, hh, q_t, base)
            for lo, hi in key_ranges:
                st = jnp.dot(ks_ref[lo * ATT_TQ:hi * ATT_TQ, :], w_main, preferred_element_type=F32)
                p_ref[hh, lo * ATT_TQ:hi * ATT_TQ, :] = jnp.exp2(st).astype(BF16)
            p_ref[hh, last_rows, :] = jnp.exp2(sd).astype(BF16)

    def tile_values(tile, p_ref):
        accs = []
        for hh in range(HEADS_PER_STEP):
            head_rows = slice(hh * VT_ROWS, (hh + 1) * VT_ROWS)
            acc = jnp.dot(vtc_ref[tile, head_rows, :], p_ref[hh, last_rows, :], preferred_element_type=F32)
            for blk in range(last):
                chunk = jnp.where(tile == blk, last, blk)
                acc = acc + jnp.dot(vtc_ref[chunk, head_rows, :], p_ref[hh, blk * ATT_TQ:(blk + 1) * ATT_TQ, :],
                                    preferred_element_type=F32)
            accs.append(acc)
        return tuple(accs)

    def finish(tile, accs):
        outs = []
        for acc in accs:
            norm = []
            for c in range(2):
                part = acc[:, c * ATT_TQ:(c + 1) * ATT_TQ]
                denom = part[V_HEAD_DIM:V_HEAD_DIM + 1, :]
                norm.append(part[:V_HEAD_DIM, :] * (1.0 / denom))
            a_t = norm[0] - lam * norm[1]
            a_t = a_t * lax.rsqrt(jnp.mean(a_t * a_t, axis=0, keepdims=True) + SUBLN_EPS)
            outs.append(a_t * g_col * (1.0 - lam_init))
        o_ref[tile_rows(tile), :] = jnp.concatenate(outs, axis=0).T.astype(o_ref.dtype)

    dummy = tuple(jnp.ones((VT_ROWS, 2 * ATT_TQ), F32) for _ in range(HEADS_PER_STEP))

    @pl.when(flag_ref[0] == 1)
    def _():
        def step(tile, accs, p_new, p_old):
            finish(jnp.maximum(tile - 2, 0), accs)
            enter_tile(tile)
            tile_scores(tile, p_new)
            return tile_values(tile - 1, p_old)

        enter_tile(0)
        tile_scores(0, p_even_ref)

        def body(i, accs):
            odd = 2 * i + 1
            accs = step(odd, accs, p_odd_ref, p_even_ref)
            return step(odd + 1, accs, p_even_ref, p_odd_ref)

        accs = lax.fori_loop(0, (ntiles - 2) // 2, body, dummy)
        accs = step(last, accs, p_odd_ref, p_even_ref)
        finish(last - 1, accs)
        finish(last, tile_values(last, p_odd_ref))

    @pl.when(flag_ref[0] != 1)
    def _():
        def body(tile, accs):
            finish(jnp.maximum(tile - 1, 0), accs)
            enter_tile(tile)
            q_t, base = tile_operands(tile)
            return attend(tile, q_t, base)

        finish(last, lax.fori_loop(0, ntiles, body, dummy))


def _attn_call(q, kp, vtc, nrm, lam_params, subln_g, batch, seq, lam_init):
    n_pairs = N_HEADS // HEADS_PER_STEP
    tiles_per_seq = nrm.shape[0] // batch
    kern = functools.partial(_attn_kernel, lam_init=lam_init, seq=seq)
    pair_rows = HEADS_PER_STEP * VT_ROWS
    return pl.pallas_call(
        kern,
        grid=(batch, n_pairs),
        in_specs=[pl.BlockSpec((4, HEAD_DIM), lambda b, p: (0, 0)),
                  pl.BlockSpec((V_HEAD_DIM, 1), lambda b, p: (0, 0)),
                  pl.BlockSpec((tiles_per_seq, 2 * NORM_ROWS, LANES), lambda b, p: (b, 0, 0)),
                  pl.BlockSpec((seq, LANES), lambda b, p: (b, p)),
                  pl.BlockSpec((seq, KP_WIDTH), lambda b, p: (b, p)),
                  pl.BlockSpec((None, seq // ATT_TQ, pair_rows, ATT_TQ), lambda b, p: (b, 0, p, 0))],
        out_specs=pl.BlockSpec((seq, LANES), lambda b, p: (b, p)),
        out_shape=jax.ShapeDtypeStruct((batch * seq, ATT_WIDTH), BF16),
        scratch_shapes=[pltpu.VMEM((seq, KP_WIDTH), BF16),
                        pltpu.VMEM((HEADS_PER_STEP, seq, 2 * ATT_TQ), BF16),
                        pltpu.VMEM((HEADS_PER_STEP, seq, 2 * ATT_TQ), BF16),
                        pltpu.SMEM((1,), jnp.int32)],
        compiler_params=_params("parallel", "parallel"),
        name="diff_attn",
    )(lam_params, subln_g.reshape(V_HEAD_DIM, 1), nrm, q, kp, vtc)


FOLD_BLK = 256
TABLE_SPLIT = 64


def _dft_tables(seq):
    c = FGROUP_DIM
    ci = lax.broadcasted_iota(jnp.int32, (c, c), 0) * lax.broadcasted_iota(jnp.int32, (c, c), 1) % c
    ang_c = ci.astype(F32) * (2.0 * math.pi / c)
    cc, sc = jnp.cos(ang_c) * c ** -0.5, jnp.sin(ang_c) * c ** -0.5
    chan = jnp.concatenate([jnp.concatenate([cc, sc], axis=1), jnp.concatenate([cc, -sc], axis=1)], axis=0)
    half = seq // 2
    n = lax.broadcasted_iota(jnp.int32, (1, half), 1) + 1
    weight = jnp.where(n == half, 0.5, 1.0) * seq ** -0.5

    def thin(count, stride):
        kk = lax.broadcasted_iota(jnp.int32, (count, half), 0) * stride
        ang = (kk * n % seq).astype(F32) * (2.0 * math.pi / seq)
        return jnp.cos(ang), jnp.sin(ang)

    ca, sa = thin(half // TABLE_SPLIT, TABLE_SPLIT)
    cb, sb = thin(TABLE_SPLIT, 1)
    cb, sb = cb * weight, sb * weight
    pos_cos = (ca[:, None, :] * cb[None] - sa[:, None, :] * sb[None]).reshape(half, half).astype(BF16)
    pos_sin = (sa[:, None, :] * cb[None] + ca[:, None, :] * sb[None]).reshape(half, half).astype(BF16)
    n1 = lax.broadcasted_iota(jnp.int32, (BF16_SUBLANES, half), 1) + 1
    row = lax.broadcasted_iota(jnp.int32, (BF16_SUBLANES, half), 0)
    alt = jnp.where(row == 0, jnp.where(n1 % 2 == 0, 1.0, -1.0), 0.0)
    alt = (alt * jnp.where(n1 == half, 0.5, 1.0) * seq ** -0.5).astype(BF16)
    return chan.astype(BF16), pos_cos, pos_sin, alt


def _select(rows, cols, offset):
    r = lax.broadcasted_iota(jnp.int32, (rows, cols), 0)
    c = lax.broadcasted_iota(jnp.int32, (rows, cols), 1)
    return jnp.where(c == offset - r, 1.0, 0.0).astype(BF16)


def _fold_kernel(u_ref, t_ref, ae_ref, bo_ref, a0_ref, *, seq):
    half = seq // 2
    t = t_ref[...]
    r = lax.broadcasted_iota(jnp.int32, (FOLD_BLK, FOLD_BLK), 0)
    c = lax.broadcasted_iota(jnp.int32, (FOLD_BLK, FOLD_BLK), 1)
    shift_one = jnp.where(c == r + 1, 1.0, 0.0).astype(BF16)
    last_row = lax.broadcasted_iota(jnp.int32, (FOLD_BLK, FOURIER_WIDTH), 0) == FOLD_BLK - 1
    reverse = _select(FOLD_BLK, FOLD_BLK, FOLD_BLK - 1)
    for i in range(half // FOLD_BLK):
        lo = i * FOLD_BLK
        nxt = u_ref[lo + FOLD_BLK:lo + FOLD_BLK + BF16_SUBLANES, :][0:1, :].astype(F32)
        fwd = jnp.where(last_row, nxt, jnp.dot(shift_one, u_ref[lo:lo + FOLD_BLK, :], preferred_element_type=F32))
        fwd = fwd.astype(BF16)
        hi = seq - lo - FOLD_BLK
        bwd = jnp.dot(reverse, u_ref[hi:hi + FOLD_BLK, :], preferred_element_type=F32).astype(BF16)
        for gi in range(N_FGROUPS):
            cols = slice(gi * FGROUP_DIM, (gi + 1) * FGROUP_DIM)
            both = jnp.concatenate([fwd[:, cols], bwd[:, cols]], axis=1)
            res = jnp.dot(both, t, preferred_element_type=F32)
            ae_ref[lo:lo + FOLD_BLK, cols] = res[:, :FGROUP_DIM].astype(ae_ref.dtype)
            bo_ref[lo:lo + FOLD_BLK, cols] = res[:, FGROUP_DIM:].astype(bo_ref.dtype)
    for gi in range(N_FGROUPS):
        cols = slice(gi * FGROUP_DIM, (gi + 1) * FGROUP_DIM)
        first = jnp.dot(u_ref[0:BF16_SUBLANES, cols], t[:FGROUP_DIM, :FGROUP_DIM], preferred_element_type=F32)
        a0_ref[:, cols] = jnp.broadcast_to(first[0:1, :], (a0_ref.shape[0], FGROUP_DIM))


def _fold_call(u, chan_tab, batch, seq):
    half = seq // 2
    out = jax.ShapeDtypeStruct((batch * half, FOURIER_WIDTH), BF16)
    return pl.pallas_call(
        functools.partial(_fold_kernel, seq=seq),
        grid=(batch,),
        in_specs=[pl.BlockSpec((seq, FOURIER_WIDTH), lambda b: (b, 0)),
                  pl.BlockSpec((2 * FGROUP_DIM, 2 * FGROUP_DIM), lambda b: (0, 0))],
        out_specs=[pl.BlockSpec((half, FOURIER_WIDTH), lambda b: (b, 0)),
                   pl.BlockSpec((half, FOURIER_WIDTH), lambda b: (b, 0)),
                   pl.BlockSpec((None, F32_SUBLANES, FOURIER_WIDTH), lambda b: (b, 0, 0))],
        out_shape=[out, out, jax.ShapeDtypeStruct((batch, F32_SUBLANES, FOURIER_WIDTH), F32)],
        compiler_params=_params("parallel"),
        name="dft_fold",
    )(u, chan_tab)


def _pos_dft_kernel(c_ref, s_ref, alt_ref, ae_ref, bo_ref, a0_ref, o_ref, *, seq):
    half = seq // 2
    ae = ae_ref[...]
    dc = a0_ref[0:1, :] * seq ** -0.5
    even = jnp.dot(c_ref[...], ae, preferred_element_type=F32) + dc
    odd = jnp.dot(s_ref[...], bo_ref[...], preferred_element_type=F32)
    o_ref[0:half, :] = (even - odd).astype(o_ref.dtype)
    mirror = (even + odd).astype(BF16)
    middle = jnp.dot(alt_ref[...], ae, preferred_element_type=F32)[0:1, :] + dc
    take = _select(FOLD_BLK, FOLD_BLK, FOLD_BLK)
    first_row = lax.broadcasted_iota(jnp.int32, (FOLD_BLK, FOURIER_WIDTH), 0) == 0
    nblk = half // FOLD_BLK
    for bt in range(nblk):
        head = middle if bt == 0 else mirror[(nblk - bt) * FOLD_BLK:(nblk - bt) * FOLD_BLK + 1, :].astype(F32)
        lower = mirror[(nblk - 1 - bt) * FOLD_BLK:(nblk - bt) * FOLD_BLK, :]
        blk = jnp.where(first_row, head, jnp.dot(take, lower, preferred_element_type=F32))
        o_ref[half + bt * FOLD_BLK:half + (bt + 1) * FOLD_BLK, :] = blk.astype(o_ref.dtype)


def _pos_dft_call(pos_cos, pos_sin, alt, ae, bo, a0, batch, seq):
    half = seq // 2
    const = dict(pipeline_mode=pl.Buffered(1))
    return pl.pallas_call(
        functools.partial(_pos_dft_kernel, seq=seq),
        grid=(batch,),
        in_specs=[pl.BlockSpec((half, half), lambda b: (0, 0), **const),
                  pl.BlockSpec((half, half), lambda b: (0, 0), **const),
                  pl.BlockSpec((BF16_SUBLANES, half), lambda b: (0, 0)),
                  pl.BlockSpec((half, FOURIER_WIDTH), lambda b: (b, 0)),
                  pl.BlockSpec((half, FOURIER_WIDTH), lambda b: (b, 0)),
                  pl.BlockSpec((None, F32_SUBLANES, FOURIER_WIDTH), lambda b: (b, 0, 0))],
        out_specs=pl.BlockSpec((seq, FOURIER_WIDTH), lambda b: (b, 0)),
        out_shape=jax.ShapeDtypeStruct((batch * seq, FOURIER_WIDTH), BF16),
        compiler_params=_params("parallel"),
        name="pos_dft",
    )(pos_cos, pos_sin, alt, ae, bo, a0)


FFN_CHUNKS = (1536, 1280)
TAIL_SUBTILES = 2


def _tail_kernel(h_ref, a_ref, f_ref, wf_ref, bf_ref, wo_ref, g1_ref, b1_ref, wgu_ref, wd_ref, g2_ref, b2_ref,
                 o_ref):
    sub = h_ref.shape[0] // TAIL_SUBTILES
    groups = [slice(s * sub, (s + 1) * sub) for s in range(TAIL_SUBTILES)]

    def mixed(rows):
        parts = []
        for gi in range(N_FGROUPS):
            cols = slice(gi * FGROUP_DIM, (gi + 1) * FGROUP_DIM)
            parts.append(jnp.dot(f_ref[rows, cols], wf_ref[gi], preferred_element_type=F32))
        y = jnp.concatenate(parts, axis=-1) + bf_ref[...]
        mix = jnp.dot(a_ref[rows, :], wo_ref[:ATT_WIDTH, :], preferred_element_type=F32)
        return mix + jnp.dot(y.astype(BF16), wo_ref[ATT_WIDTH:, :], preferred_element_type=F32)

    def swiglu(h1):
        hb = h1.astype(BF16)
        acc = None
        lo = 0
        for width in FFN_CHUNKS:
            gate = jnp.dot(hb, wgu_ref[:, lo:lo + width], preferred_element_type=F32)
            up = jnp.dot(hb, wgu_ref[:, D_FF + lo:D_FF + lo + width], preferred_element_type=F32)
            act = (gate * jax.nn.sigmoid(gate) * up).astype(BF16)
            part = jnp.dot(act, wd_ref[lo:lo + width, :], preferred_element_type=F32)
            acc = part if acc is None else acc + part
            lo += width
        return acc

    mixes = [mixed(rows) for rows in groups]
    h1s = [_layer_norm(ALPHA * h_ref[rows, :] + mix, g1_ref[...], b1_ref[...]) for rows, mix in zip(groups, mixes)]
    ffns = [swiglu(h1) for h1 in h1s]
    for rows, h1, ffn in zip(groups, h1s, ffns):
        o_ref[rows, :] = _layer_norm(ALPHA * h1 + ffn, g2_ref[...], b2_ref[...])


def _tail_call(h2, a, f, layer, wf_bf, b_f, wo_bf, g1, b1, wgu_bf, wd_bf, g2, b2, tm=512):
    m, d = h2.shape
    assert sum(FFN_CHUNKS) == D_FF
    const = dict(pipeline_mode=pl.Buffered(1))
    row = lambda i: (i, 0)
    fixed = lambda i: (0, 0)
    stacked = lambda i: (layer, 0, 0)
    return pl.pallas_call(
        _tail_kernel,
        grid=(m // tm,),
        in_specs=[pl.BlockSpec((tm, d), row),
                  pl.BlockSpec((tm, ATT_WIDTH), row),
                  pl.BlockSpec((tm, FOURIER_WIDTH), row),
                  pl.BlockSpec((None, N_FGROUPS, FGROUP_DIM, FGROUP_DIM), lambda i: (layer, 0, 0, 0), **const),
                  pl.BlockSpec((1, FOURIER_WIDTH), fixed),
                  pl.BlockSpec((None, d, d), stacked, **const),
                  pl.BlockSpec((1, d), fixed),
                  pl.BlockSpec((1, d), fixed),
                  pl.BlockSpec((None, d, 2 * D_FF), stacked, **const),
                  pl.BlockSpec((None, D_FF, d), stacked, **const),
                  pl.BlockSpec((1, d), fixed),
                  pl.BlockSpec((1, d), fixed)],
        out_specs=pl.BlockSpec((tm, d), row),
        out_shape=jax.ShapeDtypeStruct((m, d), F32),
        compiler_params=_params("parallel"),
        name="mix_ffn",
    )(h2, a, f, wf_bf, b_f.reshape(1, -1), wo_bf, g1.reshape(1, d), b1.reshape(1, d),
      wgu_bf, wd_bf, g2.reshape(1, d), b2.reshape(1, d))


def kernel(x, ln_in_g, ln_in_b, w_in, lam_params, subln_g, w_f, b_f, w_o, ln1_g, ln1_b, w_gu, w_down, ln2_g, ln2_b):
    batch, seq, d = x.shape
    assert d == D_MODEL and seq % (2 * ATT_TQ) == 0
    chan_tab, pos_cos, pos_sin, pos_alt = _dft_tables(seq)
    w_in_bf, w_f_bf, w_o_bf = w_in.astype(BF16), w_f.astype(BF16), w_o.astype(BF16)
    w_gu_bf, w_down_bf = w_gu.astype(BF16), w_down.astype(BF16)
    h = x.reshape(batch * seq, d)
    for l in range(DEPTH):
        if l == 0:
            h, q, kp, vtc, nrm, u = _proj_call(h, w_in_bf, l, batch, seq, ln=(ln_in_g, ln_in_b))
        else:
            q, kp, vtc, nrm, u = _proj_call(h, w_in_bf, l, batch, seq)
        a = _attn_call(q, kp, vtc, nrm, lam_params[l], subln_g[l], batch, seq, _lambda_init(l))
        fe, fo, f0 = _fold_call(u, chan_tab, batch, seq)
        f = _pos_dft_call(pos_cos, pos_sin, pos_alt, fe, fo, f0, batch, seq)
        h = _tail_call(h, a, f, l, w_f_bf, b_f[l], w_o_bf, ln1_g[l], ln1_b[l],
                       w_gu_bf, w_down_bf, ln2_g[l], ln2_b[l])
    return h.reshape(batch, seq, d)
```

```python
import functools
import math

import jax
import jax.numpy as jnp
from jax import lax
from jax.experimental import pallas as pl
from jax.experimental.pallas import tpu as pltpu

D_MODEL = 1024
DEPTH = 2
ATT_WIDTH = 512
FOURIER_WIDTH = 512
N_HEADS = 8
HEAD_DIM = 32
V_HEAD_DIM = 64
N_FGROUPS = 4
FGROUP_DIM = 128
D_FF = 2816
LN_EPS = 1e-5
SUBLN_EPS = 1e-5
ALPHA = (2.0 * DEPTH) ** 0.25
LOG2E = math.log2(math.e)

LANES = 128
F32_SUBLANES = 8
BF16_SUBLANES = 16
VMEM_LIMIT = 56 * 1024 * 1024

BF16 = jnp.bfloat16
F32 = jnp.float32


def _lambda_init(layer_idx):
    return 0.8 - 0.6 * math.exp(-0.3 * layer_idx)


def _layer_norm(x, g, b):
    mu = jnp.mean(x, axis=-1, keepdims=True)
    xc = x - mu
    var = jnp.mean(xc * xc, axis=-1, keepdims=True)
    return xc * lax.rsqrt(var + LN_EPS) * g + b


def _params(*sem):
    return pltpu.CompilerParams(dimension_semantics=sem, vmem_limit_bytes=VMEM_LIMIT)


def _split3(x):
    hi = x.astype(BF16).astype(F32)
    r = x - hi
    mid = r.astype(BF16).astype(F32)
    lo = (r - mid).astype(BF16).astype(F32)
    return hi, mid, lo


ATT_TQ = 256
ATT_RANGE = 8
HEADS_PER_STEP = LANES // V_HEAD_DIM
CHAINS_PER_STEP = 2 * HEADS_PER_STEP
VT_ROWS = 128
KP_WIDTH = 2 * LANES
NORM_ROUND_UP = 1.05
NORM_ROWS = F32_SUBLANES
MAX_UNSHIFTED_SCORE = 50.0


def _position_columns(seq):
    pos = lax.broadcasted_iota(jnp.int32, (seq, 1), 0)
    hi, mid, lo = _split3((pos - seq // 2).astype(F32) * LOG2E)
    lane = lax.broadcasted_iota(jnp.int32, (seq, LANES), 1)
    aux = jnp.where(lane < 3, 1.0,
                    jnp.where(lane == 3, hi, jnp.where(lane == 4, mid, jnp.where(lane == 5, lo, 0.0))))
    return aux.astype(BF16)


def _slice_norms(x_bf):
    x32 = x_bf.astype(F32)
    seg = lax.broadcasted_iota(jnp.int32, (ATT_WIDTH, LANES), 0) // HEAD_DIM
    col = lax.broadcasted_iota(jnp.int32, (ATT_WIDTH, LANES), 1)
    ind = jnp.where(seg == col, 1.0, 0.0).astype(BF16)
    return jnp.dot((x32 * x32).astype(BF16), ind, preferred_element_type=F32) * NORM_ROUND_UP


def _row_max(x, groups=F32_SUBLANES):
    r, c = x.shape
    return jnp.max(jnp.max(x.reshape(groups, r // groups, c), axis=0), axis=0, keepdims=True)


def _proj_kernel(*refs, tm, pre_ln):
    if pre_ln:
        x_ref, g_ref, b_ref, w_ref, h_ref, q_ref, k_ref, vtc_ref, nrm_ref, u_ref = refs
        h = _layer_norm(x_ref[...], g_ref[...], b_ref[...])
        h_ref[...] = h
    else:
        x_ref, w_ref, q_ref, k_ref, vtc_ref, nrm_ref, u_ref = refs
        h = x_ref[...]
    hb = h.astype(BF16)
    qf = jnp.dot(hb, w_ref[:, 0:ATT_WIDTH], preferred_element_type=F32)
    qb = (qf * (HEAD_DIM ** -0.5 * LOG2E)).astype(BF16)
    q_ref[...] = qb

    kf = jnp.dot(hb, w_ref[:, ATT_WIDTH:2 * ATT_WIDTH], preferred_element_type=F32).astype(BF16)
    k_ref[...] = kf

    nrm_ref[0:NORM_ROWS, :] = jnp.broadcast_to(_row_max(_slice_norms(qb)), (NORM_ROWS, LANES))
    nrm_ref[NORM_ROWS:, :] = jnp.broadcast_to(_row_max(_slice_norms(kf)), (NORM_ROWS, LANES))

    vf = jnp.dot(hb, w_ref[:, 2 * ATT_WIDTH:3 * ATT_WIDTH], preferred_element_type=F32)
    vtr = vf.T.astype(BF16)
    for cc in range(tm // ATT_TQ):
        vtc_ref[cc] = vtr[:, cc * ATT_TQ:(cc + 1) * ATT_TQ]

    u_ref[...] = jnp.dot(hb, w_ref[:, 3 * ATT_WIDTH:], preferred_element_type=F32).astype(BF16)


def _proj_call(x2, w_bf, layer, batch, seq, ln=None, tm=1024):
    m, d = x2.shape
    n = w_bf.shape[2]
    tiles_per_seq = seq // tm
    kern = functools.partial(_proj_kernel, tm=tm, pre_ln=ln is not None)
    row = lambda i: (i, 0)
    fixed = lambda i: (0, 0)
    in_specs = [pl.BlockSpec((tm, d), row), pl.BlockSpec((None, d, n), lambda i: (layer, 0, 0))]
    args = [x2, w_bf]
    out_specs = [pl.BlockSpec((tm, ATT_WIDTH), row),
                 pl.BlockSpec((tm, ATT_WIDTH), row),
                 pl.BlockSpec((None, tm // ATT_TQ, ATT_WIDTH, ATT_TQ),
                              lambda i: (i // tiles_per_seq, i % tiles_per_seq, 0, 0)),
                 pl.BlockSpec((None, 2 * NORM_ROWS, LANES), lambda i: (i, 0, 0)),
                 pl.BlockSpec((tm, FOURIER_WIDTH), row)]
    out_shape = [jax.ShapeDtypeStruct((m, ATT_WIDTH), BF16),
                 jax.ShapeDtypeStruct((m, ATT_WIDTH), BF16),
                 jax.ShapeDtypeStruct((batch, seq // ATT_TQ, ATT_WIDTH, ATT_TQ), BF16),
                 jax.ShapeDtypeStruct((m // tm, 2 * NORM_ROWS, LANES), F32),
                 jax.ShapeDtypeStruct((m, FOURIER_WIDTH), BF16)]
    if ln is not None:
        in_specs[1:1] = [pl.BlockSpec((1, d), fixed), pl.BlockSpec((1, d), fixed)]
        args[1:1] = [ln[0].reshape(1, d), ln[1].reshape(1, d)]
        out_specs.insert(0, pl.BlockSpec((tm, d), row))
        out_shape.insert(0, jax.ShapeDtypeStruct((m, d), F32))
    return pl.pallas_call(
        kern,
        grid=(m // tm,),
        in_specs=in_specs,
        out_specs=out_specs,
        out_shape=out_shape,
        compiler_params=_params("parallel"),
        name="proj",
    )(*args)


def _attn_kernel(lam_ref, g_ref, nrm_ref, q_ref, k_ref, aux_ref, vtc_ref, o_ref, ks_ref, p_even_ref, p_odd_ref, flag_ref,
                 *,
                 lam_init, seq):
    pair = pl.program_id(1)
    ntiles = seq // ATT_TQ
    last = ntiles - 1
    last_rows = slice(last * ATT_TQ, seq)

    def tile_rows(tile):
        return pl.ds(pl.multiple_of(tile * ATT_TQ, ATT_TQ), ATT_TQ)

    def enter_tile(tile):
        prev = tile_rows(jnp.maximum(tile - 1, 0))
        ks_ref[prev, :LANES] = k_ref[prev, :]
        ks_ref[prev, LANES:] = aux_ref[prev, :]
        ks_ref[tile_rows(tile), :LANES] = k_ref[last_rows, :]
        ks_ref[tile_rows(tile), LANES:] = -aux_ref[last_rows, :]

    ks_ref[:, :LANES] = k_ref[...]
    ks_ref[:, LANES:] = -aux_ref[...]
    nrm = jnp.max(nrm_ref[...], axis=0)
    lane = lax.broadcasted_iota(jnp.int32, (1, LANES), 1)
    prod = jnp.where(lane // CHAINS_PER_STEP == pair, nrm[0:1, :] * nrm[NORM_ROWS:NORM_ROWS + 1, :], 0.0)
    flag_ref[0] = (jnp.max(prod) <= MAX_UNSHIFTED_SCORE ** 2).astype(jnp.int32)

    lp = lam_ref[...]
    lam = (jnp.exp(jnp.sum(lp[0:1] * lp[1:2], axis=-1, keepdims=True))
           - jnp.exp(jnp.sum(lp[2:3] * lp[3:4], axis=-1, keepdims=True)) + lam_init)

    rr = lax.broadcasted_iota(jnp.int32, (ATT_TQ, ATT_TQ), 0)
    cc = lax.broadcasted_iota(jnp.int32, (ATT_TQ, ATT_TQ), 1)
    diag_dist = jnp.abs(rr - cc).astype(F32) * LOG2E
    diag_dist = jnp.concatenate([diag_dist, diag_dist], axis=1)
    row_q = lax.broadcasted_iota(jnp.int32, (LANES, ATT_TQ), 0)
    ones_rows = jnp.ones((VT_ROWS - V_HEAD_DIM, ATT_TQ), BF16)
    g_col = g_ref[...]

    def tile_operands(tile):
        q_t = q_ref[tile_rows(tile), :].astype(F32).T
        pos = tile * ATT_TQ + lax.broadcasted_iota(jnp.int32, (1, ATT_TQ), 1)
        hi, mid, lo = _split3((pos - seq // 2).astype(F32) * LOG2E)
        base = jnp.where(row_q == 0, -hi,
                         jnp.where(row_q == 1, -mid,
                                   jnp.where(row_q == 2, -lo, jnp.where(row_q < 6, 1.0, 0.0))))
        return q_t, base

    def stage_operands(tile, hh, q_t, base):
        head = pair * HEADS_PER_STEP + hh
        slope = jnp.exp2(-jnp.full((1, 1), head + 1, jnp.int32).astype(F32))
        aux = (base * slope).astype(BF16)
        qm = jnp.concatenate([jnp.where(row_q // HEAD_DIM == hh * 2 + c, q_t, 0.0).astype(BF16) for c in range(2)],
                             axis=1)
        w_main = jnp.concatenate([qm, jnp.concatenate([aux, aux], axis=1)], axis=0)
        sd = jnp.dot(k_ref[tile_rows(tile), :], qm, preferred_element_type=F32)
        sd = sd - slope * diag_dist
        return w_main, sd

    def values_lhs(chunk, hh):
        return jnp.concatenate([vtc_ref[chunk, hh * V_HEAD_DIM:(hh + 1) * V_HEAD_DIM, :], ones_rows], axis=0)

    key_ranges = [(lo, min(lo + ATT_RANGE, last)) for lo in range(0, last, ATT_RANGE)]

    def attend(tile, q_t, base):
        accs = []
        for hh in range(HEADS_PER_STEP):
            w_main, sd = stage_operands(tile, hh, q_t, base)
            sts = [jnp.dot(ks_ref[lo * ATT_TQ:hi * ATT_TQ, :], w_main, preferred_element_type=F32)
                   for lo, hi in key_ranges]
            m = jnp.max(sd, axis=0, keepdims=True)
            for st in sts:
                m = jnp.maximum(m, jnp.max(st, axis=0, keepdims=True))
            acc = jnp.dot(values_lhs(tile, hh), jnp.exp2(sd - m).astype(BF16), preferred_element_type=F32)
            for (lo, _), st in zip(key_ranges, sts):
                p = jnp.exp2(st - m).astype(BF16)
                for j in range(st.shape[0] // ATT_TQ):
                    blk = lo + j
                    chunk = jnp.where(tile == blk, last, blk)
                    acc = acc + jnp.dot(values_lhs(chunk, hh), p[j * ATT_TQ:(j + 1) * ATT_TQ, :],
                                        preferred_element_type=F32)
            accs.append(acc)
        return tuple(accs)

    def tile_scores(tile, p_ref):
        q_t, base = tile_operands(tile)
        for hh in range(HEADS_PER_STEP):
            w_main, sd = stage_operands(tile, hh, q_t, base)
            for lo, hi in key_ranges:
                st = jnp.dot(ks_ref[lo * ATT_TQ:hi * ATT_TQ, :], w_main, preferred_element_type=F32)
                p_ref[hh, lo * ATT_TQ:hi * ATT_TQ, :] = jnp.exp2(st).astype(BF16)
            p_ref[hh, last_rows, :] = jnp.exp2(sd).astype(BF16)

    def tile_values(tile, p_ref):
        accs = []
        for hh in range(HEADS_PER_STEP):
            acc = jnp.dot(values_lhs(tile, hh), p_ref[hh, last_rows, :], preferred_element_type=F32)
            for blk in range(last):
                chunk = jnp.where(tile == blk, last, blk)
                acc = acc + jnp.dot(values_lhs(chunk, hh), p_ref[hh, blk * ATT_TQ:(blk + 1) * ATT_TQ, :],
                                    preferred_element_type=F32)
            accs.append(acc)
        return tuple(accs)

    def finish(tile, accs):
        outs = []
        for acc in accs:
            norm = []
            for c in range(2):
                part = acc[:, c * ATT_TQ:(c + 1) * ATT_TQ]
                denom = part[V_HEAD_DIM:V_HEAD_DIM + 1, :]
                norm.append(part[:V_HEAD_DIM, :] * (1.0 / denom))
            a_t = norm[0] - lam * norm[1]
            a_t = a_t * lax.rsqrt(jnp.mean(a_t * a_t, axis=0, keepdims=True) + SUBLN_EPS)
            outs.append(a_t * g_col * (1.0 - lam_init))
        o_ref[tile_rows(tile), :] = jnp.concatenate(outs, axis=0).T.astype(o_ref.dtype)

    dummy = tuple(jnp.ones((VT_ROWS, 2 * ATT_TQ), F32) for _ in range(HEADS_PER_STEP))

    @pl.when(flag_ref[0] == 1)
    def _():
        def step(tile, accs, p_new, p_old):
            finish(jnp.maximum(tile - 2, 0), accs)
            enter_tile(tile)
            tile_scores(tile, p_new)
            return tile_values(tile - 1, p_old)

        enter_tile(0)
        tile_scores(0, p_even_ref)

        def body(i, accs):
            odd = 2 * i + 1
            accs = step(odd, accs, p_odd_ref, p_even_ref)
            return step(odd + 1, accs, p_even_ref, p_odd_ref)

        accs = lax.fori_loop(0, (ntiles - 2) // 2, body, dummy)
        accs = step(last, accs, p_odd_ref, p_even_ref)
        finish(last - 1, accs)
        finish(last, tile_values(last, p_odd_ref))

    @pl.when(flag_ref[0] != 1)
    def _():
        def body(tile, accs):
            finish(jnp.maximum(tile - 1, 0), accs)
            enter_tile(tile)
            q_t, base = tile_operands(tile)
            return attend(tile, q_t, base)

        finish(last, lax.fori_loop(0, ntiles, body, dummy))


def _attn_call(q, k, pos_cols, vtc, nrm, lam_params, subln_g, batch, seq, lam_init):
    n_pairs = N_HEADS // HEADS_PER_STEP
    tiles_per_seq = nrm.shape[0] // batch
    kern = functools.partial(_attn_kernel, lam_init=lam_init, seq=seq)
    pair_rows = HEADS_PER_STEP * V_HEAD_DIM
    return pl.pallas_call(
        kern,
        grid=(batch, n_pairs),
        in_specs=[pl.BlockSpec((4, HEAD_DIM), lambda b, p: (0, 0)),
                  pl.BlockSpec((V_HEAD_DIM, 1), lambda b, p: (0, 0)),
                  pl.BlockSpec((tiles_per_seq, 2 * NORM_ROWS, LANES), lambda b, p: (b, 0, 0)),
                  pl.BlockSpec((seq, LANES), lambda b, p: (b, p)),
                  pl.BlockSpec((seq, LANES), lambda b, p: (b, p)),
                  pl.BlockSpec((seq, LANES), lambda b, p: (0, 0)),
                  pl.BlockSpec((None, seq // ATT_TQ, pair_rows, ATT_TQ), lambda b, p: (b, 0, p, 0))],
        out_specs=pl.BlockSpec((seq, LANES), lambda b, p: (b, p)),
        out_shape=jax.ShapeDtypeStruct((batch * seq, ATT_WIDTH), BF16),
        scratch_shapes=[pltpu.VMEM((seq, KP_WIDTH), BF16),
                        pltpu.VMEM((HEADS_PER_STEP, seq, 2 * ATT_TQ), BF16),
                        pltpu.VMEM((HEADS_PER_STEP, seq, 2 * ATT_TQ), BF16),
                        pltpu.SMEM((1,), jnp.int32)],
        compiler_params=_params("parallel", "parallel"),
        name="diff_attn",
    )(lam_params, subln_g.reshape(V_HEAD_DIM, 1), nrm, q, k, pos_cols, vtc)


FOLD_BLK = 256
TABLE_SPLIT = 64


def _dft_tables(seq):
    c = FGROUP_DIM
    ci = lax.broadcasted_iota(jnp.int32, (c, c), 0) * lax.broadcasted_iota(jnp.int32, (c, c), 1) % c
    ang_c = ci.astype(F32) * (2.0 * math.pi / c)
    cc, sc = jnp.cos(ang_c) * c ** -0.5, jnp.sin(ang_c) * c ** -0.5
    chan = jnp.concatenate([jnp.concatenate([cc, sc], axis=1), jnp.concatenate([cc, -sc], axis=1)], axis=0)
    half = seq // 2
    n = lax.broadcasted_iota(jnp.int32, (1, half), 1) + 1
    weight = jnp.where(n == half, 0.5, 1.0) * seq ** -0.5

    def thin(count, stride):
        kk = lax.broadcasted_iota(jnp.int32, (count, half), 0) * stride
        ang = (kk * n % seq).astype(F32) * (2.0 * math.pi / seq)
        return jnp.cos(ang), jnp.sin(ang)

    ca, sa = thin(half // TABLE_SPLIT, TABLE_SPLIT)
    cb, sb = thin(TABLE_SPLIT, 1)
    cb, sb = cb * weight, sb * weight
    pos_cos = (ca[:, None, :] * cb[None] - sa[:, None, :] * sb[None]).reshape(half, half).astype(BF16)
    pos_sin = (sa[:, None, :] * cb[None] + ca[:, None, :] * sb[None]).reshape(half, half).astype(BF16)
    n1 = lax.broadcasted_iota(jnp.int32, (BF16_SUBLANES, half), 1) + 1
    row = lax.broadcasted_iota(jnp.int32, (BF16_SUBLANES, half), 0)
    alt = jnp.where(row == 0, jnp.where(n1 % 2 == 0, 1.0, -1.0), 0.0)
    alt = (alt * jnp.where(n1 == half, 0.5, 1.0) * seq ** -0.5).astype(BF16)
    return chan.astype(BF16), pos_cos, pos_sin, alt


def _select(rows, cols, offset):
    r = lax.broadcasted_iota(jnp.int32, (rows, cols), 0)
    c = lax.broadcasted_iota(jnp.int32, (rows, cols), 1)
    return jnp.where(c == offset - r, 1.0, 0.0).astype(BF16)


def _fold_kernel(u_ref, t_ref, ae_ref, bo_ref, a0_ref, *, seq):
    half = seq // 2
    t = t_ref[...]
    r = lax.broadcasted_iota(jnp.int32, (FOLD_BLK, FOLD_BLK), 0)
    c = lax.broadcasted_iota(jnp.int32, (FOLD_BLK, FOLD_BLK), 1)
    shift_one = jnp.where(c == r + 1, 1.0, 0.0).astype(BF16)
    last_row = lax.broadcasted_iota(jnp.int32, (FOLD_BLK, FOURIER_WIDTH), 0) == FOLD_BLK - 1
    reverse = _select(FOLD_BLK, FOLD_BLK, FOLD_BLK - 1)
    for i in range(half // FOLD_BLK):
        lo = i * FOLD_BLK
        nxt = u_ref[lo + FOLD_BLK:lo + FOLD_BLK + BF16_SUBLANES, :][0:1, :].astype(F32)
        fwd = jnp.where(last_row, nxt, jnp.dot(shift_one, u_ref[lo:lo + FOLD_BLK, :], preferred_element_type=F32))
        fwd = fwd.astype(BF16)
        hi = seq - lo - FOLD_BLK
        bwd = jnp.dot(reverse, u_ref[hi:hi + FOLD_BLK, :], preferred_element_type=F32).astype(BF16)
        for gi in range(N_FGROUPS):
            cols = slice(gi * FGROUP_DIM, (gi + 1) * FGROUP_DIM)
            both = jnp.concatenate([fwd[:, cols], bwd[:, cols]], axis=1)
            res = jnp.dot(both, t, preferred_element_type=F32)
            ae_ref[lo:lo + FOLD_BLK, cols] = res[:, :FGROUP_DIM].astype(ae_ref.dtype)
            bo_ref[lo:lo + FOLD_BLK, cols] = res[:, FGROUP_DIM:].astype(bo_ref.dtype)
    for gi in range(N_FGROUPS):
        cols = slice(gi * FGROUP_DIM, (gi + 1) * FGROUP_DIM)
        first = jnp.dot(u_ref[0:BF16_SUBLANES, cols], t[:FGROUP_DIM, :FGROUP_DIM], preferred_element_type=F32)
        a0_ref[:, cols] = jnp.broadcast_to(first[0:1, :], (a0_ref.shape[0], FGROUP_DIM))


def _fold_call(u, chan_tab, batch, seq):
    half = seq // 2
    out = jax.ShapeDtypeStruct((batch * half, FOURIER_WIDTH), BF16)
    return pl.pallas_call(
        functools.partial(_fold_kernel, seq=seq),
        grid=(batch,),
        in_specs=[pl.BlockSpec((seq, FOURIER_WIDTH), lambda b: (b, 0)),
                  pl.BlockSpec((2 * FGROUP_DIM, 2 * FGROUP_DIM), lambda b: (0, 0))],
        out_specs=[pl.BlockSpec((half, FOURIER_WIDTH), lambda b: (b, 0)),
                   pl.BlockSpec((half, FOURIER_WIDTH), lambda b: (b, 0)),
                   pl.BlockSpec((None, F32_SUBLANES, FOURIER_WIDTH), lambda b: (b, 0, 0))],
        out_shape=[out, out, jax.ShapeDtypeStruct((batch, F32_SUBLANES, FOURIER_WIDTH), F32)],
        compiler_params=_params("parallel"),
        name="dft_fold",
    )(u, chan_tab)


def _pos_dft_kernel(c_ref, s_ref, alt_ref, ae_ref, bo_ref, a0_ref, o_ref, *, seq):
    half = seq // 2
    ae = ae_ref[...]
    dc = a0_ref[0:1, :] * seq ** -0.5
    even = jnp.dot(c_ref[...], ae, preferred_element_type=F32) + dc
    odd = jnp.dot(s_ref[...], bo_ref[...], preferred_element_type=F32)
    o_ref[0:half, :] = (even - odd).astype(o_ref.dtype)
    mirror = (even + odd).astype(BF16)
    middle = jnp.dot(alt_ref[...], ae, preferred_element_type=F32)[0:1, :] + dc
    take = _select(FOLD_BLK, FOLD_BLK, FOLD_BLK)
    first_row = lax.broadcasted_iota(jnp.int32, (FOLD_BLK, FOURIER_WIDTH), 0) == 0
    nblk = half // FOLD_BLK
    for bt in range(nblk):
        head = middle if bt == 0 else mirror[(nblk - bt) * FOLD_BLK:(nblk - bt) * FOLD_BLK + 1, :].astype(F32)
        lower = mirror[(nblk - 1 - bt) * FOLD_BLK:(nblk - bt) * FOLD_BLK, :]
        blk = jnp.where(first_row, head, jnp.dot(take, lower, preferred_element_type=F32))
        o_ref[half + bt * FOLD_BLK:half + (bt + 1) * FOLD_BLK, :] = blk.astype(o_ref.dtype)


def _pos_dft_call(pos_cos, pos_sin, alt, ae, bo, a0, batch, seq):
    half = seq // 2
    const = dict(pipeline_mode=pl.Buffered(1))
    return pl.pallas_call(
        functools.partial(_pos_dft_kernel, seq=seq),
        grid=(batch,),
        in_specs=[pl.BlockSpec((half, half), lambda b: (0, 0), **const),
                  pl.BlockSpec((half, half), lambda b: (0, 0), **const),
                  pl.BlockSpec((BF16_SUBLANES, half), lambda b: (0, 0)),
                  pl.BlockSpec((half, FOURIER_WIDTH), lambda b: (b, 0)),
                  pl.BlockSpec((half, FOURIER_WIDTH), lambda b: (b, 0)),
                  pl.BlockSpec((None, F32_SUBLANES, FOURIER_WIDTH), lambda b: (b, 0, 0))],
        out_specs=pl.BlockSpec((seq, FOURIER_WIDTH), lambda b: (b, 0)),
        out_shape=jax.ShapeDtypeStruct((batch * seq, FOURIER_WIDTH), BF16),
        compiler_params=_params("parallel"),
        name="pos_dft",
    )(pos_cos, pos_sin, alt, ae, bo, a0)


FFN_CHUNKS = (1536, 1280)
TAIL_SUBTILES = 2


def _tail_kernel(h_ref, a_ref, f_ref, wf_ref, bf_ref, wo_ref, g1_ref, b1_ref, wgu_ref, wd_ref, g2_ref, b2_ref,
                 o_ref):
    sub = h_ref.shape[0] // TAIL_SUBTILES
    groups = [slice(s * sub, (s + 1) * sub) for s in range(TAIL_SUBTILES)]

    def mixed(rows):
        parts = []
        for gi in range(N_FGROUPS):
            cols = slice(gi * FGROUP_DIM, (gi + 1) * FGROUP_DIM)
            parts.append(jnp.dot(f_ref[rows, cols], wf_ref[gi], preferred_element_type=F32))
        y = jnp.concatenate(parts, axis=-1) + bf_ref[...]
        mix = jnp.dot(a_ref[rows, :], wo_ref[:ATT_WIDTH, :], preferred_element_type=F32)
        return mix + jnp.dot(y.astype(BF16), wo_ref[ATT_WIDTH:, :], preferred_element_type=F32)

    def swiglu(h1):
        hb = h1.astype(BF16)
        acc = None
        lo = 0
        for width in FFN_CHUNKS:
            gate = jnp.dot(hb, wgu_ref[:, lo:lo + width], preferred_element_type=F32)
            up = jnp.dot(hb, wgu_ref[:, D_FF + lo:D_FF + lo + width], preferred_element_type=F32)
            act = (gate * jax.nn.sigmoid(gate) * up).astype(BF16)
            part = jnp.dot(act, wd_ref[lo:lo + width, :], preferred_element_type=F32)
            acc = part if acc is None else acc + part
            lo += width
        return acc

    mixes = [mixed(rows) for rows in groups]
    h1s = [_layer_norm(ALPHA * h_ref[rows, :] + mix, g1_ref[...], b1_ref[...]) for rows, mix in zip(groups, mixes)]
    ffns = [swiglu(h1) for h1 in h1s]
    for rows, h1, ffn in zip(groups, h1s, ffns):
        o_ref[rows, :] = _layer_norm(ALPHA * h1 + ffn, g2_ref[...], b2_ref[...])


def _tail_call(h2, a, f, layer, wf_bf, b_f, wo_bf, g1, b1, wgu_bf, wd_bf, g2, b2, tm=512):
    m, d = h2.shape
    assert sum(FFN_CHUNKS) == D_FF
    const = dict(pipeline_mode=pl.Buffered(1))
    row = lambda i: (i, 0)
    fixed = lambda i: (0, 0)
    stacked = lambda i: (layer, 0, 0)
    return pl.pallas_call(
        _tail_kernel,
        grid=(m // tm,),
        in_specs=[pl.BlockSpec((tm, d), row),
                  pl.BlockSpec((tm, ATT_WIDTH), row),
                  pl.BlockSpec((tm, FOURIER_WIDTH), row),
                  pl.BlockSpec((None, N_FGROUPS, FGROUP_DIM, FGROUP_DIM), lambda i: (layer, 0, 0, 0), **const),
                  pl.BlockSpec((1, FOURIER_WIDTH), fixed),
                  pl.BlockSpec((None, d, d), stacked, **const),
                  pl.BlockSpec((1, d), fixed),
                  pl.BlockSpec((1, d), fixed),
                  pl.BlockSpec((None, d, 2 * D_FF), stacked, **const),
                  pl.BlockSpec((None, D_FF, d), stacked, **const),
                  pl.BlockSpec((1, d), fixed),
                  pl.BlockSpec((1, d), fixed)],
        out_specs=pl.BlockSpec((tm, d), row),
        out_shape=jax.ShapeDtypeStruct((m, d), F32),
        compiler_params=_params("parallel"),
        name="mix_ffn",
    )(h2, a, f, wf_bf, b_f.reshape(1, -1), wo_bf, g1.reshape(1, d), b1.reshape(1, d),
      wgu_bf, wd_bf, g2.reshape(1, d), b2.reshape(1, d))


def kernel(x, ln_in_g, ln_in_b, w_in, lam_params, subln_g, w_f, b_f, w_o, ln1_g, ln1_b, w_gu, w_down, ln2_g, ln2_b):
    batch, seq, d = x.shape
    assert d == D_MODEL and seq % (2 * ATT_TQ) == 0
    chan_tab, pos_cos, pos_sin, pos_alt = _dft_tables(seq)
    pos_cols = _position_columns(seq)
    w_in_bf, w_f_bf, w_o_bf = w_in.astype(BF16), w_f.astype(BF16), w_o.astype(BF16)
    w_gu_bf, w_down_bf = w_gu.astype(BF16), w_down.astype(BF16)
    h = x.reshape(batch * seq, d)
    for l in range(DEPTH):
        if l == 0:
            h, q, k, vtc, nrm, u = _proj_call(h, w_in_bf, l, batch, seq, ln=(ln_in_g, ln_in_b))
        else:
            q, k, vtc, nrm, u = _proj_call(h, w_in_bf, l, batch, seq)
        a = _attn_call(q, k, pos_cols, vtc, nrm, lam_params[l], subln_g[l], batch, seq, _lambda_init(l))
        fe, fo, f0 = _fold_call(u, chan_tab, batch, seq)
        f = _pos_dft_call(pos_cos, pos_sin, pos_alt, fe, fo, f0, batch, seq)
        h = _tail_call(h, a, f, l, w_f_bf, b_f[l], w_o_bf, ln1_g[l], ln1_b[l],
                       w_gu_bf, w_down_bf, ln2_g[l], ln2_b[l])
    return h.reshape(batch, seq, d)
```

```python
import functools
import math

import jax
import jax.numpy as jnp
from jax import lax
from jax.experimental import pallas as pl
from jax.experimental.pallas import tpu as pltpu

D_MODEL = 1024
DEPTH = 2
ATT_WIDTH = 512
FOURIER_WIDTH = 512
N_HEADS = 8
HEAD_DIM = 32
V_HEAD_DIM = 64
N_FGROUPS = 4
FGROUP_DIM = 128
D_FF = 2816
LN_EPS = 1e-5
SUBLN_EPS = 1e-5
ALPHA = (2.0 * DEPTH) ** 0.25
LOG2E = math.log2(math.e)

LANES = 128
F32_SUBLANES = 8
BF16_SUBLANES = 16
VMEM_LIMIT = 56 * 1024 * 1024

BF16 = jnp.bfloat16
F32 = jnp.float32


def _lambda_init(layer_idx):
    return 0.8 - 0.6 * math.exp(-0.3 * layer_idx)


def _layer_norm(x, g, b):
    mu = jnp.mean(x, axis=-1, keepdims=True)
    xc = x - mu
    var = jnp.mean(xc * xc, axis=-1, keepdims=True)
    return xc * lax.rsqrt(var + LN_EPS) * g + b


def _params(*sem):
    return pltpu.CompilerParams(dimension_semantics=sem, vmem_limit_bytes=VMEM_LIMIT)


def _split3(x):
    hi = x.astype(BF16).astype(F32)
    r = x - hi
    mid = r.astype(BF16).astype(F32)
    lo = (r - mid).astype(BF16).astype(F32)
    return hi, mid, lo


ATT_TQ = 256
ATT_RANGE = 8
HEADS_PER_STEP = LANES // V_HEAD_DIM
CHAINS_PER_STEP = 2 * HEADS_PER_STEP
VT_ROWS = 128
KP_WIDTH = 2 * LANES
NORM_ROUND_UP = 1.05
NORM_ROWS = F32_SUBLANES
MAX_UNSHIFTED_SCORE = 50.0


def _position_columns(seq):
    def keep_bf16_bits(x):
        bits = lax.bitcast_convert_type(x, jnp.uint32) & jnp.uint32(0xFFFF0000)
        return lax.bitcast_convert_type(bits, F32)

    pos = lax.broadcasted_iota(jnp.int32, (seq, 1), 0)
    full = (pos - seq // 2).astype(F32) * LOG2E
    hi = keep_bf16_bits(full)
    mid = keep_bf16_bits(full - hi)
    lo = keep_bf16_bits(full - hi - mid)
    lane = lax.broadcasted_iota(jnp.int32, (seq, LANES), 1)
    aux = jnp.where(lane < 3, 1.0,
                    jnp.where(lane == 3, hi, jnp.where(lane == 4, mid, jnp.where(lane == 5, lo, 0.0))))
    return aux.astype(BF16)


def _slice_norms(x_bf):
    x32 = x_bf.astype(F32)
    seg = lax.broadcasted_iota(jnp.int32, (ATT_WIDTH, LANES), 0) // HEAD_DIM
    col = lax.broadcasted_iota(jnp.int32, (ATT_WIDTH, LANES), 1)
    ind = jnp.where(seg == col, 1.0, 0.0).astype(BF16)
    return jnp.dot((x32 * x32).astype(BF16), ind, preferred_element_type=F32) * NORM_ROUND_UP


def _row_max(x, groups=F32_SUBLANES):
    r, c = x.shape
    return jnp.max(jnp.max(x.reshape(groups, r // groups, c), axis=0), axis=0, keepdims=True)


def _proj_kernel(*refs, tm, pre_ln):
    if pre_ln:
        x_ref, g_ref, b_ref, w_ref, h_ref, q_ref, k_ref, vtc_ref, nrm_ref, u_ref = refs
        h = _layer_norm(x_ref[...], g_ref[...], b_ref[...])
        h_ref[...] = h
    else:
        x_ref, w_ref, q_ref, k_ref, vtc_ref, nrm_ref, u_ref = refs
        h = x_ref[...]
    hb = h.astype(BF16)
    qf = jnp.dot(hb, w_ref[:, 0:ATT_WIDTH], preferred_element_type=F32)
    qb = (qf * (HEAD_DIM ** -0.5 * LOG2E)).astype(BF16)
    q_ref[...] = qb

    kf = jnp.dot(hb, w_ref[:, ATT_WIDTH:2 * ATT_WIDTH], preferred_element_type=F32).astype(BF16)
    k_ref[...] = kf

    nrm_ref[0:NORM_ROWS, :] = jnp.broadcast_to(_row_max(_slice_norms(qb)), (NORM_ROWS, LANES))
    nrm_ref[NORM_ROWS:, :] = jnp.broadcast_to(_row_max(_slice_norms(kf)), (NORM_ROWS, LANES))

    vf = jnp.dot(hb, w_ref[:, 2 * ATT_WIDTH:3 * ATT_WIDTH], preferred_element_type=F32)
    vtr = vf.T.astype(BF16)
    for cc in range(tm // ATT_TQ):
        vtc_ref[cc] = vtr[:, cc * ATT_TQ:(cc + 1) * ATT_TQ]

    u_ref[...] = jnp.dot(hb, w_ref[:, 3 * ATT_WIDTH:], preferred_element_type=F32).astype(BF16)


def _proj_call(x2, w_bf, layer, batch, seq, ln=None, tm=1024):
    m, d = x2.shape
    n = w_bf.shape[2]
    tiles_per_seq = seq // tm
    kern = functools.partial(_proj_kernel, tm=tm, pre_ln=ln is not None)
    row = lambda i: (i, 0)
    fixed = lambda i: (0, 0)
    in_specs = [pl.BlockSpec((tm, d), row), pl.BlockSpec((None, d, n), lambda i: (layer, 0, 0))]
    args = [x2, w_bf]
    out_specs = [pl.BlockSpec((tm, ATT_WIDTH), row),
                 pl.BlockSpec((tm, ATT_WIDTH), row),
                 pl.BlockSpec((None, tm // ATT_TQ, ATT_WIDTH, ATT_TQ),
                              lambda i: (i // tiles_per_seq, i % tiles_per_seq, 0, 0)),
                 pl.BlockSpec((None, 2 * NORM_ROWS, LANES), lambda i: (i, 0, 0)),
                 pl.BlockSpec((tm, FOURIER_WIDTH), row)]
    out_shape = [jax.ShapeDtypeStruct((m, ATT_WIDTH), BF16),
                 jax.ShapeDtypeStruct((m, ATT_WIDTH), BF16),
                 jax.ShapeDtypeStruct((batch, seq // ATT_TQ, ATT_WIDTH, ATT_TQ), BF16),
                 jax.ShapeDtypeStruct((m // tm, 2 * NORM_ROWS, LANES), F32),
                 jax.ShapeDtypeStruct((m, FOURIER_WIDTH), BF16)]
    if ln is not None:
        in_specs[1:1] = [pl.BlockSpec((1, d), fixed), pl.BlockSpec((1, d), fixed)]
        args[1:1] = [ln[0].reshape(1, d), ln[1].reshape(1, d)]
        out_specs.insert(0, pl.BlockSpec((tm, d), row))
        out_shape.insert(0, jax.ShapeDtypeStruct((m, d), F32))
    return pl.pallas_call(
        kern,
        grid=(m // tm,),
        in_specs=in_specs,
        out_specs=out_specs,
        out_shape=out_shape,
        compiler_params=_params("parallel"),
        name="proj",
    )(*args)


def _attn_kernel(lam_ref, g_ref, nrm_ref, q_ref, k_ref, aux_ref, vtc_ref, o_ref, ks_ref, p_even_ref, p_odd_ref, flag_ref,
                 *,
                 lam_init, seq):
    pair = pl.program_id(1)
    ntiles = seq // ATT_TQ
    last = ntiles - 1
    last_rows = slice(last * ATT_TQ, seq)

    def tile_rows(tile):
        return pl.ds(pl.multiple_of(tile * ATT_TQ, ATT_TQ), ATT_TQ)

    def enter_tile(tile):
        prev = tile_rows(jnp.maximum(tile - 1, 0))
        ks_ref[prev, :LANES] = k_ref[prev, :]
        ks_ref[prev, LANES:] = aux_ref[prev, :]
        ks_ref[tile_rows(tile), :LANES] = k_ref[last_rows, :]
        ks_ref[tile_rows(tile), LANES:] = -aux_ref[last_rows, :]

    ks_ref[:, :LANES] = k_ref[...]
    ks_ref[:, LANES:] = -aux_ref[...]
    nrm = jnp.max(nrm_ref[...], axis=0)
    lane = lax.broadcasted_iota(jnp.int32, (1, LANES), 1)
    prod = jnp.where(lane // CHAINS_PER_STEP == pair, nrm[0:1, :] * nrm[NORM_ROWS:NORM_ROWS + 1, :], 0.0)
    flag_ref[0] = (jnp.max(prod) <= MAX_UNSHIFTED_SCORE ** 2).astype(jnp.int32)

    lp = lam_ref[...]
    lam = (jnp.exp(jnp.sum(lp[0:1] * lp[1:2], axis=-1, keepdims=True))
           - jnp.exp(jnp.sum(lp[2:3] * lp[3:4], axis=-1, keepdims=True)) + lam_init)

    rr = lax.broadcasted_iota(jnp.int32, (ATT_TQ, ATT_TQ), 0)
    cc = lax.broadcasted_iota(jnp.int32, (ATT_TQ, ATT_TQ), 1)
    diag_dist = jnp.abs(rr - cc).astype(F32) * LOG2E
    diag_dist = jnp.concatenate([diag_dist, diag_dist], axis=1)
    row_q = lax.broadcasted_iota(jnp.int32, (LANES, ATT_TQ), 0)
    ones_rows = jnp.ones((VT_ROWS - V_HEAD_DIM, ATT_TQ), BF16)
    g_col = g_ref[...]

    def tile_operands(tile):
        q_t = q_ref[tile_rows(tile), :].astype(F32).T
        pos = tile * ATT_TQ + lax.broadcasted_iota(jnp.int32, (1, ATT_TQ), 1)
        hi, mid, lo = _split3((pos - seq // 2).astype(F32) * LOG2E)
        base = jnp.where(row_q == 0, -hi,
                         jnp.where(row_q == 1, -mid,
                                   jnp.where(row_q == 2, -lo, jnp.where(row_q < 6, 1.0, 0.0))))
        return q_t, base

    def stage_operands(tile, hh, q_t, base):
        head = pair * HEADS_PER_STEP + hh
        slope = jnp.exp2(-jnp.full((1, 1), head + 1, jnp.int32).astype(F32))
        aux = (base * slope).astype(BF16)
        qm = jnp.concatenate([jnp.where(row_q // HEAD_DIM == hh * 2 + c, q_t, 0.0).astype(BF16) for c in range(2)],
                             axis=1)
        w_main = jnp.concatenate([qm, jnp.concatenate([aux, aux], axis=1)], axis=0)
        sd = jnp.dot(k_ref[tile_rows(tile), :], qm, preferred_element_type=F32)
        sd = sd - slope * diag_dist
        return w_main, sd

    def values_lhs(chunk, hh):
        return jnp.concatenate([vtc_ref[chunk, hh * V_HEAD_DIM:(hh + 1) * V_HEAD_DIM, :], ones_rows], axis=0)

    key_ranges = [(lo, min(lo + ATT_RANGE, last)) for lo in range(0, last, ATT_RANGE)]

    def attend(tile, q_t, base):
        accs = []
        for hh in range(HEADS_PER_STEP):
            w_main, sd = stage_operands(tile, hh, q_t, base)
            sts = [jnp.dot(ks_ref[lo * ATT_TQ:hi * ATT_TQ, :], w_main, preferred_element_type=F32)
                   for lo, hi in key_ranges]
            m = jnp.max(sd, axis=0, keepdims=True)
            for st in sts:
                m = jnp.maximum(m, jnp.max(st, axis=0, keepdims=True))
            acc = jnp.dot(values_lhs(tile, hh), jnp.exp2(sd - m).astype(BF16), preferred_element_type=F32)
            for (lo, _), st in zip(key_ranges, sts):
                p = jnp.exp2(st - m).astype(BF16)
                for j in range(st.shape[0] // ATT_TQ):
                    blk = lo + j
                    chunk = jnp.where(tile == blk, last, blk)
                    acc = acc + jnp.dot(values_lhs(chunk, hh), p[j * ATT_TQ:(j + 1) * ATT_TQ, :],
                                        preferred_element_type=F32)
            accs.append(acc)
        return tuple(accs)

    def tile_scores(tile, p_ref):
        q_t, base = tile_operands(tile)
        for hh in range(HEADS_PER_STEP):
            w_main, sd = stage_operands(tile, hh, q_t, base)
            for lo, hi in key_ranges:
                st = jnp.dot(ks_ref[lo * ATT_TQ:hi * ATT_TQ, :], w_main, preferred_element_type=F32)
                p_ref[hh, lo * ATT_TQ:hi * ATT_TQ, :] = jnp.exp2(st).astype(BF16)
            p_ref[hh, last_rows, :] = jnp.exp2(sd).astype(BF16)

    def tile_values(tile, p_ref):
        accs = []
        for hh in range(HEADS_PER_STEP):
            acc = jnp.dot(values_lhs(tile, hh), p_ref[hh, last_rows, :], preferred_element_type=F32)
            for blk in range(last):
                chunk = jnp.where(tile == blk, last, blk)
                acc = acc + jnp.dot(values_lhs(chunk, hh), p_ref[hh, blk * ATT_TQ:(blk + 1) * ATT_TQ, :],
                                    preferred_element_type=F32)
            accs.append(acc)
        return tuple(accs)

    def finish(tile, accs):
        outs = []
        for acc in accs:
            norm = []
            for c in range(2):
                part = acc[:, c * ATT_TQ:(c + 1) * ATT_TQ]
                denom = part[V_HEAD_DIM:V_HEAD_DIM + 1, :]
                norm.append(part[:V_HEAD_DIM, :] * (1.0 / denom))
            a_t = norm[0] - lam * norm[1]
            a_t = a_t * lax.rsqrt(jnp.mean(a_t * a_t, axis=0, keepdims=True) + SUBLN_EPS)
            outs.append(a_t * g_col * (1.0 - lam_init))
        o_ref[tile_rows(tile), :] = jnp.concatenate(outs, axis=0).T.astype(o_ref.dtype)

    dummy = tuple(jnp.ones((VT_ROWS, 2 * ATT_TQ), F32) for _ in range(HEADS_PER_STEP))

    @pl.when(flag_ref[0] == 1)
    def _():
        def step(tile, accs, p_new, p_old):
            finish(jnp.maximum(tile - 2, 0), accs)
            enter_tile(tile)
            tile_scores(tile, p_new)
            return tile_values(tile - 1, p_old)

        enter_tile(0)
        tile_scores(0, p_even_ref)

        def body(i, accs):
            odd = 2 * i + 1
            accs = step(odd, accs, p_odd_ref, p_even_ref)
            return step(odd + 1, accs, p_even_ref, p_odd_ref)

        accs = lax.fori_loop(0, (ntiles - 2) // 2, body, dummy)
        accs = step(last, accs, p_odd_ref, p_even_ref)
        finish(last - 1, accs)
        finish(last, tile_values(last, p_odd_ref))

    @pl.when(flag_ref[0] != 1)
    def _():
        def body(tile, accs):
            finish(jnp.maximum(tile - 1, 0), accs)
            enter_tile(tile)
            q_t, base = tile_operands(tile)
            return attend(tile, q_t, base)

        finish(last, lax.fori_loop(0, ntiles, body, dummy))


def _attn_call(q, k, pos_cols, vtc, nrm, lam_params, subln_g, batch, seq, lam_init):
    n_pairs = N_HEADS // HEADS_PER_STEP
    tiles_per_seq = nrm.shape[0] // batch
    kern = functools.partial(_attn_kernel, lam_init=lam_init, seq=seq)
    pair_rows = HEADS_PER_STEP * V_HEAD_DIM
    return pl.pallas_call(
        kern,
        grid=(batch, n_pairs),
        in_specs=[pl.BlockSpec((4, HEAD_DIM), lambda b, p: (0, 0)),
                  pl.BlockSpec((V_HEAD_DIM, 1), lambda b, p: (0, 0)),
                  pl.BlockSpec((tiles_per_seq, 2 * NORM_ROWS, LANES), lambda b, p: (b, 0, 0)),
                  pl.BlockSpec((seq, LANES), lambda b, p: (b, p)),
                  pl.BlockSpec((seq, LANES), lambda b, p: (b, p)),
                  pl.BlockSpec((seq, LANES), lambda b, p: (0, 0)),
                  pl.BlockSpec((None, seq // ATT_TQ, pair_rows, ATT_TQ), lambda b, p: (b, 0, p, 0))],
        out_specs=pl.BlockSpec((seq, LANES), lambda b, p: (b, p)),
        out_shape=jax.ShapeDtypeStruct((batch * seq, ATT_WIDTH), BF16),
        scratch_shapes=[pltpu.VMEM((seq, KP_WIDTH), BF16),
                        pltpu.VMEM((HEADS_PER_STEP, seq, 2 * ATT_TQ), BF16),
                        pltpu.VMEM((HEADS_PER_STEP, seq, 2 * ATT_TQ), BF16),
                        pltpu.SMEM((1,), jnp.int32)],
        compiler_params=_params("parallel", "parallel"),
        name="diff_attn",
    )(lam_params, subln_g.reshape(V_HEAD_DIM, 1), nrm, q, k, pos_cols, vtc)


FOLD_BLK = 256
TABLE_SPLIT = 64


def _dft_tables(seq):
    c = FGROUP_DIM
    ci = lax.broadcasted_iota(jnp.int32, (c, c), 0) * lax.broadcasted_iota(jnp.int32, (c, c), 1) % c
    ang_c = ci.astype(F32) * (2.0 * math.pi / c)
    cc, sc = jnp.cos(ang_c) * c ** -0.5, jnp.sin(ang_c) * c ** -0.5
    chan = jnp.concatenate([jnp.concatenate([cc, sc], axis=1), jnp.concatenate([cc, -sc], axis=1)], axis=0)
    half = seq // 2
    n = lax.broadcasted_iota(jnp.int32, (1, half), 1) + 1
    weight = jnp.where(n == half, 0.5, 1.0) * seq ** -0.5

    def thin(count, stride):
        kk = lax.broadcasted_iota(jnp.int32, (count, half), 0) * stride
        ang = (kk * n % seq).astype(F32) * (2.0 * math.pi / seq)
        return jnp.cos(ang), jnp.sin(ang)

    ca, sa = thin(half // TABLE_SPLIT, TABLE_SPLIT)
    cb, sb = thin(TABLE_SPLIT, 1)
    cb, sb = cb * weight, sb * weight
    pos_cos = (ca[:, None, :] * cb[None] - sa[:, None, :] * sb[None]).reshape(half, half).astype(BF16)
    pos_sin = (sa[:, None, :] * cb[None] + ca[:, None, :] * sb[None]).reshape(half, half).astype(BF16)
    n1 = lax.broadcasted_iota(jnp.int32, (BF16_SUBLANES, half), 1) + 1
    row = lax.broadcasted_iota(jnp.int32, (BF16_SUBLANES, half), 0)
    alt = jnp.where(row == 0, jnp.where(n1 % 2 == 0, 1.0, -1.0), 0.0)
    alt = (alt * jnp.where(n1 == half, 0.5, 1.0) * seq ** -0.5).astype(BF16)
    return chan.astype(BF16), pos_cos, pos_sin, alt


def _select(rows, cols, offset):
    r = lax.broadcasted_iota(jnp.int32, (rows, cols), 0)
    c = lax.broadcasted_iota(jnp.int32, (rows, cols), 1)
    return jnp.where(c == offset - r, 1.0, 0.0).astype(BF16)


def _fold_kernel(u_ref, t_ref, ae_ref, bo_ref, a0_ref, *, seq):
    half = seq // 2
    t = t_ref[...]
    r = lax.broadcasted_iota(jnp.int32, (FOLD_BLK, FOLD_BLK), 0)
    c = lax.broadcasted_iota(jnp.int32, (FOLD_BLK, FOLD_BLK), 1)
    shift_one = jnp.where(c == r + 1, 1.0, 0.0).astype(BF16)
    last_row = lax.broadcasted_iota(jnp.int32, (FOLD_BLK, FOURIER_WIDTH), 0) == FOLD_BLK - 1
    reverse = _select(FOLD_BLK, FOLD_BLK, FOLD_BLK - 1)
    for i in range(half // FOLD_BLK):
        lo = i * FOLD_BLK
        nxt = u_ref[lo + FOLD_BLK:lo + FOLD_BLK + BF16_SUBLANES, :][0:1, :].astype(F32)
        fwd = jnp.where(last_row, nxt, jnp.dot(shift_one, u_ref[lo:lo + FOLD_BLK, :], preferred_element_type=F32))
        fwd = fwd.astype(BF16)
        hi = seq - lo - FOLD_BLK
        bwd = jnp.dot(reverse, u_ref[hi:hi + FOLD_BLK, :], preferred_element_type=F32).astype(BF16)
        for gi in range(N_FGROUPS):
            cols = slice(gi * FGROUP_DIM, (gi + 1) * FGROUP_DIM)
            both = jnp.concatenate([fwd[:, cols], bwd[:, cols]], axis=1)
            res = jnp.dot(both, t, preferred_element_type=F32)
            ae_ref[lo:lo + FOLD_BLK, cols] = res[:, :FGROUP_DIM].astype(ae_ref.dtype)
            bo_ref[lo:lo + FOLD_BLK, cols] = res[:, FGROUP_DIM:].astype(bo_ref.dtype)
    for gi in range(N_FGROUPS):
        cols = slice(gi * FGROUP_DIM, (gi + 1) * FGROUP_DIM)
        first = jnp.dot(u_ref[0:BF16_SUBLANES, cols], t[:FGROUP_DIM, :FGROUP_DIM], preferred_element_type=F32)
        a0_ref[:, cols] = jnp.broadcast_to(first[0:1, :], (a0_ref.shape[0], FGROUP_DIM))


def _fold_call(u, chan_tab, batch, seq):
    half = seq // 2
    out = jax.ShapeDtypeStruct((batch * half, FOURIER_WIDTH), BF16)
    return pl.pallas_call(
        functools.partial(_fold_kernel, seq=seq),
        grid=(batch,),
        in_specs=[pl.BlockSpec((seq, FOURIER_WIDTH), lambda b: (b, 0)),
                  pl.BlockSpec((2 * FGROUP_DIM, 2 * FGROUP_DIM), lambda b: (0, 0))],
        out_specs=[pl.BlockSpec((half, FOURIER_WIDTH), lambda b: (b, 0)),
                   pl.BlockSpec((half, FOURIER_WIDTH), lambda b: (b, 0)),
                   pl.BlockSpec((None, F32_SUBLANES, FOURIER_WIDTH), lambda b: (b, 0, 0))],
        out_shape=[out, out, jax.ShapeDtypeStruct((batch, F32_SUBLANES, FOURIER_WIDTH), F32)],
        compiler_params=_params("parallel"),
        name="dft_fold",
    )(u, chan_tab)


def _pos_dft_kernel(c_ref, s_ref, alt_ref, ae_ref, bo_ref, a0_ref, o_ref, *, seq):
    half = seq // 2
    ae = ae_ref[...]
    dc = a0_ref[0:1, :] * seq ** -0.5
    even = jnp.dot(c_ref[...], ae, preferred_element_type=F32) + dc
    odd = jnp.dot(s_ref[...], bo_ref[...], preferred_element_type=F32)
    o_ref[0:half, :] = (even - odd).astype(o_ref.dtype)
    mirror = (even + odd).astype(BF16)
    middle = jnp.dot(alt_ref[...], ae, preferred_element_type=F32)[0:1, :] + dc
    take = _select(FOLD_BLK, FOLD_BLK, FOLD_BLK)
    first_row = lax.broadcasted_iota(jnp.int32, (FOLD_BLK, FOURIER_WIDTH), 0) == 0
    nblk = half // FOLD_BLK
    for bt in range(nblk):
        head = middle if bt == 0 else mirror[(nblk - bt) * FOLD_BLK:(nblk - bt) * FOLD_BLK + 1, :].astype(F32)
        lower = mirror[(nblk - 1 - bt) * FOLD_BLK:(nblk - bt) * FOLD_BLK, :]
        blk = jnp.where(first_row, head, jnp.dot(take, lower, preferred_element_type=F32))
        o_ref[half + bt * FOLD_BLK:half + (bt + 1) * FOLD_BLK, :] = blk.astype(o_ref.dtype)


def _pos_dft_call(pos_cos, pos_sin, alt, ae, bo, a0, batch, seq):
    half = seq // 2
    const = dict(pipeline_mode=pl.Buffered(1))
    return pl.pallas_call(
        functools.partial(_pos_dft_kernel, seq=seq),
        grid=(batch,),
        in_specs=[pl.BlockSpec((half, half), lambda b: (0, 0), **const),
                  pl.BlockSpec((half, half), lambda b: (0, 0), **const),
                  pl.BlockSpec((BF16_SUBLANES, half), lambda b: (0, 0)),
                  pl.BlockSpec((half, FOURIER_WIDTH), lambda b: (b, 0)),
                  pl.BlockSpec((half, FOURIER_WIDTH), lambda b: (b, 0)),
                  pl.BlockSpec((None, F32_SUBLANES, FOURIER_WIDTH), lambda b: (b, 0, 0))],
        out_specs=pl.BlockSpec((seq, FOURIER_WIDTH), lambda b: (b, 0)),
        out_shape=jax.ShapeDtypeStruct((batch * seq, FOURIER_WIDTH), BF16),
        compiler_params=_params("parallel"),
        name="pos_dft",
    )(pos_cos, pos_sin, alt, ae, bo, a0)


FFN_CHUNKS = (1536, 1280)
TAIL_SUBTILES = 2


def _tail_kernel(h_ref, a_ref, f_ref, wf_ref, bf_ref, wo_ref, g1_ref, b1_ref, wgu_ref, wd_ref, g2_ref, b2_ref,
                 o_ref):
    sub = h_ref.shape[0] // TAIL_SUBTILES
    groups = [slice(s * sub, (s + 1) * sub) for s in range(TAIL_SUBTILES)]

    def mixed(rows):
        parts = []
        for gi in range(N_FGROUPS):
            cols = slice(gi * FGROUP_DIM, (gi + 1) * FGROUP_DIM)
            parts.append(jnp.dot(f_ref[rows, cols], wf_ref[gi], preferred_element_type=F32))
        y = jnp.concatenate(parts, axis=-1) + bf_ref[...]
        mix = jnp.dot(a_ref[rows, :], wo_ref[:ATT_WIDTH, :], preferred_element_type=F32)
        return mix + jnp.dot(y.astype(BF16), wo_ref[ATT_WIDTH:, :], preferred_element_type=F32)

    def swiglu(h1):
        hb = h1.astype(BF16)
        acc = None
        lo = 0
        for width in FFN_CHUNKS:
            gate = jnp.dot(hb, wgu_ref[:, lo:lo + width], preferred_element_type=F32)
            up = jnp.dot(hb, wgu_ref[:, D_FF + lo:D_FF + lo + width], preferred_element_type=F32)
            act = (gate * jax.nn.sigmoid(gate) * up).astype(BF16)
            part = jnp.dot(act, wd_ref[lo:lo + width, :], preferred_element_type=F32)
            acc = part if acc is None else acc + part
            lo += width
        return acc

    mixes = [mixed(rows) for rows in groups]
    h1s = [_layer_norm(ALPHA * h_ref[rows, :] + mix, g1_ref[...], b1_ref[...]) for rows, mix in zip(groups, mixes)]
    ffns = [swiglu(h1) for h1 in h1s]
    for rows, h1, ffn in zip(groups, h1s, ffns):
        o_ref[rows, :] = _layer_norm(ALPHA * h1 + ffn, g2_ref[...], b2_ref[...])


def _tail_call(h2, a, f, layer, wf_bf, b_f, wo_bf, g1, b1, wgu_bf, wd_bf, g2, b2, tm=512):
    m, d = h2.shape
    assert sum(FFN_CHUNKS) == D_FF
    const = dict(pipeline_mode=pl.Buffered(1))
    row = lambda i: (i, 0)
    fixed = lambda i: (0, 0)
    stacked = lambda i: (layer, 0, 0)
    return pl.pallas_call(
        _tail_kernel,
        grid=(m // tm,),
        in_specs=[pl.BlockSpec((tm, d), row),
                  pl.BlockSpec((tm, ATT_WIDTH), row),
                  pl.BlockSpec((tm, FOURIER_WIDTH), row),
                  pl.BlockSpec((None, N_FGROUPS, FGROUP_DIM, FGROUP_DIM), lambda i: (layer, 0, 0, 0), **const),
                  pl.BlockSpec((1, FOURIER_WIDTH), fixed),
                  pl.BlockSpec((None, d, d), stacked, **const),
                  pl.BlockSpec((1, d), fixed),
                  pl.BlockSpec((1, d), fixed),
                  pl.BlockSpec((None, d, 2 * D_FF), stacked, **const),
                  pl.BlockSpec((None, D_FF, d), stacked, **const),
                  pl.BlockSpec((1, d), fixed),
                  pl.BlockSpec((1, d), fixed)],
        out_specs=pl.BlockSpec((tm, d), row),
        out_shape=jax.ShapeDtypeStruct((m, d), F32),
        compiler_params=_params("parallel"),
        name="mix_ffn",
    )(h2, a, f, wf_bf, b_f.reshape(1, -1), wo_bf, g1.reshape(1, d), b1.reshape(1, d),
      wgu_bf, wd_bf, g2.reshape(1, d), b2.reshape(1, d))


def kernel(x, ln_in_g, ln_in_b, w_in, lam_params, subln_g, w_f, b_f, w_o, ln1_g, ln1_b, w_gu, w_down, ln2_g, ln2_b):
    batch, seq, d = x.shape
    assert d == D_MODEL and seq % (2 * ATT_TQ) == 0
    chan_tab, pos_cos, pos_sin, pos_alt = _dft_tables(seq)
    pos_cols = _position_columns(seq)
    w_in_bf, w_f_bf, w_o_bf = w_in.astype(BF16), w_f.astype(BF16), w_o.astype(BF16)
    w_gu_bf, w_down_bf = w_gu.astype(BF16), w_down.astype(BF16)
    h = x.reshape(batch * seq, d)
    for l in range(DEPTH):
        if l == 0:
            h, q, k, vtc, nrm, u = _proj_call(h, w_in_bf, l, batch, seq, ln=(ln_in_g, ln_in_b))
        else:
            q, k, vtc, nrm, u = _proj_call(h, w_in_bf, l, batch, seq)
        a = _attn_call(q, k, pos_cols, vtc, nrm, lam_params[l], subln_g[l], batch, seq, _lambda_init(l))
        fe, fo, f0 = _fold_call(u, chan_tab, batch, seq)
        f = _pos_dft_call(pos_cos, pos_sin, pos_alt, fe, fo, f0, batch, seq)
        h = _tail_call(h, a, f, l, w_f_bf, b_f[l], w_o_bf, ln1_g[l], ln1_b[l],
                       w_gu_bf, w_down_bf, ln2_g[l], ln2_b[l])
    return h.reshape(batch, seq, d)
```

```python
import functools
import math

import jax
import jax.numpy as jnp
from jax import lax
from jax.experimental import pallas as pl
from jax.experimental.pallas import tpu as pltpu

D_MODEL = 1024
DEPTH = 2
ATT_WIDTH = 512
FOURIER_WIDTH = 512
N_HEADS = 8
HEAD_DIM = 32
V_HEAD_DIM = 64
N_FGROUPS = 4
FGROUP_DIM = 128
D_FF = 2816
LN_EPS = 1e-5
SUBLN_EPS = 1e-5
ALPHA = (2.0 * DEPTH) ** 0.25
LOG2E = math.log2(math.e)

LANES = 128
F32_SUBLANES = 8
BF16_SUBLANES = 16
VMEM_LIMIT = 56 * 1024 * 1024

BF16 = jnp.bfloat16
F32 = jnp.float32


def _lambda_init(layer_idx):
    return 0.8 - 0.6 * math.exp(-0.3 * layer_idx)


def _layer_norm(x, g, b):
    mu = jnp.mean(x, axis=-1, keepdims=True)
    xc = x - mu
    var = jnp.mean(xc * xc, axis=-1, keepdims=True)
    return xc * lax.rsqrt(var + LN_EPS) * g + b


def _params(*sem):
    return pltpu.CompilerParams(dimension_semantics=sem, vmem_limit_bytes=VMEM_LIMIT)


def _split3(x):
    hi = x.astype(BF16).astype(F32)
    r = x - hi
    mid = r.astype(BF16).astype(F32)
    lo = (r - mid).astype(BF16).astype(F32)
    return hi, mid, lo


ATT_TQ = 256
ATT_RANGE = 8
HEADS_PER_STEP = LANES // V_HEAD_DIM
CHAINS_PER_STEP = 2 * HEADS_PER_STEP
VT_ROWS = 96
KP_WIDTH = 2 * LANES
NORM_ROUND_UP = 1.05
NORM_ROWS = F32_SUBLANES
MAX_UNSHIFTED_SCORE = 50.0


def _position_columns(seq):
    def keep_bf16_bits(x):
        bits = lax.bitcast_convert_type(x, jnp.uint32) & jnp.uint32(0xFFFF0000)
        return lax.bitcast_convert_type(bits, F32)

    pos = lax.broadcasted_iota(jnp.int32, (seq, 1), 0)
    full = (pos - seq // 2).astype(F32) * LOG2E
    hi = keep_bf16_bits(full)
    mid = keep_bf16_bits(full - hi)
    lo = keep_bf16_bits(full - hi - mid)
    lane = lax.broadcasted_iota(jnp.int32, (seq, LANES), 1)
    aux = jnp.where(lane < 3, 1.0,
                    jnp.where(lane == 3, hi, jnp.where(lane == 4, mid, jnp.where(lane == 5, lo, 0.0))))
    return aux.astype(BF16)


def _slice_norms(x_bf):
    x32 = x_bf.astype(F32)
    seg = lax.broadcasted_iota(jnp.int32, (ATT_WIDTH, LANES), 0) // HEAD_DIM
    col = lax.broadcasted_iota(jnp.int32, (ATT_WIDTH, LANES), 1)
    ind = jnp.where(seg == col, 1.0, 0.0).astype(BF16)
    return jnp.dot((x32 * x32).astype(BF16), ind, preferred_element_type=F32) * NORM_ROUND_UP


def _row_max(x, groups=F32_SUBLANES):
    r, c = x.shape
    return jnp.max(jnp.max(x.reshape(groups, r // groups, c), axis=0), axis=0, keepdims=True)


def _proj_kernel(*refs, tm, pre_ln):
    if pre_ln:
        x_ref, g_ref, b_ref, w_ref, h_ref, q_ref, k_ref, vtc_ref, nrm_ref, u_ref = refs
        h = _layer_norm(x_ref[...], g_ref[...], b_ref[...])
        h_ref[...] = h
    else:
        x_ref, w_ref, q_ref, k_ref, vtc_ref, nrm_ref, u_ref = refs
        h = x_ref[...]
    hb = h.astype(BF16)
    qf = jnp.dot(hb, w_ref[:, 0:ATT_WIDTH], preferred_element_type=F32)
    qb = (qf * (HEAD_DIM ** -0.5 * LOG2E)).astype(BF16)
    q_ref[...] = qb

    kf = jnp.dot(hb, w_ref[:, ATT_WIDTH:2 * ATT_WIDTH], preferred_element_type=F32).astype(BF16)
    k_ref[...] = kf

    nrm_ref[0:NORM_ROWS, :] = jnp.broadcast_to(_row_max(_slice_norms(qb)), (NORM_ROWS, LANES))
    nrm_ref[NORM_ROWS:, :] = jnp.broadcast_to(_row_max(_slice_norms(kf)), (NORM_ROWS, LANES))

    vf = jnp.dot(hb, w_ref[:, 2 * ATT_WIDTH:3 * ATT_WIDTH], preferred_element_type=F32)
    vtr = vf.T.astype(BF16)
    for cc in range(tm // ATT_TQ):
        vtc_ref[cc] = vtr[:, cc * ATT_TQ:(cc + 1) * ATT_TQ]

    u_ref[...] = jnp.dot(hb, w_ref[:, 3 * ATT_WIDTH:], preferred_element_type=F32).astype(BF16)


def _proj_call(x2, w_bf, layer, batch, seq, ln=None, tm=1024):
    m, d = x2.shape
    n = w_bf.shape[2]
    tiles_per_seq = seq // tm
    kern = functools.partial(_proj_kernel, tm=tm, pre_ln=ln is not None)
    row = lambda i: (i, 0)
    fixed = lambda i: (0, 0)
    in_specs = [pl.BlockSpec((tm, d), row), pl.BlockSpec((None, d, n), lambda i: (layer, 0, 0))]
    args = [x2, w_bf]
    out_specs = [pl.BlockSpec((tm, ATT_WIDTH), row),
                 pl.BlockSpec((tm, ATT_WIDTH), row),
                 pl.BlockSpec((None, tm // ATT_TQ, ATT_WIDTH, ATT_TQ),
                              lambda i: (i // tiles_per_seq, i % tiles_per_seq, 0, 0)),
                 pl.BlockSpec((None, 2 * NORM_ROWS, LANES), lambda i: (i, 0, 0)),
                 pl.BlockSpec((tm, FOURIER_WIDTH), row)]
    out_shape = [jax.ShapeDtypeStruct((m, ATT_WIDTH), BF16),
                 jax.ShapeDtypeStruct((m, ATT_WIDTH), BF16),
                 jax.ShapeDtypeStruct((batch, seq // ATT_TQ, ATT_WIDTH, ATT_TQ), BF16),
                 jax.ShapeDtypeStruct((m // tm, 2 * NORM_ROWS, LANES), F32),
                 jax.ShapeDtypeStruct((m, FOURIER_WIDTH), BF16)]
    if ln is not None:
        in_specs[1:1] = [pl.BlockSpec((1, d), fixed), pl.BlockSpec((1, d), fixed)]
        args[1:1] = [ln[0].reshape(1, d), ln[1].reshape(1, d)]
        out_specs.insert(0, pl.BlockSpec((tm, d), row))
        out_shape.insert(0, jax.ShapeDtypeStruct((m, d), F32))
    return pl.pallas_call(
        kern,
        grid=(m // tm,),
        in_specs=in_specs,
        out_specs=out_specs,
        out_shape=out_shape,
        compiler_params=_params("parallel"),
        name="proj",
    )(*args)


def _attn_kernel(lam_ref, g_ref, nrm_ref, q_ref, k_ref, aux_ref, vtc_ref, o_ref, ks_ref, p_even_ref, p_odd_ref, flag_ref,
                 *,
                 lam_init, seq):
    pair = pl.program_id(1)
    ntiles = seq // ATT_TQ
    last = ntiles - 1
    last_rows = slice(last * ATT_TQ, seq)

    def tile_rows(tile):
        return pl.ds(pl.multiple_of(tile * ATT_TQ, ATT_TQ), ATT_TQ)

    def enter_tile(tile):
        prev = tile_rows(jnp.maximum(tile - 1, 0))
        ks_ref[prev, :LANES] = k_ref[prev, :]
        ks_ref[prev, LANES:] = aux_ref[prev, :]
        ks_ref[tile_rows(tile), :LANES] = k_ref[last_rows, :]
        ks_ref[tile_rows(tile), LANES:] = -aux_ref[last_rows, :]

    ks_ref[:, :LANES] = k_ref[...]
    ks_ref[:, LANES:] = -aux_ref[...]
    nrm = jnp.max(nrm_ref[...], axis=0)
    lane = lax.broadcasted_iota(jnp.int32, (1, LANES), 1)
    prod = jnp.where(lane // CHAINS_PER_STEP == pair, nrm[0:1, :] * nrm[NORM_ROWS:NORM_ROWS + 1, :], 0.0)
    flag_ref[0] = (jnp.max(prod) <= MAX_UNSHIFTED_SCORE ** 2).astype(jnp.int32)

    lp = lam_ref[...]
    lam = (jnp.exp(jnp.sum(lp[0:1] * lp[1:2], axis=-1, keepdims=True))
           - jnp.exp(jnp.sum(lp[2:3] * lp[3:4], axis=-1, keepdims=True)) + lam_init)

    rr = lax.broadcasted_iota(jnp.int32, (ATT_TQ, ATT_TQ), 0)
    cc = lax.broadcasted_iota(jnp.int32, (ATT_TQ, ATT_TQ), 1)
    diag_dist = jnp.abs(rr - cc).astype(F32) * LOG2E
    diag_dist = jnp.concatenate([diag_dist, diag_dist], axis=1)
    row_q = lax.broadcasted_iota(jnp.int32, (LANES, ATT_TQ), 0)
    ones_rows = jnp.ones((VT_ROWS - V_HEAD_DIM, ATT_TQ), BF16)
    g_col = g_ref[...]

    def tile_operands(tile):
        q_t = q_ref[tile_rows(tile), :].astype(F32).T
        pos = tile * ATT_TQ + lax.broadcasted_iota(jnp.int32, (1, ATT_TQ), 1)
        hi, mid, lo = _split3((pos - seq // 2).astype(F32) * LOG2E)
        base = jnp.where(row_q == 0, -hi,
                         jnp.where(row_q == 1, -mid,
                                   jnp.where(row_q == 2, -lo, jnp.where(row_q < 6, 1.0, 0.0))))
        return q_t, base

    def stage_operands(tile, hh, q_t, base):
        head = pair * HEADS_PER_STEP + hh
        slope = jnp.exp2(-jnp.full((1, 1), head + 1, jnp.int32).astype(F32))
        aux = (base * slope).astype(BF16)
        qm = jnp.concatenate([jnp.where(row_q // HEAD_DIM == hh * 2 + c, q_t, 0.0).astype(BF16) for c in range(2)],
                             axis=1)
        w_main = jnp.concatenate([qm, jnp.concatenate([aux, aux], axis=1)], axis=0)
        sd = jnp.dot(k_ref[tile_rows(tile), :], qm, preferred_element_type=F32)
        sd = sd - slope * diag_dist
        return w_main, sd

    def values_lhs(chunk, hh):
        return jnp.concatenate([vtc_ref[chunk, hh * V_HEAD_DIM:(hh + 1) * V_HEAD_DIM, :], ones_rows], axis=0)

    key_ranges = [(lo, min(lo + ATT_RANGE, last)) for lo in range(0, last, ATT_RANGE)]

    def attend(tile, q_t, base):
        accs = []
        for hh in range(HEADS_PER_STEP):
            w_main, sd = stage_operands(tile, hh, q_t, base)
            sts = [jnp.dot(ks_ref[lo * ATT_TQ:hi * ATT_TQ, :], w_main, preferred_element_type=F32)
                   for lo, hi in key_ranges]
            m = jnp.max(sd, axis=0, keepdims=True)
            for st in sts:
                m = jnp.maximum(m, jnp.max(st, axis=0, keepdims=True))
            acc = jnp.dot(values_lhs(tile, hh), jnp.exp2(sd - m).astype(BF16), preferred_element_type=F32)
            for (lo, _), st in zip(key_ranges, sts):
                p = jnp.exp2(st - m).astype(BF16)
                for j in range(st.shape[0] // ATT_TQ):
                    blk = lo + j
                    chunk = jnp.where(tile == blk, last, blk)
                    acc = acc + jnp.dot(values_lhs(chunk, hh), p[j * ATT_TQ:(j + 1) * ATT_TQ, :],
                                        preferred_element_type=F32)
            accs.append(acc)
        return tuple(accs)

    def tile_scores(tile, p_ref):
        q_t, base = tile_operands(tile)
        for hh in range(HEADS_PER_STEP):
            w_main, sd = stage_operands(tile, hh, q_t, base)
            for lo, hi in key_ranges:
                st = jnp.dot(ks_ref[lo * ATT_TQ:hi * ATT_TQ, :], w_main, preferred_element_type=F32)
                p_ref[hh, lo * ATT_TQ:hi * ATT_TQ, :] = jnp.exp2(st).astype(BF16)
            p_ref[hh, last_rows, :] = jnp.exp2(sd).astype(BF16)

    def tile_values(tile, p_ref):
        accs = []
        for hh in range(HEADS_PER_STEP):
            acc = jnp.dot(values_lhs(tile, hh), p_ref[hh, last_rows, :], preferred_element_type=F32)
            for blk in range(last):
                chunk = jnp.where(tile == blk, last, blk)
                acc = acc + jnp.dot(values_lhs(chunk, hh), p_ref[hh, blk * ATT_TQ:(blk + 1) * ATT_TQ, :],
                                    preferred_element_type=F32)
            accs.append(acc)
        return tuple(accs)

    def finish(tile, accs):
        outs = []
        for acc in accs:
            norm = []
            for c in range(2):
                part = acc[:, c * ATT_TQ:(c + 1) * ATT_TQ]
                denom = part[V_HEAD_DIM:V_HEAD_DIM + 1, :]
                norm.append(part[:V_HEAD_DIM, :] * (1.0 / denom))
            a_t = norm[0] - lam * norm[1]
            a_t = a_t * lax.rsqrt(jnp.mean(a_t * a_t, axis=0, keepdims=True) + SUBLN_EPS)
            outs.append(a_t * g_col * (1.0 - lam_init))
        o_ref[tile_rows(tile), :] = jnp.concatenate(outs, axis=0).T.astype(o_ref.dtype)

    dummy = tuple(jnp.ones((VT_ROWS, 2 * ATT_TQ), F32) for _ in range(HEADS_PER_STEP))

    @pl.when(flag_ref[0] == 1)
    def _():
        def step(tile, accs, p_new, p_old):
            finish(jnp.maximum(tile - 2, 0), accs)
            enter_tile(tile)
            tile_scores(tile, p_new)
            return tile_values(tile - 1, p_old)

        enter_tile(0)
        tile_scores(0, p_even_ref)

        def body(i, accs):
            odd = 2 * i + 1
            accs = step(odd, accs, p_odd_ref, p_even_ref)
            return step(odd + 1, accs, p_even_ref, p_odd_ref)

        accs = lax.fori_loop(0, (ntiles - 2) // 2, body, dummy)
        accs = step(last, accs, p_odd_ref, p_even_ref)
        finish(last - 1, accs)
        finish(last, tile_values(last, p_odd_ref))

    @pl.when(flag_ref[0] != 1)
    def _():
        def body(tile, accs):
            finish(jnp.maximum(tile - 1, 0), accs)
            enter_tile(tile)
            q_t, base = tile_operands(tile)
            return attend(tile, q_t, base)

        finish(last, lax.fori_loop(0, ntiles, body, dummy))


def _attn_call(q, k, pos_cols, vtc, nrm, lam_params, subln_g, batch, seq, lam_init):
    n_pairs = N_HEADS // HEADS_PER_STEP
    tiles_per_seq = nrm.shape[0] // batch
    kern = functools.partial(_attn_kernel, lam_init=lam_init, seq=seq)
    pair_rows = HEADS_PER_STEP * V_HEAD_DIM
    return pl.pallas_call(
        kern,
        grid=(batch, n_pairs),
        in_specs=[pl.BlockSpec((4, HEAD_DIM), lambda b, p: (0, 0)),
                  pl.BlockSpec((V_HEAD_DIM, 1), lambda b, p: (0, 0)),
                  pl.BlockSpec((tiles_per_seq, 2 * NORM_ROWS, LANES), lambda b, p: (b, 0, 0)),
                  pl.BlockSpec((seq, LANES), lambda b, p: (b, p)),
                  pl.BlockSpec((seq, LANES), lambda b, p: (b, p)),
                  pl.BlockSpec((seq, LANES), lambda b, p: (0, 0)),
                  pl.BlockSpec((None, seq // ATT_TQ, pair_rows, ATT_TQ), lambda b, p: (b, 0, p, 0))],
        out_specs=pl.BlockSpec((seq, LANES), lambda b, p: (b, p)),
        out_shape=jax.ShapeDtypeStruct((batch * seq, ATT_WIDTH), BF16),
        scratch_shapes=[pltpu.VMEM((seq, KP_WIDTH), BF16),
                        pltpu.VMEM((HEADS_PER_STEP, seq, 2 * ATT_TQ), BF16),
                        pltpu.VMEM((HEADS_PER_STEP, seq, 2 * ATT_TQ), BF16),
                        pltpu.SMEM((1,), jnp.int32)],
        compiler_params=_params("parallel", "parallel"),
        name="diff_attn",
    )(lam_params, subln_g.reshape(V_HEAD_DIM, 1), nrm, q, k, pos_cols, vtc)


FOLD_BLK = 256
TABLE_SPLIT = 64


def _dft_tables(seq):
    c = FGROUP_DIM
    ci = lax.broadcasted_iota(jnp.int32, (c, c), 0) * lax.broadcasted_iota(jnp.int32, (c, c), 1) % c
    ang_c = ci.astype(F32) * (2.0 * math.pi / c)
    cc, sc = jnp.cos(ang_c) * c ** -0.5, jnp.sin(ang_c) * c ** -0.5
    chan = jnp.concatenate([jnp.concatenate([cc, sc], axis=1), jnp.concatenate([cc, -sc], axis=1)], axis=0)
    half = seq // 2
    n = lax.broadcasted_iota(jnp.int32, (1, half), 1) + 1
    weight = jnp.where(n == half, 0.5, 1.0) * seq ** -0.5

    def thin(count, stride):
        kk = lax.broadcasted_iota(jnp.int32, (count, half), 0) * stride
        ang = (kk * n % seq).astype(F32) * (2.0 * math.pi / seq)
        return jnp.cos(ang), jnp.sin(ang)

    ca, sa = thin(half // TABLE_SPLIT, TABLE_SPLIT)
    cb, sb = thin(TABLE_SPLIT, 1)
    cb, sb = cb * weight, sb * weight
    pos_cos = (ca[:, None, :] * cb[None] - sa[:, None, :] * sb[None]).reshape(half, half).astype(BF16)
    pos_sin = (sa[:, None, :] * cb[None] + ca[:, None, :] * sb[None]).reshape(half, half).astype(BF16)
    n1 = lax.broadcasted_iota(jnp.int32, (BF16_SUBLANES, half), 1) + 1
    row = lax.broadcasted_iota(jnp.int32, (BF16_SUBLANES, half), 0)
    alt = jnp.where(row == 0, jnp.where(n1 % 2 == 0, 1.0, -1.0), 0.0)
    alt = (alt * jnp.where(n1 == half, 0.5, 1.0) * seq ** -0.5).astype(BF16)
    return chan.astype(BF16), pos_cos, pos_sin, alt


def _select(rows, cols, offset):
    r = lax.broadcasted_iota(jnp.int32, (rows, cols), 0)
    c = lax.broadcasted_iota(jnp.int32, (rows, cols), 1)
    return jnp.where(c == offset - r, 1.0, 0.0).astype(BF16)


def _fold_kernel(u_ref, t_ref, ae_ref, bo_ref, a0_ref, *, seq):
    half = seq // 2
    t = t_ref[...]
    r = lax.broadcasted_iota(jnp.int32, (FOLD_BLK, FOLD_BLK), 0)
    c = lax.broadcasted_iota(jnp.int32, (FOLD_BLK, FOLD_BLK), 1)
    shift_one = jnp.where(c == r + 1, 1.0, 0.0).astype(BF16)
    last_row = lax.broadcasted_iota(jnp.int32, (FOLD_BLK, FOURIER_WIDTH), 0) == FOLD_BLK - 1
    reverse = _select(FOLD_BLK, FOLD_BLK, FOLD_BLK - 1)
    for i in range(half // FOLD_BLK):
        lo = i * FOLD_BLK
        nxt = u_ref[lo + FOLD_BLK:lo + FOLD_BLK + BF16_SUBLANES, :][0:1, :].astype(F32)
        fwd = jnp.where(last_row, nxt, jnp.dot(shift_one, u_ref[lo:lo + FOLD_BLK, :], preferred_element_type=F32))
        fwd = fwd.astype(BF16)
        hi = seq - lo - FOLD_BLK
        bwd = jnp.dot(reverse, u_ref[hi:hi + FOLD_BLK, :], preferred_element_type=F32).astype(BF16)
        for gi in range(N_FGROUPS):
            cols = slice(gi * FGROUP_DIM, (gi + 1) * FGROUP_DIM)
            both = jnp.concatenate([fwd[:, cols], bwd[:, cols]], axis=1)
            res = jnp.dot(both, t, preferred_element_type=F32)
            ae_ref[lo:lo + FOLD_BLK, cols] = res[:, :FGROUP_DIM].astype(ae_ref.dtype)
            bo_ref[lo:lo + FOLD_BLK, cols] = res[:, FGROUP_DIM:].astype(bo_ref.dtype)
    for gi in range(N_FGROUPS):
        cols = slice(gi * FGROUP_DIM, (gi + 1) * FGROUP_DIM)
        first = jnp.dot(u_ref[0:BF16_SUBLANES, cols], t[:FGROUP_DIM, :FGROUP_DIM], preferred_element_type=F32)
        a0_ref[:, cols] = jnp.broadcast_to(first[0:1, :], (a0_ref.shape[0], FGROUP_DIM))


def _fold_call(u, chan_tab, batch, seq):
    half = seq // 2
    out = jax.ShapeDtypeStruct((batch * half, FOURIER_WIDTH), BF16)
    return pl.pallas_call(
        functools.partial(_fold_kernel, seq=seq),
        grid=(batch,),
        in_specs=[pl.BlockSpec((seq, FOURIER_WIDTH), lambda b: (b, 0)),
                  pl.BlockSpec((2 * FGROUP_DIM, 2 * FGROUP_DIM), lambda b: (0, 0))],
        out_specs=[pl.BlockSpec((half, FOURIER_WIDTH), lambda b: (b, 0)),
                   pl.BlockSpec((half, FOURIER_WIDTH), lambda b: (b, 0)),
                   pl.BlockSpec((None, F32_SUBLANES, FOURIER_WIDTH), lambda b: (b, 0, 0))],
        out_shape=[out, out, jax.ShapeDtypeStruct((batch, F32_SUBLANES, FOURIER_WIDTH), F32)],
        compiler_params=_params("parallel"),
        name="dft_fold",
    )(u, chan_tab)


def _pos_dft_kernel(c_ref, s_ref, alt_ref, ae_ref, bo_ref, a0_ref, o_ref, *, seq):
    half = seq // 2
    ae = ae_ref[...]
    dc = a0_ref[0:1, :] * seq ** -0.5
    even = jnp.dot(c_ref[...], ae, preferred_element_type=F32) + dc
    odd = jnp.dot(s_ref[...], bo_ref[...], preferred_element_type=F32)
    o_ref[0:half, :] = (even - odd).astype(o_ref.dtype)
    mirror = (even + odd).astype(BF16)
    middle = jnp.dot(alt_ref[...], ae, preferred_element_type=F32)[0:1, :] + dc
    take = _select(FOLD_BLK, FOLD_BLK, FOLD_BLK)
    first_row = lax.broadcasted_iota(jnp.int32, (FOLD_BLK, FOURIER_WIDTH), 0) == 0
    nblk = half // FOLD_BLK
    for bt in range(nblk):
        head = middle if bt == 0 else mirror[(nblk - bt) * FOLD_BLK:(nblk - bt) * FOLD_BLK + 1, :].astype(F32)
        lower = mirror[(nblk - 1 - bt) * FOLD_BLK:(nblk - bt) * FOLD_BLK, :]
        blk = jnp.where(first_row, head, jnp.dot(take, lower, preferred_element_type=F32))
        o_ref[half + bt * FOLD_BLK:half + (bt + 1) * FOLD_BLK, :] = blk.astype(o_ref.dtype)


def _pos_dft_call(pos_cos, pos_sin, alt, ae, bo, a0, batch, seq):
    half = seq // 2
    const = dict(pipeline_mode=pl.Buffered(1))
    return pl.pallas_call(
        functools.partial(_pos_dft_kernel, seq=seq),
        grid=(batch,),
        in_specs=[pl.BlockSpec((half, half), lambda b: (0, 0), **const),
                  pl.BlockSpec((half, half), lambda b: (0, 0), **const),
                  pl.BlockSpec((BF16_SUBLANES, half), lambda b: (0, 0)),
                  pl.BlockSpec((half, FOURIER_WIDTH), lambda b: (b, 0)),
                  pl.BlockSpec((half, FOURIER_WIDTH), lambda b: (b, 0)),
                  pl.BlockSpec((None, F32_SUBLANES, FOURIER_WIDTH), lambda b: (b, 0, 0))],
        out_specs=pl.BlockSpec((seq, FOURIER_WIDTH), lambda b: (b, 0)),
        out_shape=jax.ShapeDtypeStruct((batch * seq, FOURIER_WIDTH), BF16),
        compiler_params=_params("parallel"),
        name="pos_dft",
    )(pos_cos, pos_sin, alt, ae, bo, a0)


FFN_CHUNKS = (1536, 1280)
TAIL_SUBTILES = 2


def _tail_kernel(h_ref, a_ref, f_ref, wf_ref, bf_ref, wo_ref, g1_ref, b1_ref, wgu_ref, wd_ref, g2_ref, b2_ref,
                 o_ref):
    sub = h_ref.shape[0] // TAIL_SUBTILES
    groups = [slice(s * sub, (s + 1) * sub) for s in range(TAIL_SUBTILES)]

    def mixed(rows):
        parts = []
        for gi in range(N_FGROUPS):
            cols = slice(gi * FGROUP_DIM, (gi + 1) * FGROUP_DIM)
            parts.append(jnp.dot(f_ref[rows, cols], wf_ref[gi], preferred_element_type=F32))
        y = jnp.concatenate(parts, axis=-1) + bf_ref[...]
        mix = jnp.dot(a_ref[rows, :], wo_ref[:ATT_WIDTH, :], preferred_element_type=F32)
        return mix + jnp.dot(y.astype(BF16), wo_ref[ATT_WIDTH:, :], preferred_element_type=F32)

    def swiglu(h1):
        hb = h1.astype(BF16)
        acc = None
        lo = 0
        for width in FFN_CHUNKS:
            gate = jnp.dot(hb, wgu_ref[:, lo:lo + width], preferred_element_type=F32)
            up = jnp.dot(hb, wgu_ref[:, D_FF + lo:D_FF + lo + width], preferred_element_type=F32)
            act = (gate * jax.nn.sigmoid(gate) * up).astype(BF16)
            part = jnp.dot(act, wd_ref[lo:lo + width, :], preferred_element_type=F32)
            acc = part if acc is None else acc + part
            lo += width
        return acc

    mixes = [mixed(rows) for rows in groups]
    h1s = [_layer_norm(ALPHA * h_ref[rows, :] + mix, g1_ref[...], b1_ref[...]) for rows, mix in zip(groups, mixes)]
    ffns = [swiglu(h1) for h1 in h1s]
    for rows, h1, ffn in zip(groups, h1s, ffns):
        o_ref[rows, :] = _layer_norm(ALPHA * h1 + ffn, g2_ref[...], b2_ref[...])


def _tail_call(h2, a, f, layer, wf_bf, b_f, wo_bf, g1, b1, wgu_bf, wd_bf, g2, b2, tm=512):
    m, d = h2.shape
    assert sum(FFN_CHUNKS) == D_FF
    const = dict(pipeline_mode=pl.Buffered(1))
    row = lambda i: (i, 0)
    fixed = lambda i: (0, 0)
    stacked = lambda i: (layer, 0, 0)
    return pl.pallas_call(
        _tail_kernel,
        grid=(m // tm,),
        in_specs=[pl.BlockSpec((tm, d), row),
                  pl.BlockSpec((tm, ATT_WIDTH), row),
                  pl.BlockSpec((tm, FOURIER_WIDTH), row),
                  pl.BlockSpec((None, N_FGROUPS, FGROUP_DIM, FGROUP_DIM), lambda i: (layer, 0, 0, 0), **const),
                  pl.BlockSpec((1, FOURIER_WIDTH), fixed),
                  pl.BlockSpec((None, d, d), stacked, **const),
                  pl.BlockSpec((1, d), fixed),
                  pl.BlockSpec((1, d), fixed),
                  pl.BlockSpec((None, d, 2 * D_FF), stacked, **const),
                  pl.BlockSpec((None, D_FF, d), stacked, **const),
                  pl.BlockSpec((1, d), fixed),
                  pl.BlockSpec((1, d), fixed)],
        out_specs=pl.BlockSpec((tm, d), row),
        out_shape=jax.ShapeDtypeStruct((m, d), F32),
        compiler_params=_params("parallel"),
        name="mix_ffn",
    )(h2, a, f, wf_bf, b_f.reshape(1, -1), wo_bf, g1.reshape(1, d), b1.reshape(1, d),
      wgu_bf, wd_bf, g2.reshape(1, d), b2.reshape(1, d))


def kernel(x, ln_in_g, ln_in_b, w_in, lam_params, subln_g, w_f, b_f, w_o, ln1_g, ln1_b, w_gu, w_down, ln2_g, ln2_b):
    batch, seq, d = x.shape
    assert d == D_MODEL and seq % (2 * ATT_TQ) == 0
    chan_tab, pos_cos, pos_sin, pos_alt = _dft_tables(seq)
    pos_cols = _position_columns(seq)
    w_in_bf, w_f_bf, w_o_bf = w_in.astype(BF16), w_f.astype(BF16), w_o.astype(BF16)
    w_gu_bf, w_down_bf = w_gu.astype(BF16), w_down.astype(BF16)
    h = x.reshape(batch * seq, d)
    for l in range(DEPTH):
        if l == 0:
            h, q, k, vtc, nrm, u = _proj_call(h, w_in_bf, l, batch, seq, ln=(ln_in_g, ln_in_b))
        else:
            q, k, vtc, nrm, u = _proj_call(h, w_in_bf, l, batch, seq)
        a = _attn_call(q, k, pos_cols, vtc, nrm, lam_params[l], subln_g[l], batch, seq, _lambda_init(l))
        fe, fo, f0 = _fold_call(u, chan_tab, batch, seq)
        f = _pos_dft_call(pos_cos, pos_sin, pos_alt, fe, fo, f0, batch, seq)
        h = _tail_call(h, a, f, l, w_f_bf, b_f[l], w_o_bf, ln1_g[l], ln1_b[l],
                       w_gu_bf, w_down_bf, ln2_g[l], ln2_b[l])
    return h.reshape(batch, seq, d)
```

```python
import functools
import math

import jax
import jax.numpy as jnp
from jax import lax
from jax.experimental import pallas as pl
from jax.experimental.pallas import tpu as pltpu

D_MODEL = 1024
DEPTH = 2
ATT_WIDTH = 512
FOURIER_WIDTH = 512
N_HEADS = 8
HEAD_DIM = 32
V_HEAD_DIM = 64
N_FGROUPS = 4
FGROUP_DIM = 128
D_FF = 2816
LN_EPS = 1e-5
SUBLN_EPS = 1e-5
ALPHA = (2.0 * DEPTH) ** 0.25
LOG2E = math.log2(math.e)

LANES = 128
F32_SUBLANES = 8
BF16_SUBLANES = 16
VMEM_LIMIT = 56 * 1024 * 1024

BF16 = jnp.bfloat16
F32 = jnp.float32


def _lambda_init(layer_idx):
    return 0.8 - 0.6 * math.exp(-0.3 * layer_idx)


def _layer_norm(x, g, b):
    mu = jnp.mean(x, axis=-1, keepdims=True)
    xc = x - mu
    var = jnp.mean(xc * xc, axis=-1, keepdims=True)
    return xc * lax.rsqrt(var + LN_EPS) * g + b


def _params(*sem):
    return pltpu.CompilerParams(dimension_semantics=sem, vmem_limit_bytes=VMEM_LIMIT)


def _split3(x):
    hi = x.astype(BF16).astype(F32)
    r = x - hi
    mid = r.astype(BF16).astype(F32)
    lo = (r - mid).astype(BF16).astype(F32)
    return hi, mid, lo


ATT_TQ = 256
ATT_RANGE = 8
HEADS_PER_STEP = LANES // V_HEAD_DIM
CHAINS_PER_STEP = 2 * HEADS_PER_STEP
VT_ROWS = 128
KP_WIDTH = 2 * LANES
NORM_ROUND_UP = 1.05
NORM_ROWS = F32_SUBLANES
MAX_UNSHIFTED_SCORE = 50.0


def _position_columns(seq):
    def keep_bf16_bits(x):
        bits = lax.bitcast_convert_type(x, jnp.uint32) & jnp.uint32(0xFFFF0000)
        return lax.bitcast_convert_type(bits, F32)

    pos = lax.broadcasted_iota(jnp.int32, (seq, 1), 0)
    full = (pos - seq // 2).astype(F32) * LOG2E
    hi = keep_bf16_bits(full)
    mid = keep_bf16_bits(full - hi)
    lo = keep_bf16_bits(full - hi - mid)
    lane = lax.broadcasted_iota(jnp.int32, (seq, LANES), 1)
    aux = jnp.where(lane < 3, 1.0,
                    jnp.where(lane == 3, hi, jnp.where(lane == 4, mid, jnp.where(lane == 5, lo, 0.0))))
    return aux.astype(BF16)


def _slice_norms(x_bf):
    x32 = x_bf.astype(F32)
    seg = lax.broadcasted_iota(jnp.int32, (ATT_WIDTH, LANES), 0) // HEAD_DIM
    col = lax.broadcasted_iota(jnp.int32, (ATT_WIDTH, LANES), 1)
    ind = jnp.where(seg == col, 1.0, 0.0).astype(BF16)
    return jnp.dot((x32 * x32).astype(BF16), ind, preferred_element_type=F32) * NORM_ROUND_UP


def _row_max(x, groups=F32_SUBLANES):
    r, c = x.shape
    return jnp.max(jnp.max(x.reshape(groups, r // groups, c), axis=0), axis=0, keepdims=True)


def _proj_kernel(*refs, tm, pre_ln):
    if pre_ln:
        x_ref, g_ref, b_ref, w_ref, h_ref, q_ref, k_ref, vtc_ref, nrm_ref, u_ref = refs
        h = _layer_norm(x_ref[...], g_ref[...], b_ref[...])
        h_ref[...] = h
    else:
        x_ref, w_ref, q_ref, k_ref, vtc_ref, nrm_ref, u_ref = refs
        h = x_ref[...]
    hb = h.astype(BF16)
    qf = jnp.dot(hb, w_ref[:, 0:ATT_WIDTH], preferred_element_type=F32)
    qb = (qf * (HEAD_DIM ** -0.5 * LOG2E)).astype(BF16)
    q_ref[...] = qb

    kf = jnp.dot(hb, w_ref[:, ATT_WIDTH:2 * ATT_WIDTH], preferred_element_type=F32).astype(BF16)
    k_ref[...] = kf

    nrm_ref[0:NORM_ROWS, :] = jnp.broadcast_to(_row_max(_slice_norms(qb)), (NORM_ROWS, LANES))
    nrm_ref[NORM_ROWS:, :] = jnp.broadcast_to(_row_max(_slice_norms(kf)), (NORM_ROWS, LANES))

    vf = jnp.dot(hb, w_ref[:, 2 * ATT_WIDTH:3 * ATT_WIDTH], preferred_element_type=F32)
    vtr = vf.T.astype(BF16)
    for cc in range(tm // ATT_TQ):
        vtc_ref[cc] = vtr[:, cc * ATT_TQ:(cc + 1) * ATT_TQ]

    u_ref[...] = jnp.dot(hb, w_ref[:, 3 * ATT_WIDTH:], preferred_element_type=F32).astype(BF16)


def _proj_call(x2, w_bf, layer, batch, seq, ln=None, tm=1024):
    m, d = x2.shape
    n = w_bf.shape[2]
    tiles_per_seq = seq // tm
    kern = functools.partial(_proj_kernel, tm=tm, pre_ln=ln is not None)
    row = lambda i: (i, 0)
    fixed = lambda i: (0, 0)
    in_specs = [pl.BlockSpec((tm, d), row), pl.BlockSpec((None, d, n), lambda i: (layer, 0, 0))]
    args = [x2, w_bf]
    out_specs = [pl.BlockSpec((tm, ATT_WIDTH), row),
                 pl.BlockSpec((tm, ATT_WIDTH), row),
                 pl.BlockSpec((None, tm // ATT_TQ, ATT_WIDTH, ATT_TQ),
                              lambda i: (i // tiles_per_seq, i % tiles_per_seq, 0, 0)),
                 pl.BlockSpec((None, 2 * NORM_ROWS, LANES), lambda i: (i, 0, 0)),
                 pl.BlockSpec((tm, FOURIER_WIDTH), row)]
    out_shape = [jax.ShapeDtypeStruct((m, ATT_WIDTH), BF16),
                 jax.ShapeDtypeStruct((m, ATT_WIDTH), BF16),
                 jax.ShapeDtypeStruct((batch, seq // ATT_TQ, ATT_WIDTH, ATT_TQ), BF16),
                 jax.ShapeDtypeStruct((m // tm, 2 * NORM_ROWS, LANES), F32),
                 jax.ShapeDtypeStruct((m, FOURIER_WIDTH), BF16)]
    if ln is not None:
        in_specs[1:1] = [pl.BlockSpec((1, d), fixed), pl.BlockSpec((1, d), fixed)]
        args[1:1] = [ln[0].reshape(1, d), ln[1].reshape(1, d)]
        out_specs.insert(0, pl.BlockSpec((tm, d), row))
        out_shape.insert(0, jax.ShapeDtypeStruct((m, d), F32))
    return pl.pallas_call(
        kern,
        grid=(m // tm,),
        in_specs=in_specs,
        out_specs=out_specs,
        out_shape=out_shape,
        compiler_params=_params("parallel"),
        name="proj",
    )(*args)


def _attn_kernel(lam_ref, g_ref, nrm_ref, q_ref, k_ref, aux_ref, vtc_ref, o_ref, ks_ref, p_even_ref, p_odd_ref, flag_ref,
                 *,
                 lam_init, seq):
    pair = pl.program_id(1)
    ntiles = seq // ATT_TQ
    last = ntiles - 1
    last_rows = slice(last * ATT_TQ, seq)

    def tile_rows(tile):
        return pl.ds(pl.multiple_of(tile * ATT_TQ, ATT_TQ), ATT_TQ)

    def enter_tile(tile):
        prev = tile_rows(jnp.maximum(tile - 1, 0))
        ks_ref[prev, :LANES] = k_ref[prev, :]
        ks_ref[prev, LANES:] = aux_ref[prev, :]
        ks_ref[tile_rows(tile), :LANES] = k_ref[last_rows, :]
        ks_ref[tile_rows(tile), LANES:] = -aux_ref[last_rows, :]

    ks_ref[:, :LANES] = k_ref[...]
    ks_ref[:, LANES:] = -aux_ref[...]
    nrm = jnp.max(nrm_ref[...], axis=0)
    lane = lax.broadcasted_iota(jnp.int32, (1, LANES), 1)
    prod = jnp.where(lane // CHAINS_PER_STEP == pair, nrm[0:1, :] * nrm[NORM_ROWS:NORM_ROWS + 1, :], 0.0)
    flag_ref[0] = (jnp.max(prod) <= MAX_UNSHIFTED_SCORE ** 2).astype(jnp.int32)

    lp = lam_ref[...]
    lam = (jnp.exp(jnp.sum(lp[0:1] * lp[1:2], axis=-1, keepdims=True))
           - jnp.exp(jnp.sum(lp[2:3] * lp[3:4], axis=-1, keepdims=True)) + lam_init)

    rr = lax.broadcasted_iota(jnp.int32, (ATT_TQ, ATT_TQ), 0)
    cc = lax.broadcasted_iota(jnp.int32, (ATT_TQ, ATT_TQ), 1)
    diag_dist = jnp.abs(rr - cc).astype(F32) * LOG2E
    diag_dist = jnp.concatenate([diag_dist, diag_dist], axis=1)
    row_q = lax.broadcasted_iota(jnp.int32, (LANES, ATT_TQ), 0)
    ones_rows = jnp.ones((VT_ROWS - V_HEAD_DIM, ATT_TQ), BF16)
    g_col = g_ref[...]

    def tile_operands(tile):
        q_t = q_ref[tile_rows(tile), :].astype(F32).T
        pos = tile * ATT_TQ + lax.broadcasted_iota(jnp.int32, (1, ATT_TQ), 1)
        hi, mid, lo = _split3((pos - seq // 2).astype(F32) * LOG2E)
        base = jnp.where(row_q == 0, -hi,
                         jnp.where(row_q == 1, -mid,
                                   jnp.where(row_q == 2, -lo, jnp.where(row_q < 6, 1.0, 0.0))))
        return q_t, base

    def stage_operands(tile, hh, q_t, base):
        head = pair * HEADS_PER_STEP + hh
        slope = jnp.exp2(-jnp.full((1, 1), head + 1, jnp.int32).astype(F32))
        aux = (base * slope).astype(BF16)
        qm = jnp.concatenate([jnp.where(row_q // HEAD_DIM == hh * 2 + c, q_t, 0.0).astype(BF16) for c in range(2)],
                             axis=1)
        w_main = jnp.concatenate([qm, jnp.concatenate([aux, aux], axis=1)], axis=0)
        sd = jnp.dot(k_ref[tile_rows(tile), :], qm, preferred_element_type=F32)
        sd = sd - slope * diag_dist
        return w_main, sd

    def values_lhs(chunk, hh):
        return jnp.concatenate([vtc_ref[chunk, hh * V_HEAD_DIM:(hh + 1) * V_HEAD_DIM, :], ones_rows], axis=0)

    key_ranges = [(lo, min(lo + ATT_RANGE, last)) for lo in range(0, last, ATT_RANGE)]

    def attend(tile, q_t, base):
        accs = []
        for hh in range(HEADS_PER_STEP):
            w_main, sd = stage_operands(tile, hh, q_t, base)
            sts = [jnp.dot(ks_ref[lo * ATT_TQ:hi * ATT_TQ, :], w_main, preferred_element_type=F32)
                   for lo, hi in key_ranges]
            m = jnp.max(sd, axis=0, keepdims=True)
            for st in sts:
                m = jnp.maximum(m, jnp.max(st, axis=0, keepdims=True))
            acc = jnp.dot(values_lhs(tile, hh), jnp.exp2(sd - m).astype(BF16), preferred_element_type=F32)
            for (lo, _), st in zip(key_ranges, sts):
                p = jnp.exp2(st - m).astype(BF16)
                for j in range(st.shape[0] // ATT_TQ):
                    blk = lo + j
                    chunk = jnp.where(tile == blk, last, blk)
                    acc = acc + jnp.dot(values_lhs(chunk, hh), p[j * ATT_TQ:(j + 1) * ATT_TQ, :],
                                        preferred_element_type=F32)
            accs.append(acc)
        return tuple(accs)

    def tile_scores(tile, p_ref):
        q_t, base = tile_operands(tile)
        for hh in range(HEADS_PER_STEP):
            w_main, sd = stage_operands(tile, hh, q_t, base)
            for lo, hi in key_ranges:
                st = jnp.dot(ks_ref[lo * ATT_TQ:hi * ATT_TQ, :], w_main, preferred_element_type=F32)
                p_ref[hh, lo * ATT_TQ:hi * ATT_TQ, :] = jnp.exp2(st).astype(BF16)
            p_ref[hh, last_rows, :] = jnp.exp2(sd).astype(BF16)

    def tile_values(tile, p_ref):
        accs = []
        for hh in range(HEADS_PER_STEP):
            acc = jnp.dot(values_lhs(tile, hh), p_ref[hh, last_rows, :], preferred_element_type=F32)
            for blk in range(last):
                chunk = jnp.where(tile == blk, last, blk)
                acc = acc + jnp.dot(values_lhs(chunk, hh), p_ref[hh, blk * ATT_TQ:(blk + 1) * ATT_TQ, :],
                                    preferred_element_type=F32)
            accs.append(acc)
        return tuple(accs)

    def finish(tile, accs):
        outs = []
        for acc in accs:
            norm = []
            for c in range(2):
                part = acc[:, c * ATT_TQ:(c + 1) * ATT_TQ]
                denom = part[V_HEAD_DIM:V_HEAD_DIM + 1, :]
                norm.append(part[:V_HEAD_DIM, :] * (1.0 / denom))
            a_t = norm[0] - lam * norm[1]
            a_t = a_t * lax.rsqrt(jnp.mean(a_t * a_t, axis=0, keepdims=True) + SUBLN_EPS)
            outs.append(a_t * g_col * (1.0 - lam_init))
        o_ref[tile_rows(tile), :] = jnp.concatenate(outs, axis=0).T.astype(o_ref.dtype)

    dummy = tuple(jnp.ones((VT_ROWS, 2 * ATT_TQ), F32) for _ in range(HEADS_PER_STEP))

    @pl.when(flag_ref[0] == 1)
    def _():
        def step(tile, accs, p_new, p_old):
            finish(jnp.maximum(tile - 2, 0), accs)
            enter_tile(tile)
            tile_scores(tile, p_new)
            return tile_values(tile - 1, p_old)

        enter_tile(0)
        tile_scores(0, p_even_ref)

        def body(i, accs):
            odd = 2 * i + 1
            accs = step(odd, accs, p_odd_ref, p_even_ref)
            return step(odd + 1, accs, p_even_ref, p_odd_ref)

        accs = lax.fori_loop(0, (ntiles - 2) // 2, body, dummy)
        accs = step(last, accs, p_odd_ref, p_even_ref)
        finish(last - 1, accs)
        finish(last, tile_values(last, p_odd_ref))

    @pl.when(flag_ref[0] != 1)
    def _():
        def body(tile, accs):
            finish(jnp.maximum(tile - 1, 0), accs)
            enter_tile(tile)
            q_t, base = tile_operands(tile)
            return attend(tile, q_t, base)

        finish(last, lax.fori_loop(0, ntiles, body, dummy))


def _attn_call(q, k, pos_cols, vtc, nrm, lam_params, subln_g, batch, seq, lam_init):
    n_pairs = N_HEADS // HEADS_PER_STEP
    tiles_per_seq = nrm.shape[0] // batch
    kern = functools.partial(_attn_kernel, lam_init=lam_init, seq=seq)
    pair_rows = HEADS_PER_STEP * V_HEAD_DIM
    return pl.pallas_call(
        kern,
        grid=(batch, n_pairs),
        in_specs=[pl.BlockSpec((4, HEAD_DIM), lambda b, p: (0, 0)),
                  pl.BlockSpec((V_HEAD_DIM, 1), lambda b, p: (0, 0)),
                  pl.BlockSpec((tiles_per_seq, 2 * NORM_ROWS, LANES), lambda b, p: (b, 0, 0)),
                  pl.BlockSpec((seq, LANES), lambda b, p: (b, p)),
                  pl.BlockSpec((seq, LANES), lambda b, p: (b, p)),
                  pl.BlockSpec((seq, LANES), lambda b, p: (0, 0)),
                  pl.BlockSpec((None, seq // ATT_TQ, pair_rows, ATT_TQ), lambda b, p: (b, 0, p, 0))],
        out_specs=pl.BlockSpec((seq, LANES), lambda b, p: (b, p)),
        out_shape=jax.ShapeDtypeStruct((batch * seq, ATT_WIDTH), BF16),
        scratch_shapes=[pltpu.VMEM((seq, KP_WIDTH), BF16),
                        pltpu.VMEM((HEADS_PER_STEP, seq, 2 * ATT_TQ), BF16),
                        pltpu.VMEM((HEADS_PER_STEP, seq, 2 * ATT_TQ), BF16),
                        pltpu.SMEM((1,), jnp.int32)],
        compiler_params=_params("parallel", "parallel"),
        name="diff_attn",
    )(lam_params, subln_g.reshape(V_HEAD_DIM, 1), nrm, q, k, pos_cols, vtc)


FOLD_BLK = 256
TABLE_SPLIT = 64


def _dft_tables(seq):
    c = FGROUP_DIM
    ci = lax.broadcasted_iota(jnp.int32, (c, c), 0) * lax.broadcasted_iota(jnp.int32, (c, c), 1) % c
    ang_c = ci.astype(F32) * (2.0 * math.pi / c)
    cc, sc = jnp.cos(ang_c) * c ** -0.5, jnp.sin(ang_c) * c ** -0.5
    chan = jnp.concatenate([jnp.concatenate([cc, sc], axis=1), jnp.concatenate([cc, -sc], axis=1)], axis=0)
    half = seq // 2
    n = lax.broadcasted_iota(jnp.int32, (1, half), 1) + 1
    weight = jnp.where(n == half, 0.5, 1.0) * seq ** -0.5

    def thin(count, stride):
        kk = lax.broadcasted_iota(jnp.int32, (count, half), 0) * stride
        ang = (kk * n % seq).astype(F32) * (2.0 * math.pi / seq)
        return jnp.cos(ang), jnp.sin(ang)

    ca, sa = thin(half // TABLE_SPLIT, TABLE_SPLIT)
    cb, sb = thin(TABLE_SPLIT, 1)
    cb, sb = cb * weight, sb * weight
    pos_cos = (ca[:, None, :] * cb[None] - sa[:, None, :] * sb[None]).reshape(half, half).astype(BF16)
    pos_sin = (sa[:, None, :] * cb[None] + ca[:, None, :] * sb[None]).reshape(half, half).astype(BF16)
    n1 = lax.broadcasted_iota(jnp.int32, (BF16_SUBLANES, half), 1) + 1
    row = lax.broadcasted_iota(jnp.int32, (BF16_SUBLANES, half), 0)
    alt = jnp.where(row == 0, jnp.where(n1 % 2 == 0, 1.0, -1.0), 0.0)
    alt = (alt * jnp.where(n1 == half, 0.5, 1.0) * seq ** -0.5).astype(BF16)
    return chan.astype(BF16), pos_cos, pos_sin, alt


def _select(rows, cols, offset):
    r = lax.broadcasted_iota(jnp.int32, (rows, cols), 0)
    c = lax.broadcasted_iota(jnp.int32, (rows, cols), 1)
    return jnp.where(c == offset - r, 1.0, 0.0).astype(BF16)


def _fold_kernel(u_ref, t_ref, ae_ref, bo_ref, a0_ref, *, seq):
    half = seq // 2
    t = t_ref[...]
    r = lax.broadcasted_iota(jnp.int32, (FOLD_BLK, FOLD_BLK), 0)
    c = lax.broadcasted_iota(jnp.int32, (FOLD_BLK, FOLD_BLK), 1)
    shift_one = jnp.where(c == r + 1, 1.0, 0.0).astype(BF16)
    last_row = lax.broadcasted_iota(jnp.int32, (FOLD_BLK, FOURIER_WIDTH), 0) == FOLD_BLK - 1
    reverse = _select(FOLD_BLK, FOLD_BLK, FOLD_BLK - 1)
    for i in range(half // FOLD_BLK):
        lo = i * FOLD_BLK
        nxt = u_ref[lo + FOLD_BLK:lo + FOLD_BLK + BF16_SUBLANES, :][0:1, :].astype(F32)
        fwd = jnp.where(last_row, nxt, jnp.dot(shift_one, u_ref[lo:lo + FOLD_BLK, :], preferred_element_type=F32))
        fwd = fwd.astype(BF16)
        hi = seq - lo - FOLD_BLK
        bwd = jnp.dot(reverse, u_ref[hi:hi + FOLD_BLK, :], preferred_element_type=F32).astype(BF16)
        for gi in range(N_FGROUPS):
            cols = slice(gi * FGROUP_DIM, (gi + 1) * FGROUP_DIM)
            both = jnp.concatenate([fwd[:, cols], bwd[:, cols]], axis=1)
            res = jnp.dot(both, t, preferred_element_type=F32)
            ae_ref[lo:lo + FOLD_BLK, cols] = res[:, :FGROUP_DIM].astype(ae_ref.dtype)
            bo_ref[lo:lo + FOLD_BLK, cols] = res[:, FGROUP_DIM:].astype(bo_ref.dtype)
    for gi in range(N_FGROUPS):
        cols = slice(gi * FGROUP_DIM, (gi + 1) * FGROUP_DIM)
        first = jnp.dot(u_ref[0:BF16_SUBLANES, cols], t[:FGROUP_DIM, :FGROUP_DIM], preferred_element_type=F32)
        a0_ref[:, cols] = jnp.broadcast_to(first[0:1, :], (a0_ref.shape[0], FGROUP_DIM))


def _fold_call(u, chan_tab, batch, seq):
    half = seq // 2
    out = jax.ShapeDtypeStruct((batch * half, FOURIER_WIDTH), BF16)
    return pl.pallas_call(
        functools.partial(_fold_kernel, seq=seq),
        grid=(batch,),
        in_specs=[pl.BlockSpec((seq, FOURIER_WIDTH), lambda b: (b, 0)),
                  pl.BlockSpec((2 * FGROUP_DIM, 2 * FGROUP_DIM), lambda b: (0, 0))],
        out_specs=[pl.BlockSpec((half, FOURIER_WIDTH), lambda b: (b, 0)),
                   pl.BlockSpec((half, FOURIER_WIDTH), lambda b: (b, 0)),
                   pl.BlockSpec((None, F32_SUBLANES, FOURIER_WIDTH), lambda b: (b, 0, 0))],
        out_shape=[out, out, jax.ShapeDtypeStruct((batch, F32_SUBLANES, FOURIER_WIDTH), F32)],
        compiler_params=_params("parallel"),
        name="dft_fold",
    )(u, chan_tab)


def _pos_dft_kernel(c_ref, s_ref, alt_ref, ae_ref, bo_ref, a0_ref, o_ref, *, seq):
    half = seq // 2
    ae = ae_ref[...]
    dc = a0_ref[0:1, :] * seq ** -0.5
    even = jnp.dot(c_ref[...], ae, preferred_element_type=F32) + dc
    odd = jnp.dot(s_ref[...], bo_ref[...], preferred_element_type=F32)
    o_ref[0:half, :] = (even - odd).astype(o_ref.dtype)
    mirror = (even + odd).astype(BF16)
    middle = jnp.dot(alt_ref[...], ae, preferred_element_type=F32)[0:1, :] + dc
    take = _select(FOLD_BLK, FOLD_BLK, FOLD_BLK)
    first_row = lax.broadcasted_iota(jnp.int32, (FOLD_BLK, FOURIER_WIDTH), 0) == 0
    nblk = half // FOLD_BLK
    for bt in range(nblk):
        head = middle if bt == 0 else mirror[(nblk - bt) * FOLD_BLK:(nblk - bt) * FOLD_BLK + 1, :].astype(F32)
        lower = mirror[(nblk - 1 - bt) * FOLD_BLK:(nblk - bt) * FOLD_BLK, :]
        blk = jnp.where(first_row, head, jnp.dot(take, lower, preferred_element_type=F32))
        o_ref[half + bt * FOLD_BLK:half + (bt + 1) * FOLD_BLK, :] = blk.astype(o_ref.dtype)


def _pos_dft_call(pos_cos, pos_sin, alt, ae, bo, a0, batch, seq):
    half = seq // 2
    const = dict(pipeline_mode=pl.Buffered(1))
    return pl.pallas_call(
        functools.partial(_pos_dft_kernel, seq=seq),
        grid=(batch,),
        in_specs=[pl.BlockSpec((half, half), lambda b: (0, 0), **const),
                  pl.BlockSpec((half, half), lambda b: (0, 0), **const),
                  pl.BlockSpec((BF16_SUBLANES, half), lambda b: (0, 0)),
                  pl.BlockSpec((half, FOURIER_WIDTH), lambda b: (b, 0)),
                  pl.BlockSpec((half, FOURIER_WIDTH), lambda b: (b, 0)),
                  pl.BlockSpec((None, F32_SUBLANES, FOURIER_WIDTH), lambda b: (b, 0, 0))],
        out_specs=pl.BlockSpec((seq, FOURIER_WIDTH), lambda b: (b, 0)),
        out_shape=jax.ShapeDtypeStruct((batch * seq, FOURIER_WIDTH), BF16),
        compiler_params=_params("parallel"),
        name="pos_dft",
    )(pos_cos, pos_sin, alt, ae, bo, a0)


FFN_CHUNKS = (1536, 1280)
TAIL_SUBTILES = 2


def _tail_kernel(h_ref, a_ref, f_ref, wf_ref, bf_ref, wo_ref, g1_ref, b1_ref, wgu_ref, wd_ref, g2_ref, b2_ref,
                 o_ref, *ob_refs):
    sub = h_ref.shape[0] // TAIL_SUBTILES
    groups = [slice(s * sub, (s + 1) * sub) for s in range(TAIL_SUBTILES)]

    def mixed(rows):
        parts = []
        for gi in range(N_FGROUPS):
            cols = slice(gi * FGROUP_DIM, (gi + 1) * FGROUP_DIM)
            parts.append(jnp.dot(f_ref[rows, cols], wf_ref[gi], preferred_element_type=F32))
        y = jnp.concatenate(parts, axis=-1) + bf_ref[...]
        mix = jnp.dot(a_ref[rows, :], wo_ref[:ATT_WIDTH, :], preferred_element_type=F32)
        return mix + jnp.dot(y.astype(BF16), wo_ref[ATT_WIDTH:, :], preferred_element_type=F32)

    def swiglu(h1):
        hb = h1.astype(BF16)
        acc = None
        lo = 0
        for width in FFN_CHUNKS:
            gate = jnp.dot(hb, wgu_ref[:, lo:lo + width], preferred_element_type=F32)
            up = jnp.dot(hb, wgu_ref[:, D_FF + lo:D_FF + lo + width], preferred_element_type=F32)
            act = (gate * jax.nn.sigmoid(gate) * up).astype(BF16)
            part = jnp.dot(act, wd_ref[lo:lo + width, :], preferred_element_type=F32)
            acc = part if acc is None else acc + part
            lo += width
        return acc

    mixes = [mixed(rows) for rows in groups]
    h1s = [_layer_norm(ALPHA * h_ref[rows, :] + mix, g1_ref[...], b1_ref[...]) for rows, mix in zip(groups, mixes)]
    ffns = [swiglu(h1) for h1 in h1s]
    for rows, h1, ffn in zip(groups, h1s, ffns):
        out = _layer_norm(ALPHA * h1 + ffn, g2_ref[...], b2_ref[...])
        o_ref[rows, :] = out
        for ob_ref in ob_refs:
            ob_ref[rows, :] = out.astype(BF16)


def _tail_call(h2, a, f, layer, wf_bf, b_f, wo_bf, g1, b1, wgu_bf, wd_bf, g2, b2, with_bf16_copy, tm=512):
    m, d = h2.shape
    assert sum(FFN_CHUNKS) == D_FF
    const = dict(pipeline_mode=pl.Buffered(1))
    row = lambda i: (i, 0)
    fixed = lambda i: (0, 0)
    stacked = lambda i: (layer, 0, 0)
    return pl.pallas_call(
        _tail_kernel,
        grid=(m // tm,),
        in_specs=[pl.BlockSpec((tm, d), row),
                  pl.BlockSpec((tm, ATT_WIDTH), row),
                  pl.BlockSpec((tm, FOURIER_WIDTH), row),
                  pl.BlockSpec((None, N_FGROUPS, FGROUP_DIM, FGROUP_DIM), lambda i: (layer, 0, 0, 0), **const),
                  pl.BlockSpec((1, FOURIER_WIDTH), fixed),
                  pl.BlockSpec((None, d, d), stacked, **const),
                  pl.BlockSpec((1, d), fixed),
                  pl.BlockSpec((1, d), fixed),
                  pl.BlockSpec((None, d, 2 * D_FF), stacked, **const),
                  pl.BlockSpec((None, D_FF, d), stacked, **const),
                  pl.BlockSpec((1, d), fixed),
                  pl.BlockSpec((1, d), fixed)],
        out_specs=[pl.BlockSpec((tm, d), row)] * (2 if with_bf16_copy else 1),
        out_shape=[jax.ShapeDtypeStruct((m, d), F32)] + ([jax.ShapeDtypeStruct((m, d), BF16)] if with_bf16_copy else []),
        compiler_params=_params("parallel"),
        name="mix_ffn",
    )(h2, a, f, wf_bf, b_f.reshape(1, -1), wo_bf, g1.reshape(1, d), b1.reshape(1, d),
      wgu_bf, wd_bf, g2.reshape(1, d), b2.reshape(1, d))


def kernel(x, ln_in_g, ln_in_b, w_in, lam_params, subln_g, w_f, b_f, w_o, ln1_g, ln1_b, w_gu, w_down, ln2_g, ln2_b):
    batch, seq, d = x.shape
    assert d == D_MODEL and seq % (2 * ATT_TQ) == 0
    chan_tab, pos_cos, pos_sin, pos_alt = _dft_tables(seq)
    pos_cols = _position_columns(seq)
    w_in_bf, w_f_bf, w_o_bf = w_in.astype(BF16), w_f.astype(BF16), w_o.astype(BF16)
    w_gu_bf, w_down_bf = w_gu.astype(BF16), w_down.astype(BF16)
    h = x.reshape(batch * seq, d)
    for l in range(DEPTH):
        if l == 0:
            h, q, k, vtc, nrm, u = _proj_call(h, w_in_bf, l, batch, seq, ln=(ln_in_g, ln_in_b))
        else:
            q, k, vtc, nrm, u = _proj_call(h_bf, w_in_bf, l, batch, seq)
        a = _attn_call(q, k, pos_cols, vtc, nrm, lam_params[l], subln_g[l], batch, seq, _lambda_init(l))
        fe, fo, f0 = _fold_call(u, chan_tab, batch, seq)
        f = _pos_dft_call(pos_cos, pos_sin, pos_alt, fe, fo, f0, batch, seq)
        outs = _tail_call(h, a, f, l, w_f_bf, b_f[l], w_o_bf, ln1_g[l], ln1_b[l],
                          w_gu_bf, w_down_bf, ln2_g[l], ln2_b[l], with_bf16_copy=l + 1 < DEPTH)
        h, h_bf = outs if l + 1 < DEPTH else (outs[0], None)
    return h.reshape(batch, seq, d)
```
